```python
import math
import jax
import jax.numpy as jnp
from jax import lax
import numpy as np


D_MODEL = 1024
BATCH = 2
SEQ = 8192
DEPTH = 4

N_META = 16
CHUNK = 64
PAD = CHUNK - N_META
HGRN_CHUNK = 16
EPS = 1e-6

HG_HEADS = 4
HG_DK = 64
HG_DV = 128
RET_HEADS = 4
RET_DK = 64
RET_DV = 128
ROPE_BASE = 10000.0
SSM_WIDTH = 512
SSM_GROUP = 16
SSM_GROUPS = SSM_WIDTH // SSM_GROUP
SSM_STATE = 64
GDN_HEADS = 4
GDN_DK = 128
GDN_DV = 128
CONV_K = 4
GDN_QKV = 2 * GDN_HEADS * GDN_DK + GDN_HEADS * GDN_DV
BRANCH_WIDTH = 512
N_BRANCH = 4
D_FF = -(-8 * D_MODEL // (3 * 256)) * 256

IN_SPLITS = (
    HG_HEADS * HG_DK, HG_HEADS * HG_DK, HG_HEADS * HG_DV, HG_HEADS * HG_DV,
    RET_HEADS * RET_DK, RET_HEADS * RET_DK, RET_HEADS * RET_DV, RET_HEADS * RET_DV,
    SSM_WIDTH,
    GDN_QKV, GDN_HEADS, GDN_HEADS, GDN_HEADS * GDN_DV,
    N_BRANCH * D_MODEL,
)
IN_COLS = sum(IN_SPLITS)

kernel_name = "hybrid_gated_parallel_mixers"


def _split(z, sizes):
    offsets = [int(o) for o in np.cumsum(sizes)[:-1]]
    return jnp.split(z, offsets, axis=-1)


def rmsnorm(x, g):
    xf = x.astype(jnp.float32)
    y = xf * lax.rsqrt(jnp.mean(xf * xf, axis=-1, keepdims=True) + EPS)
    return (y * g.astype(jnp.float32)).astype(x.dtype)


def head_rmsnorm(o, g):
    y = o * lax.rsqrt(jnp.mean(o * o, axis=-1, keepdims=True) + EPS)
    return y.reshape(o.shape[0], o.shape[1], -1) * g.astype(jnp.float32)


def head_groupnorm(o, g):
    mu = jnp.mean(o, axis=-1, keepdims=True)
    oc = o - mu
    y = oc * lax.rsqrt(jnp.mean(oc * oc, axis=-1, keepdims=True) + EPS)
    return y.reshape(o.shape[0], o.shape[1], -1) * g.astype(jnp.float32)


def l2norm(t):
    return t * lax.rsqrt(jnp.sum(t * t, axis=-1, keepdims=True) + EPS)


def to_chunks(t, c):
    b, l, h, d = t.shape
    return t.reshape(b, l // c, c, h, d).transpose(1, 0, 3, 2, 4)


def from_chunks(t):
    n, b, h, c, d = t.shape
    return t.transpose(1, 0, 3, 2, 4).reshape(b, n * c, h, d)


def rotary(t, pos):
    half = t.shape[-1] // 2
    inv = ROPE_BASE ** (-jnp.arange(half, dtype=jnp.float32) / half)
    ang = pos[:, None] * inv[None, :]
    cos = jnp.cos(ang)[:, None, :]
    sin = jnp.sin(ang)[:, None, :]
    t1, t2 = t[..., :half], t[..., half:]
    return jnp.concatenate([t1 * cos - t2 * sin, t1 * sin + t2 * cos], axis=-1)


def gla_chunked(q, k, v, log_f, c):
    qc, kc, vc, gc = to_chunks(q, c), to_chunks(k, c), to_chunks(v, c), to_chunks(log_f, c)
    b = jnp.cumsum(gc, axis=3)
    b_last = b[..., -1:, :]
    q_t = qc * jnp.exp(b)
    k_t = kc * jnp.exp(-b)
    causal = jnp.tril(jnp.ones((c, c), dtype=bool))
    att = jnp.where(causal, jnp.einsum('nbhtd,nbhsd->nbhts', q_t, k_t), 0.0)
    o_intra = jnp.einsum('nbhts,nbhse->nbhte', att, vc)
    k_end = kc * jnp.exp(b_last - b)
    decay_end = jnp.exp(b_last[..., 0, :])

    def step(S, xs):
        q_n, k_n, v_n, d_n = xs
        o = jnp.einsum('bhtd,bhde->bhte', q_n, S)
        S = d_n[..., None] * S + jnp.einsum('bhsd,bhse->bhde', k_n, v_n)
        return S, o

    S0 = jnp.zeros((q.shape[0], q.shape[2], q.shape[3], v.shape[3]), jnp.float32)
    _, o_inter = lax.scan(step, S0, (q_t, k_end, vc, decay_end))
    return from_chunks(o_intra + o_inter)


def retention_chunked(q, k, v, c):
    qc, kc, vc = to_chunks(q, c), to_chunks(k, c), to_chunks(v, c)
    log_gamma = jnp.log1p(-jnp.exp2(-5.0 - jnp.arange(RET_HEADS, dtype=jnp.float32)))
    idx = jnp.arange(c, dtype=jnp.float32)
    diff = idx[:, None] - idx[None, :]
    decay = jnp.where(diff >= 0, jnp.exp(log_gamma[:, None, None] * jnp.maximum(diff, 0.0)), 0.0)
    att = jnp.einsum('nbhtd,nbhsd->nbhts', qc, kc) * decay
    o_intra = jnp.einsum('nbhts,nbhse->nbhte', att, vc)
    xi = jnp.exp(log_gamma[:, None] * (idx + 1.0))[:, :, None]
    zeta = jnp.exp(log_gamma[:, None] * (c - 1.0 - idx))[:, :, None]
    gamma_c = jnp.exp(log_gamma * c)[:, None, None]

    def step(R, xs):
        q_n, k_n, v_n = xs
        o = jnp.einsum('bhtd,bhde->bhte', q_n * xi, R)
        R = gamma_c * R + jnp.einsum('bhsd,bhse->bhde', k_n * zeta, v_n)
        return R, o

    R0 = jnp.zeros((q.shape[0], RET_HEADS, RET_DK, RET_DV), jnp.float32)
    _, o_inter = lax.scan(step, R0, (qc, kc, vc))
    return from_chunks(o_intra + o_inter)


def gated_delta_chunked(q, k, v, beta, g, c):
    qc, kc, vc = to_chunks(q, c), to_chunks(k, c), to_chunks(v, c)
    bc = to_chunks(beta[..., None], c)[..., 0]
    cum = jnp.cumsum(to_chunks(g[..., None], c)[..., 0], axis=-1)
    lower = jnp.tril(jnp.ones((c, c), dtype=bool))
    strict = jnp.tril(jnp.ones((c, c), dtype=bool), -1)
    gamma = jnp.exp(jnp.where(lower, cum[..., :, None] - cum[..., None, :], -jnp.inf))
    kb = kc * bc[..., None]
    a0 = jnp.where(strict, jnp.einsum('nbhtd,nbhsd->nbhts', kb, kc) * gamma, 0.0)
    eye = jnp.eye(c, dtype=jnp.float32)
    rhs = jnp.concatenate([vc * bc[..., None], kb * jnp.exp(cum)[..., None]], axis=-1)
    uw = lax.linalg.triangular_solve(a0 + eye, rhs, left_side=True, lower=True, unit_diagonal=True)
    u, w = uw[..., :GDN_DV], uw[..., GDN_DV:]
    att = jnp.einsum('nbhtd,nbhsd->nbhts', qc, kc) * gamma
    qd = qc * jnp.exp(cum)[..., None]
    kd = kc * jnp.exp(cum[..., -1:] - cum)[..., None]
    dl = jnp.exp(cum[..., -1])

    def step(S, xs):
        qd_n, att_n, u_n, w_n, kd_n, dl_n = xs
        v_new = u_n - jnp.einsum('bhtd,bhde->bhte', w_n, S)
        o = jnp.einsum('bhtd,bhde->bhte', qd_n, S) + jnp.einsum('bhts,bhse->bhte', att_n, v_new)
        S = dl_n[..., None, None] * S + jnp.einsum('bhsd,bhse->bhde', kd_n, v_new)
        return S, o

    S0 = jnp.zeros((q.shape[0], GDN_HEADS, GDN_DK, GDN_DV), jnp.float32)
    _, o = lax.scan(step, S0, (qd, att, u, w, kd, dl))
    return from_chunks(o)


def _complex_affine_combine(e1, e2):
    a1r, a1i, b1r, b1i = e1
    a2r, a2i, b2r, b2i = e2
    ar = a2r * a1r - a2i * a1i
    ai = a2r * a1i + a2i * a1r
    br = a2r * b1r - a2i * b1i + b2r
    bi = a2r * b1i + a2i * b1r + b2i
    return ar, ai, br, bi


def hgrn2_branch(q_raw, f_raw, i_raw, g_raw, lb, norm_g):
    bsz, L, _ = q_raw.shape
    z = f_raw.astype(jnp.float32).reshape(bsz, L, HG_HEADS, HG_DK)
    lb = lb.reshape(HG_HEADS, HG_DK)
    log_f = jnp.logaddexp(jnp.log(lb), jnp.log1p(-lb) + jax.nn.log_sigmoid(z))
    k = (1.0 - lb) * jax.nn.sigmoid(-z)
    q = jax.nn.silu(q_raw.astype(jnp.float32)).reshape(bsz, L, HG_HEADS, HG_DK) * HG_DK ** -0.5
    v = i_raw.astype(jnp.float32).reshape(bsz, L, HG_HEADS, HG_DV)
    o = gla_chunked(q, k, v, log_f, HGRN_CHUNK)
    o = head_rmsnorm(o, norm_g) * jax.nn.silu(g_raw.astype(jnp.float32))
    return o.astype(q_raw.dtype)


def retention_branch(q_raw, k_raw, v_raw, g_raw, pos, norm_g):
    bsz, L, _ = q_raw.shape
    q = rotary(q_raw.astype(jnp.float32).reshape(bsz, L, RET_HEADS, RET_DK), pos)
    k = rotary(k_raw.astype(jnp.float32).reshape(bsz, L, RET_HEADS, RET_DK), pos) * RET_DK ** -0.5
    v = v_raw.astype(jnp.float32).reshape(bsz, L, RET_HEADS, RET_DV)
    o = retention_chunked(q, k, v, CHUNK)
    o = head_groupnorm(o, norm_g) * jax.nn.silu(g_raw.astype(jnp.float32))
    return o.astype(q_raw.dtype)


def s5_branch(u, a_re, a_im, log_dt, b_re, b_im, c_re, c_im, d_skip, w_glu):
    bsz, L, _ = u.shape
    f32 = jnp.float32
    uf = u.astype(f32).reshape(bsz, L, SSM_GROUPS, SSM_GROUP)
    a_re, a_im = a_re.astype(f32), a_im.astype(f32)
    b_re, b_im, c_re, c_im = b_re.astype(f32), b_im.astype(f32), c_re.astype(f32), c_im.astype(f32)
    dt = jnp.exp(log_dt.astype(f32))[:, None]
    mag = jnp.exp(dt * a_re)
    abar_re = mag * jnp.cos(dt * a_im)
    abar_im = mag * jnp.sin(dt * a_im)
    den = a_re * a_re + a_im * a_im
    nr, ni = abar_re - 1.0, abar_im
    coef_re = (nr * a_re + ni * a_im) / den
    coef_im = (ni * a_re - nr * a_im) / den
    bbar_re = coef_re[..., None] * b_re - coef_im[..., None] * b_im
    bbar_im = coef_re[..., None] * b_im + coef_im[..., None] * b_re
    bu_re = jnp.einsum('blgp,gnp->lbgn', uf, bbar_re)
    bu_im = jnp.einsum('blgp,gnp->lbgn', uf, bbar_im)
    shape = (L, 1, SSM_GROUPS, SSM_STATE)
    a_r = jnp.broadcast_to(abar_re, shape)
    a_i = jnp.broadcast_to(abar_im, shape)
    _, _, x_re, x_im = lax.associative_scan(_complex_affine_combine, (a_r, a_i, bu_re, bu_im), axis=0)
    y = jnp.einsum('lbgn,gpn->blgp', x_re, c_re) - jnp.einsum('lbgn,gpn->blgp', x_im, c_im)
    y = y.reshape(bsz, L, SSM_WIDTH) + d_skip.astype(f32) * u.astype(f32)
    y = jax.nn.gelu(y)
    y = y * jax.nn.sigmoid(y @ w_glu.astype(f32))
    return y.astype(u.dtype)


def causal_depthwise_conv(x, w):
    return lax.conv_general_dilated(x, w[:, None, :], window_strides=(1,), padding=[(CONV_K - 1, 0)],
                                    dimension_numbers=('NWC', 'WIO', 'NWC'), feature_group_count=x.shape[-1])


def gdn_branch(qkv_raw, b_raw, a_raw, g_raw, conv_w, a_log, dt_bias, norm_g):
    bsz, L, _ = qkv_raw.shape
    f32 = jnp.float32
    qkv = jax.nn.silu(causal_depthwise_conv(qkv_raw, conv_w.astype(qkv_raw.dtype))).astype(f32)
    q, k, v = jnp.split(qkv, [GDN_HEADS * GDN_DK, 2 * GDN_HEADS * GDN_DK], axis=-1)
    q = l2norm(q.reshape(bsz, L, GDN_HEADS, GDN_DK)) * GDN_DK ** -0.5
    k = l2norm(k.reshape(bsz, L, GDN_HEADS, GDN_DK))
    v = v.reshape(bsz, L, GDN_HEADS, GDN_DV)
    beta = jax.nn.sigmoid(b_raw.astype(f32))
    g = -jnp.exp(a_log.astype(f32)) * jax.nn.softplus(a_raw.astype(f32) + dt_bias.astype(f32))
    o = gated_delta_chunked(q, k, v, beta, g, CHUNK)
    o = head_rmsnorm(o, norm_g) * jax.nn.silu(g_raw.astype(f32))
    return o.astype(qkv_raw.dtype)


def hybrid_layer(x, mask, pos, lb, norm_mix, w_in, hg_norm, ret_norm, a_re, a_im, log_dt, b_re, b_im,
                 c_re, c_im, d_skip, w_glu, conv_w, a_log, dt_bias, gdn_norm, w_branch, w_out,
                 norm_ffn, w_gu, w_down):
    h = rmsnorm(x, norm_mix) * mask
    (hg_q, hg_f, hg_i, hg_g, rt_q, rt_k, rt_v, rt_g, ss_u,
     gd_qkv, gd_b, gd_a, gd_g, gate) = _split(h @ w_in, IN_SPLITS)
    branches = (
        hgrn2_branch(hg_q, hg_f, hg_i, hg_g, lb, hg_norm),
        retention_branch(rt_q, rt_k, rt_v, rt_g, pos, ret_norm),
        s5_branch(ss_u, a_re, a_im, log_dt, b_re, b_im, c_re, c_im, d_skip, w_glu),
        gdn_branch(gd_qkv, gd_b, gd_a, gd_g, conv_w, a_log, dt_bias, gdn_norm),
    )
    gates = jnp.split(gate, N_BRANCH, axis=-1)
    merged = jax.nn.sigmoid(gates[0]) * (branches[0] @ w_branch[0])
    for m in range(1, N_BRANCH):
        merged = merged + jax.nn.sigmoid(gates[m]) * (branches[m] @ w_branch[m])
    x = x + merged @ w_out
    h2 = rmsnorm(x, norm_ffn)
    gt, up = jnp.split(h2 @ w_gu, 2, axis=-1)
    return x + (jax.nn.silu(gt) * up) @ w_down


def setup_inputs(seed: int = 0) -> dict:
    key = jax.random.key(seed)
    ks = iter(jax.random.split(key, 40))
    f32 = jnp.float32

    def nrm(shape, scale):
        return jax.random.normal(next(ks), shape, f32) * scale

    def unif(shape, lo, hi):
        return jax.random.uniform(next(ks), shape, f32, lo, hi)

    G, N, P, W = SSM_GROUPS, SSM_STATE, SSM_GROUP, SSM_WIDTH
    x = nrm((BATCH, SEQ, D_MODEL), 1.0)
    meta = nrm((N_META, D_MODEL), 1.0)
    norm_mix = 1.0 + nrm((DEPTH, D_MODEL), 0.02)
    w_in = nrm((DEPTH, D_MODEL, IN_COLS), D_MODEL ** -0.5)
    hg_lb_logits = nrm((DEPTH, HG_HEADS * HG_DK), 0.1)
    hg_norm = 1.0 + nrm((DEPTH, HG_HEADS * HG_DV), 0.02)
    ret_norm = 1.0 + nrm((DEPTH, RET_HEADS * RET_DV), 0.02)
    n_idx = jnp.arange(N, dtype=f32)
    ssm_a_re = -0.5 + nrm((DEPTH, G, N), 0.01)
    ssm_a_im = math.pi * n_idx + nrm((DEPTH, G, N), 0.01)
    ssm_log_dt = unif((DEPTH, G), math.log(0.001), math.log(0.1))
    ssm_b_re = nrm((DEPTH, G, N, P), (2 * P) ** -0.5)
    ssm_b_im = nrm((DEPTH, G, N, P), (2 * P) ** -0.5)
    ssm_c_re = nrm((DEPTH, G, P, N), N ** -0.5)
    ssm_c_im = nrm((DEPTH, G, P, N), N ** -0.5)
    ssm_d = nrm((DEPTH, W), 1.0)
    ssm_w_glu = nrm((DEPTH, W, W), W ** -0.5)
    gdn_conv = nrm((DEPTH, CONV_K, GDN_QKV), CONV_K ** -0.5)
    gdn_a_log = jnp.log(unif((DEPTH, GDN_HEADS), 1.0, 16.0))
    dt0 = jnp.exp(unif((DEPTH, GDN_HEADS), math.log(0.001), math.log(0.1)))
    gdn_dt_bias = dt0 + jnp.log(-jnp.expm1(-dt0))
    gdn_norm = 1.0 + nrm((DEPTH, GDN_HEADS * GDN_DV), 0.02)
    w_branch = nrm((DEPTH, N_BRANCH, BRANCH_WIDTH, D_MODEL), BRANCH_WIDTH ** -0.5)
    w_out = nrm((DEPTH, D_MODEL, D_MODEL), D_MODEL ** -0.5)
    norm_ffn = 1.0 + nrm((DEPTH, D_MODEL), 0.02)
    w_gu = nrm((DEPTH, D_MODEL, 2 * D_FF), D_MODEL ** -0.5)
    w_down = nrm((DEPTH, D_FF, D_MODEL), D_FF ** -0.5)
    norm_final = 1.0 + nrm((D_MODEL,), 0.02)
    return {"x": x, "meta": meta, "norm_mix": norm_mix, "w_in": w_in, "hg_lb_logits": hg_lb_logits,
            "hg_norm": hg_norm, "ret_norm": ret_norm, "ssm_a_re": ssm_a_re, "ssm_a_im": ssm_a_im,
            "ssm_log_dt": ssm_log_dt, "ssm_b_re": ssm_b_re, "ssm_b_im": ssm_b_im, "ssm_c_re": ssm_c_re,
            "ssm_c_im": ssm_c_im, "ssm_d": ssm_d, "ssm_w_glu": ssm_w_glu, "gdn_conv": gdn_conv,
            "gdn_a_log": gdn_a_log, "gdn_dt_bias": gdn_dt_bias, "gdn_norm": gdn_norm, "w_branch": w_branch,
            "w_out": w_out, "norm_ffn": norm_ffn, "w_gu": w_gu, "w_down": w_down, "norm_final": norm_final}


def reference(x, meta, norm_mix, w_in, hg_lb_logits, hg_norm, ret_norm, ssm_a_re, ssm_a_im, ssm_log_dt,
              ssm_b_re, ssm_b_im, ssm_c_re, ssm_c_im, ssm_d, ssm_w_glu, gdn_conv, gdn_a_log, gdn_dt_bias,
              gdn_norm, w_branch, w_out, norm_ffn, w_gu, w_down, norm_final):
    bsz = x.shape[0]
    dt = x.dtype
    h = jnp.concatenate([jnp.zeros((bsz, PAD, D_MODEL), dt),
                         jnp.broadcast_to(meta.astype(dt)[None], (bsz, N_META, D_MODEL)), x], axis=1)
    L = h.shape[1]
    idx = jnp.arange(L)
    mask = (idx >= PAD).astype(dt)[:, None]
    pos = (idx - PAD).astype(jnp.float32)
    lb_all = jnp.cumsum(jax.nn.softmax(hg_lb_logits.astype(jnp.float32), axis=0), axis=0)
    lb_all = lb_all - lb_all[:1]
    for l in range(DEPTH):
        h = hybrid_layer(h, mask, pos, lb_all[l], norm_mix[l], w_in[l], hg_norm[l], ret_norm[l],
                         ssm_a_re[l], ssm_a_im[l], ssm_log_dt[l], ssm_b_re[l], ssm_b_im[l], ssm_c_re[l],
                         ssm_c_im[l], ssm_d[l], ssm_w_glu[l], gdn_conv[l], gdn_a_log[l], gdn_dt_bias[l],
                         gdn_norm[l], w_branch[l], w_out[l], norm_ffn[l], w_gu[l], w_down[l])
    out = rmsnorm(h, norm_final)
    return out[:, PAD + N_META:, :]
```

```python
import functools
import math

import jax
import jax.numpy as jnp
import numpy as np
from jax import lax
from jax.experimental import pallas as pl
from jax.experimental.pallas import tpu as pltpu

F32 = jnp.float32
BF16 = jnp.bfloat16

N_META = 16
EPS = 1e-6
HG_HEADS, HG_DK, HG_DV, HG_CHUNK = 4, 64, 128, 16
RET_HEADS, RET_DK, RET_DV = 4, 64, 128
ROPE_BASE = 10000.0
SSM_WIDTH, SSM_GROUP, SSM_STATE = 512, 16, 64
SSM_GROUPS = SSM_WIDTH // SSM_GROUP
SSM_CHUNK = 16
GDN_HEADS, GDN_DK, GDN_DV = 4, 128, 128
CONV_K = 4
CHUNK = 64
N_BRANCH = 4
BRANCH_WIDTH = 512

ROW_TILE = 128
DENSE_TILES = (640, 320, 128)
VMEM_LIMIT = 56 * 1024 * 1024

NT_DIMS = (((1,), (1,)), ((), ()))
TN_DIMS = (((0,), (0,)), ((), ()))


def _mm(a, b):
    return jnp.dot(a, b, preferred_element_type=F32)


def _mm_nt(a, b):
    return lax.dot_general(a, b, NT_DIMS, preferred_element_type=F32)


def _mm_tn(a, b):
    return lax.dot_general(a, b, TN_DIMS, preferred_element_type=F32)


def _split3(x):
    hi = x.astype(BF16)
    r = x - hi.astype(F32)
    mid = r.astype(BF16)
    lo = (r - mid.astype(F32)).astype(BF16)
    return hi, mid, lo


def _mm01(m01, x):
    hi, mid, lo = _split3(x)
    return _mm(m01, hi) + _mm(m01, mid) + _mm(m01, lo)


def _mm01_right(x, m01):
    hi, mid, lo = _split3(x)
    return _mm(hi, m01) + _mm(mid, m01) + _mm(lo, m01)


def _seg_masks(n, chunk):
    sh = int(math.log2(chunk))
    r = lax.broadcasted_iota(jnp.int32, (n, n), 0)
    c = lax.broadcasted_iota(jnp.int32, (n, n), 1)
    same = lax.shift_right_logical(r, sh) == lax.shift_right_logical(c, sh)
    return same, same & (c <= r), same & (c < r), same & (c >= r)


def _sigmoid(x):
    return jax.nn.sigmoid(x)


def _silu(x):
    return x * jax.nn.sigmoid(x)


def _softplus(x):
    return jnp.maximum(x, 0.0) + jnp.log1p(jnp.exp(-jnp.abs(x)))


def _norm_in(x, g, row0, npad):
    y = x * lax.rsqrt(jnp.mean(x * x, axis=-1, keepdims=True) + EPS) * g
    rows = row0 + lax.broadcasted_iota(jnp.int32, y.shape, 0)
    return jnp.where(rows >= npad, y, 0.0).astype(BF16)


def _head_rms(o, width):
    parts = []
    for h in range(o.shape[1] // width):
        oh = o[:, h * width:(h + 1) * width]
        parts.append(oh * lax.rsqrt(jnp.mean(oh * oh, axis=-1, keepdims=True) + EPS))
    return jnp.concatenate(parts, axis=1)


def _pair_block_mask():
    r = lax.broadcasted_iota(jnp.int32, (256, 128), 0)
    c = lax.broadcasted_iota(jnp.int32, (256, 128), 1)
    return lax.shift_right_logical(r, 7) == lax.shift_right_logical(c, 6)


def _head_lane_mask(a):
    c = lax.broadcasted_iota(jnp.int32, (1, 128), 1)
    return lax.shift_right_logical(c, 6) == a


def _hg_kernel(x_ref, g_ref, w_ref, lb_ref, ng_ref, o_ref, u_ref, s_ref, *, npad):
    T = x_ref.shape[0]
    i = pl.program_id(1)

    @pl.when(i == 0)
    def _():
        s_ref[...] = jnp.zeros_like(s_ref)

    hn = _norm_in(x_ref[...], g_ref[...], i * T, npad)
    z = _mm(hn, w_ref[...])
    zq, zf, v, zg = z[:, 0:256], z[:, 256:512], z[:, 512:1024], z[:, 1024:1536]
    u_ref[...] = z[:, 1536:2048]

    log_lb, log1m_lb, one_m_lb = lb_ref[0:1, :], lb_ref[1:2, :], lb_ref[2:3, :]
    e = jnp.exp(-jnp.abs(zf))
    inv1pe = 1.0 / (1.0 + e)
    log_sig = jnp.minimum(zf, 0.0) - jnp.log1p(e)
    y2 = log1m_lb + log_sig
    log_f = jnp.maximum(log_lb, y2) + jnp.log1p(jnp.exp(-jnp.abs(log_lb - y2)))
    k = one_m_lb * jnp.where(zf >= 0.0, e * inv1pe, inv1pe)
    q = _silu(zq) * (HG_DK ** -0.5)

    same, low, _, _ = _seg_masks(T, HG_CHUNK)
    b = _mm01(low.astype(BF16), log_f)
    b_last = _mm01(same.astype(BF16), log_f)
    q_t = q * jnp.exp(b)
    k_t = (k * jnp.exp(-b)).astype(BF16)
    k_end = (k * jnp.exp(b_last - b)).astype(BF16)
    d_end = jnp.exp(b_last)
    q_tb = q_t.astype(BF16)
    vb = v.astype(BF16)

    o_intra = []
    for h in range(HG_HEADS):
        p, a = h // 2, h % 2
        qh = jnp.where(_head_lane_mask(a), q_t[:, 128 * p:128 * p + 128], 0.0).astype(BF16)
        att = jnp.where(low, _mm_nt(qh, k_t[:, 128 * p:128 * p + 128]), 0.0)
        o_intra.append(_mm(att.astype(BF16), vb[:, 128 * h:128 * h + 128]))
    o_intra = jnp.concatenate(o_intra, axis=1)

    blk = _pair_block_mask()
    states = [s_ref[0], s_ref[1]]
    o_rows = []
    for n in range(T // HG_CHUNK):
        r = slice(n * HG_CHUNK, (n + 1) * HG_CHUNK)
        o_pairs = []
        for p in range(2):
            lanes = slice(128 * p, 128 * p + 128)
            st = states[p]
            o_pairs.append(_mm_nt(q_tb[r, lanes], st.astype(BF16)))
            kv = _mm_tn(vb[r, 256 * p:256 * p + 256], k_end[r, lanes])
            states[p] = d_end[n * HG_CHUNK:n * HG_CHUNK + 1, lanes] * st + jnp.where(blk, kv, 0.0)
        o_rows.append(jnp.concatenate(o_pairs, axis=1))
    s_ref[0] = states[0]
    s_ref[1] = states[1]
    o = o_intra + jnp.concatenate(o_rows, axis=0)

    o = _head_rms(o, HG_DV) * ng_ref[...] * _silu(zg)
    o_ref[...] = o.astype(o_ref.dtype)


def _rot_half_swap(x):
    lane = lax.broadcasted_iota(jnp.int32, x.shape, 1)
    first_half = (lane & 63) < 32
    return jnp.where(first_half, pltpu.roll(x, x.shape[1] - 32, 1), pltpu.roll(x, 32, 1))


def _ret_kernel(x_ref, g_ref, w_ref, cos_ref, sin_ref, dec_ref, xi_ref, zeta_ref, gc_ref, ng_ref,
                o_ref, s_ref, *, npad):
    T = x_ref.shape[0]
    i = pl.program_id(1)

    @pl.when(i == 0)
    def _():
        s_ref[...] = jnp.zeros_like(s_ref)

    hn = _norm_in(x_ref[...], g_ref[...], i * T, npad)
    z = _mm(hn, w_ref[...])
    zq, zk, v, zg = z[:, 0:256], z[:, 256:512], z[:, 512:1024], z[:, 1024:1536]
    cos, sin = cos_ref[...], sin_ref[...]
    q = zq * cos + _rot_half_swap(zq) * sin
    k = (zk * cos + _rot_half_swap(zk) * sin) * (RET_DK ** -0.5)
    qb, kb, vb = q.astype(BF16), k.astype(BF16), v.astype(BF16)
    q_xi = (q * xi_ref[...]).astype(BF16)
    k_zeta = (k * zeta_ref[...]).astype(BF16)

    o_intra = []
    for h in range(RET_HEADS):
        p, a = h // 2, h % 2
        qh = jnp.where(_head_lane_mask(a), q[:, 128 * p:128 * p + 128], 0.0).astype(BF16)
        att = _mm_nt(qh, kb[:, 128 * p:128 * p + 128]) * dec_ref[h]
        o_intra.append(_mm(att.astype(BF16), vb[:, 128 * h:128 * h + 128]))
    o_intra = jnp.concatenate(o_intra, axis=1)

    blk = _pair_block_mask()
    states = [s_ref[0], s_ref[1]]
    o_rows = []
    for n in range(T // CHUNK):
        r = slice(n * CHUNK, (n + 1) * CHUNK)
        o_pairs = []
        for p in range(2):
            lanes = slice(128 * p, 128 * p + 128)
            st = states[p]
            o_pairs.append(_mm_nt(q_xi[r, lanes], st.astype(BF16)))
            kv = _mm_tn(vb[r, 256 * p:256 * p + 256], k_zeta[r, lanes])
            states[p] = gc_ref[:, lanes] * st + jnp.where(blk, kv, 0.0)
        o_rows.append(jnp.concatenate(o_pairs, axis=1))
    s_ref[0] = states[0]
    s_ref[1] = states[1]
    o = o_intra + jnp.concatenate(o_rows, axis=0)

    parts = []
    for h in range(RET_HEADS):
        oh = o[:, 128 * h:128 * h + 128]
        oc = oh - jnp.mean(oh, axis=-1, keepdims=True)
        parts.append(oc * lax.rsqrt(jnp.mean(oc * oc, axis=-1, keepdims=True) + EPS))
    o = jnp.concatenate(parts, axis=1) * ng_ref[...] * _silu(zg)
    o_ref[...] = o.astype(o_ref.dtype)


def _gdn_kernel(x_ref, g_ref, w_ref, wba_ref, conv_ref, hp_col_ref, hp_row_ref, ng_ref,
                o_ref, s_ref, carry_ref, *, npad):
    T = x_ref.shape[0]
    H = GDN_HEADS
    assert T == 128
    i = pl.program_id(1)

    @pl.when(i == 0)
    def _():
        s_ref[...] = jnp.zeros_like(s_ref)
        carry_ref[...] = jnp.zeros_like(carry_ref)

    hn = _norm_in(x_ref[...], g_ref[...], i * T, npad)
    z = _mm(hn, w_ref[...])
    zqkv, zg, zba = z[:, 0:1536], z[:, 1536:2048], z[:, 2048:2176]
    ba_row = _mm_nt(wba_ref[...], hn)

    ext = jnp.concatenate([carry_ref[...], zqkv], axis=0)
    carry_ref[...] = zqkv[T - 8:T, :]
    xc = conv_ref[3:4, :] * zqkv
    for j in range(CONV_K - 1):
        s = CONV_K - 1 - j
        xc = xc + conv_ref[j:j + 1, :] * ext[8 - s:8 - s + T, :]
    qkv = _silu(xc)

    beta_col = _sigmoid(zba)
    g_col = hp_col_ref[0:1, :] * _softplus(zba + hp_col_ref[1:2, :])
    g_row = hp_row_ref[0] * _softplus(ba_row + hp_row_ref[1])
    same, low, strict, up = _seg_masks(T, CHUNK)
    low_b, same_b = low.astype(BF16), same.astype(BF16)
    cum_col = _mm01(low_b, g_col)
    last_col = _mm01(same_b, g_col)
    cum_row = _mm01_right(g_row, up.astype(BF16))

    eye = (lax.broadcasted_iota(jnp.int32, (T, T), 0) == lax.broadcasted_iota(jnp.int32, (T, T), 1)).astype(F32)
    o_heads = []
    for h in range(H):
        qh = qkv[:, 128 * h:128 * h + 128]
        kh = qkv[:, 512 + 128 * h:512 + 128 * h + 128]
        vh = qkv[:, 1024 + 128 * h:1024 + 128 * h + 128]
        qn = qh * lax.rsqrt(jnp.sum(qh * qh, axis=-1, keepdims=True) + EPS) * (GDN_DK ** -0.5)
        kn = kh * lax.rsqrt(jnp.sum(kh * kh, axis=-1, keepdims=True) + EPS)
        knb = kn.astype(BF16)

        cum_c = jnp.broadcast_to(cum_col[:, H + h:H + h + 1], (T, 128))
        last_c = jnp.broadcast_to(last_col[:, H + h:H + h + 1], (T, 128))
        beta_c = jnp.broadcast_to(beta_col[:, h:h + 1], (T, 128))
        cum_r = cum_row[H + h:H + h + 1, :]
        gamma = jnp.exp(jnp.where(low, cum_c - cum_r, -jnp.inf))

        a0 = jnp.where(strict, _mm_nt(knb, knb) * beta_c * gamma, 0.0)
        inv = eye - a0
        pw = a0
        for _ in range(int(math.log2(CHUNK)) - 1):
            pwb = pw.astype(BF16)
            pw = _mm(pwb, pwb)
            inv = inv + _mm(inv.astype(BF16), pw.astype(BF16))
        e_cum = jnp.exp(cum_c)
        rhs = jnp.concatenate([vh * beta_c, kn * beta_c * e_cum], axis=1)
        uw = _mm(inv.astype(BF16), rhs.astype(BF16))
        u, w = uw[:, 0:128], uw[:, 128:256]
        att = (_mm_nt(qn.astype(BF16), knb) * gamma).astype(BF16)
        qd = (qn * e_cum).astype(BF16)
        kd = (kn * jnp.exp(last_c - cum_c)).astype(BF16)
        dl = jnp.exp(last_c)
        wb = w.astype(BF16)

        st = s_ref[h]
        v_new_rows = []
        o_rows = []
        for n in range(T // CHUNK):
            r = slice(n * CHUNK, (n + 1) * CHUNK)
            stb = st.astype(BF16)
            v_new = u[r] - _mm(wb[r], stb)
            v_new_rows.append(v_new)
            v_all = jnp.concatenate(v_new_rows + [u[(n + 1) * CHUNK:]], axis=0) if (n + 1) * CHUNK < T \
                else jnp.concatenate(v_new_rows, axis=0)
            o_rows.append(_mm(qd[r], stb) + _mm(att[r], v_all.astype(BF16)))
            st = dl[n * CHUNK:n * CHUNK + 1, :] * st + _mm_tn(kd[r], v_new.astype(BF16))
        s_ref[h] = st
        o_heads.append(jnp.concatenate(o_rows, axis=0))
    o = jnp.concatenate(o_heads, axis=1)
    o = _head_rms(o, GDN_DV) * ng_ref[...] * _silu(zg)
    o_ref[...] = o.astype(o_ref.dtype)


def _shift_rows(x, s):
    n = x.shape[0]
    if s % 8 == 0:
        return jnp.concatenate([jnp.zeros((s, x.shape[1]), x.dtype), x[:n - s]], axis=0)
    rows = lax.broadcasted_iota(jnp.int32, x.shape, 0)
    return jnp.where(rows >= s, pltpu.roll(x, s, 0), 0.0)


def _s5_kernel(u_ref, m_ref, wst_ref, wout_ref, mul_ref, d_ref, y_ref, *, nbatch):
    u = u_ref[...]
    ub = u.astype(BF16)
    y = _mm(ub, m_ref[...])
    zst = _mm(ub, wst_ref[...])
    rb = u.shape[0] // nbatch
    nsteps = mul_ref.shape[0] // 2
    xin = []
    for bi in range(nbatch):
        x = zst[bi * rb:(bi + 1) * rb]
        for k in range(nsteps):
            s = 1 << k
            if s >= rb:
                break
            xs = _shift_rows(x, s)
            x = x + xs * mul_ref[2 * k:2 * k + 1, :] + pltpu.roll(xs, 64, 1) * mul_ref[2 * k + 1:2 * k + 2, :]
        xin.append(_shift_rows(x, 1))
    xin = jnp.concatenate(xin, axis=0)
    y = y + _mm(xin.astype(BF16), wout_ref[...]) + d_ref[...] * u
    y_ref[...] = y


def _gelu_tanh(x):
    return 0.5 * x * (1.0 + jnp.tanh(math.sqrt(2.0 / math.pi) * (x + 0.044715 * (x * x * x))))


def _merge_kernel(x_ref, g_ref, b0_ref, b1_ref, y5_ref, b3_ref, wg_ref, wglu_ref, wb_ref, wo_ref,
                  o_ref, *, npad, tiles_per_batch):
    T = x_ref.shape[0]
    x = x_ref[...]
    row0 = lax.rem(pl.program_id(0), tiles_per_batch) * T
    hn = _norm_in(x, g_ref[...], row0, npad)
    ya = _gelu_tanh(y5_ref[...])
    b2 = ya * _sigmoid(_mm(ya.astype(BF16), wglu_ref[...]))
    branches = (b0_ref[...], b1_ref[...], b2.astype(BF16), b3_ref[...])
    dm = x.shape[1]
    merged = None
    for m in range(N_BRANCH):
        gate = _sigmoid(_mm(hn, wg_ref[:, m * dm:(m + 1) * dm]))
        term = gate * _mm(branches[m], wb_ref[m])
        merged = term if merged is None else merged + term
    o_ref[...] = x + _mm(merged.astype(BF16), wo_ref[...])


def _ffn_kernel(x_ref, g_ref, wgu_ref, wd_ref, o_ref, *, n_split):
    x = x_ref[...]
    h = (x * lax.rsqrt(jnp.mean(x * x, axis=-1, keepdims=True) + EPS) * g_ref[...]).astype(BF16)
    dff = wd_ref.shape[0]
    cw = dff // n_split
    acc = x
    for c in range(n_split):
        gt = _mm(h, wgu_ref[:, c * cw:(c + 1) * cw])
        up = _mm(h, wgu_ref[:, dff + c * cw:dff + (c + 1) * cw])
        acc = acc + _mm((_silu(gt) * up).astype(BF16), wd_ref[c * cw:(c + 1) * cw, :])
    o_ref[...] = acc


def _final_kernel(x_ref, g_ref, o_ref):
    x = x_ref[...]
    o_ref[...] = x * lax.rsqrt(jnp.mean(x * x, axis=-1, keepdims=True) + EPS) * g_ref[...]


def _const_spec(block, index):
    return pl.BlockSpec(block, index, pipeline_mode=pl.Buffered(1))


def _layer_spec(shape, layer, ngrid):
    zeros = (0,) * len(shape)
    if ngrid == 1:
        return _const_spec((None,) + tuple(shape), lambda i: (layer,) + zeros)
    return _const_spec((None,) + tuple(shape), lambda b, i: (layer,) + zeros)


def _params(sem):
    return pltpu.CompilerParams(dimension_semantics=sem, vmem_limit_bytes=VMEM_LIMIT)


def _s5_constants(a_re, a_im, log_dt, b_re, b_im, c_re, c_im, d_skip, nsteps):
    hp = lax.Precision.HIGHEST
    C, G, N, P = SSM_CHUNK, SSM_GROUPS, SSM_STATE, SSM_GROUP
    dt = jnp.exp(log_dt)[:, None]
    mag = jnp.exp(dt * a_re)
    ar, ai = mag * jnp.cos(dt * a_im), mag * jnp.sin(dt * a_im)
    den = a_re * a_re + a_im * a_im
    nr, ni = ar - 1.0, ai
    cr, ci = (nr * a_re + ni * a_im) / den, (ni * a_re - nr * a_im) / den
    bbr = cr[..., None] * b_re - ci[..., None] * b_im
    bbi = cr[..., None] * b_im + ci[..., None] * b_re
    pr, pi = [jnp.ones_like(ar)], [jnp.zeros_like(ar)]
    for _ in range(C):
        pr, pi = pr + [pr[-1] * ar - pi[-1] * ai], pi + [pr[-1] * ai + pi[-1] * ar]
    pr, pi = jnp.stack(pr), jnp.stack(pi)
    car = c_re[None] * pr[:C, :, None, :] - c_im[None] * pi[:C, :, None, :]
    cai = c_re[None] * pi[:C, :, None, :] + c_im[None] * pr[:C, :, None, :]
    kern = (jnp.einsum('tgpn,gnq->gtpq', car, bbr, precision=hp)
            - jnp.einsum('tgpn,gnq->gtpq', cai, bbi, precision=hp))
    lag = np.arange(C)[None, :] - np.arange(C)[:, None]
    m = jnp.where((lag >= 0)[None, :, :, None, None], kern[:, np.clip(lag, 0, C - 1)], 0.0)
    m = m.transpose(0, 1, 4, 2, 3).reshape(G, C * P, C * P)
    rev_r, rev_i = pr[C - 1::-1][:C], pi[C - 1::-1][:C]
    wst_r = rev_r[..., None] * bbr[None] - rev_i[..., None] * bbi[None]
    wst_i = rev_r[..., None] * bbi[None] + rev_i[..., None] * bbr[None]
    wst = jnp.concatenate([wst_r, wst_i], axis=2).transpose(1, 0, 3, 2).reshape(G, C * P, 2 * N)
    co_r = c_re[None] * pr[1:, :, None, :] - c_im[None] * pi[1:, :, None, :]
    co_i = c_re[None] * pi[1:, :, None, :] + c_im[None] * pr[1:, :, None, :]
    wout = jnp.concatenate([co_r, -co_i], axis=3).transpose(1, 3, 0, 2).reshape(G, 2 * N, C * P)
    sr, si = pr[C], pi[C]
    muls = []
    for _ in range(nsteps):
        muls += [jnp.concatenate([sr, sr], axis=1), jnp.concatenate([-si, si], axis=1)]
        sr, si = sr * sr - si * si, 2.0 * sr * si
    muls = jnp.stack(muls, axis=1)
    dvec = jnp.tile(d_skip.reshape(G, 1, P), (1, C, 1)).reshape(G, 1, C * P)
    return m.astype(BF16), wst.astype(BF16), wout.astype(BF16), muls.astype(F32), dvec.astype(F32)


def _retention_tables(lp, npad, tile):
    half = RET_DK // 2
    pos = (jnp.arange(lp) - npad).astype(F32)
    inv = ROPE_BASE ** (-jnp.arange(half, dtype=F32) / half)
    ang = pos[:, None] * inv[None, :]
    cos, sin = jnp.cos(ang), jnp.sin(ang)
    cos_t = jnp.tile(jnp.concatenate([cos, cos], axis=1), (1, RET_HEADS))
    sin_t = jnp.tile(jnp.concatenate([-sin, sin], axis=1), (1, RET_HEADS))
    log_gamma = jnp.log1p(-jnp.exp2(-5.0 - jnp.arange(RET_HEADS, dtype=F32)))
    idx = jnp.arange(tile)
    loc = (idx % CHUNK).astype(F32)
    diff = (idx[:, None] - idx[None, :]).astype(F32)
    ok = ((idx[:, None] // CHUNK) == (idx[None, :] // CHUNK)) & (diff >= 0)
    dec = jnp.where(ok[None], jnp.exp(log_gamma[:, None, None] * jnp.maximum(diff, 0.0)[None]), 0.0)
    xi = jnp.repeat(jnp.exp(log_gamma[None, :] * (loc[:, None] + 1.0)), RET_DK, axis=1)
    zeta = jnp.repeat(jnp.exp(log_gamma[None, :] * (CHUNK - 1.0 - loc[:, None])), RET_DK, axis=1)
    gc = jnp.repeat(jnp.exp(log_gamma * CHUNK), RET_DK)[None, :]
    return cos_t, sin_t, dec, xi, zeta, gc


def kernel(x, meta, norm_mix, w_in, hg_lb_logits, hg_norm, ret_norm, ssm_a_re, ssm_a_im, ssm_log_dt,
           ssm_b_re, ssm_b_im, ssm_c_re, ssm_c_im, ssm_d, ssm_w_glu, gdn_conv, gdn_a_log, gdn_dt_bias,
           gdn_norm, w_branch, w_out, norm_ffn, w_gu, w_down, norm_final):
    bsz, seq, dm = x.shape
    depth = w_in.shape[0]
    T = ROW_TILE
    assert seq % T == 0 and dm % 128 == 0
    npad = (-(N_META + seq)) % T
    lp = npad + N_META + seq
    nt = lp // T
    rows = bsz * lp
    tm = next(t for t in DENSE_TILES if lp % t == 0)
    dff = w_down.shape[1]

    h = jnp.concatenate([jnp.zeros((bsz, npad, dm), F32),
                         jnp.broadcast_to(meta.astype(F32)[None], (bsz, N_META, dm)), x.astype(F32)], axis=1)
    h = h.reshape(rows, dm)

    wb16 = w_in.astype(BF16)
    w_hg = jnp.concatenate([wb16[:, :, 0:1536], wb16[:, :, 3072:3584]], axis=2)
    w_rt = wb16[:, :, 1536:3072]
    w_gd = jnp.concatenate([wb16[:, :, 3584:5120], wb16[:, :, 5128:5640],
                            jnp.pad(wb16[:, :, 5120:5128], ((0, 0), (0, 0), (0, 120)))], axis=2)
    w_ba = jnp.swapaxes(wb16[:, :, 5120:5128], 1, 2)
    w_gate = wb16[:, :, 5640:5640 + N_BRANCH * dm]
    w_glu16, w_br16, w_out16 = ssm_w_glu.astype(BF16), w_branch.astype(BF16), w_out.astype(BF16)
    w_gu16, w_down16 = w_gu.astype(BF16), w_down.astype(BF16)

    lb = jnp.cumsum(jax.nn.softmax(hg_lb_logits.astype(F32), axis=0), axis=0)
    lb = lb - lb[:1]
    lb_rows = jnp.stack([jnp.log(lb), jnp.log1p(-lb), 1.0 - lb] + [jnp.zeros_like(lb)] * 5, axis=1)

    neg_decay = -jnp.exp(gdn_a_log.astype(F32))
    zeros4 = jnp.zeros_like(neg_decay)
    hp_col = jnp.stack([jnp.pad(jnp.concatenate([zeros4, neg_decay], axis=1), ((0, 0), (0, 120))),
                        jnp.pad(jnp.concatenate([zeros4, gdn_dt_bias.astype(F32)], axis=1), ((0, 0), (0, 120)))]
                       + [jnp.zeros((depth, 128), F32)] * 6, axis=1)
    hp_row = jnp.stack([jnp.concatenate([zeros4, neg_decay], axis=1),
                        jnp.concatenate([zeros4, gdn_dt_bias.astype(F32)], axis=1)], axis=1)
    hp_row = jnp.broadcast_to(hp_row[..., None], (depth, 2, 8, T))

    cos_t, sin_t, dec, xi, zeta, gc = _retention_tables(lp, npad, T)

    rchunks = rows // SSM_CHUNK
    nsteps = max(1, int(math.ceil(math.log2(lp // SSM_CHUNK))))

    row_spec = lambda w: pl.BlockSpec((T, w), lambda b, i: (b * nt + i, 0))
    vec2 = lambda w, l: _const_spec((None, 1, w), lambda b, i: (l, 0, 0))
    mixer_params = _params(("arbitrary", "arbitrary"))

    for l in range(depth):
        gmix = norm_mix.astype(F32).reshape(depth, 1, dm)
        br0, u = pl.pallas_call(
            functools.partial(_hg_kernel, npad=npad),
            grid=(bsz, nt),
            in_specs=[row_spec(dm), vec2(dm, l), _layer_spec((dm, 2048), l, 2),
                      _layer_spec((8, 256), l, 2), vec2(512, l)],
            out_specs=[row_spec(512), row_spec(512)],
            out_shape=[jax.ShapeDtypeStruct((rows, 512), BF16), jax.ShapeDtypeStruct((rows, 512), F32)],
            scratch_shapes=[pltpu.VMEM((2, 256, 128), F32)],
            compiler_params=mixer_params, name="hgrn2_mixer",
        )(h, gmix, w_hg, lb_rows, hg_norm.astype(F32).reshape(depth, 1, 512))

        br1 = pl.pallas_call(
            functools.partial(_ret_kernel, npad=npad),
            grid=(bsz, nt),
            in_specs=[row_spec(dm), vec2(dm, l), _layer_spec((dm, 1536), l, 2),
                      pl.BlockSpec((T, 256), lambda b, i: (i, 0)), pl.BlockSpec((T, 256), lambda b, i: (i, 0)),
                      _const_spec((RET_HEADS, T, T), lambda b, i: (0, 0, 0)),
                      _const_spec((T, 256), lambda b, i: (0, 0)), _const_spec((T, 256), lambda b, i: (0, 0)),
                      _const_spec((1, 256), lambda b, i: (0, 0)), vec2(512, l)],
            out_specs=row_spec(512),
            out_shape=jax.ShapeDtypeStruct((rows, 512), BF16),
            scratch_shapes=[pltpu.VMEM((2, 256, 128), F32)],
            compiler_params=mixer_params, name="retention_mixer",
        )(h, gmix, w_rt, cos_t, sin_t, dec, xi, zeta, gc, ret_norm.astype(F32).reshape(depth, 1, 512))

        br3 = pl.pallas_call(
            functools.partial(_gdn_kernel, npad=npad),
            grid=(bsz, nt),
            in_specs=[row_spec(dm), vec2(dm, l), _layer_spec((dm, 2176), l, 2), _layer_spec((8, dm), l, 2),
                      _layer_spec((CONV_K, 1536), l, 2), _layer_spec((8, 128), l, 2),
                      _layer_spec((2, 8, T), l, 2), vec2(512, l)],
            out_specs=row_spec(512),
            out_shape=jax.ShapeDtypeStruct((rows, 512), BF16),
            scratch_shapes=[pltpu.VMEM((GDN_HEADS, GDN_DK, GDN_DV), F32), pltpu.VMEM((8, 1536), F32)],
            compiler_params=mixer_params, name="gdn_mixer",
        )(h, gmix, w_gd, w_ba, gdn_conv.astype(F32), hp_col, hp_row, gdn_norm.astype(F32).reshape(depth, 1, 512))

        m_k, wst, wout, muls, dvec = _s5_constants(
            ssm_a_re[l].astype(F32), ssm_a_im[l].astype(F32), ssm_log_dt[l].astype(F32),
            ssm_b_re[l].astype(F32), ssm_b_im[l].astype(F32), ssm_c_re[l].astype(F32), ssm_c_im[l].astype(F32),
            ssm_d[l].astype(F32), nsteps)
        cp = SSM_CHUNK * SSM_GROUP
        u_g = u.reshape(rchunks, SSM_CHUNK, SSM_GROUPS, SSM_GROUP).transpose(2, 0, 1, 3).reshape(SSM_GROUPS, rchunks, cp)
        gspec = lambda a, b_: pl.BlockSpec((None, a, b_), lambda g: (g, 0, 0))
        y_g = pl.pallas_call(
            functools.partial(_s5_kernel, nbatch=bsz),
            grid=(SSM_GROUPS,),
            in_specs=[gspec(rchunks, cp), gspec(cp, cp), gspec(cp, 2 * SSM_STATE), gspec(2 * SSM_STATE, cp),
                      gspec(2 * nsteps, 2 * SSM_STATE), gspec(1, cp)],
            out_specs=gspec(rchunks, cp),
            out_shape=jax.ShapeDtypeStruct((SSM_GROUPS, rchunks, cp), F32),
            compiler_params=_params(("arbitrary",)), name="s5_mixer",
        )(u_g, m_k, wst, wout, muls, dvec)
        y5 = y_g.reshape(SSM_GROUPS, rchunks, SSM_CHUNK, SSM_GROUP).transpose(1, 2, 0, 3).reshape(rows, SSM_WIDTH)

        drow = lambda w: pl.BlockSpec((tm, w), lambda i: (i, 0))
        dvec1 = lambda w, a: _const_spec((None, 1, w), lambda i: (l, 0, 0))
        h = pl.pallas_call(
            functools.partial(_merge_kernel, npad=npad, tiles_per_batch=lp // tm),
            grid=(rows // tm,),
            in_specs=[drow(dm), dvec1(dm, None), drow(512), drow(512), drow(512), drow(512),
                      _layer_spec((dm, N_BRANCH * dm), l, 1), _layer_spec((512, 512), l, 1),
                      _layer_spec((N_BRANCH, BRANCH_WIDTH, dm), l, 1), _layer_spec((dm, dm), l, 1)],
            out_specs=drow(dm),
            out_shape=jax.ShapeDtypeStruct((rows, dm), F32),
            compiler_params=_params(("arbitrary",)), name="merge_out",
        )(h, gmix, br0, br1, y5, br3, w_gate, w_glu16, w_br16, w_out16)

        h = pl.pallas_call(
            functools.partial(_ffn_kernel, n_split=2),
            grid=(rows // tm,),
            in_specs=[drow(dm), dvec1(dm, None), _layer_spec((dm, 2 * dff), l, 1), _layer_spec((dff, dm), l, 1)],
            out_specs=drow(dm),
            out_shape=jax.ShapeDtypeStruct((rows, dm), F32),
            compiler_params=_params(("arbitrary",)), name="swiglu_ffn",
        )(h, norm_ffn.astype(F32).reshape(depth, 1, dm), w_gu16, w_down16)

    skip = (npad + N_META) // T
    assert skip * T == npad + N_META
    ns = seq // T
    out = pl.pallas_call(
        _final_kernel,
        grid=(bsz, ns),
        in_specs=[pl.BlockSpec((T, dm), lambda b, i: (b * nt + skip + i, 0)),
                  _const_spec((1, dm), lambda b, i: (0, 0))],
        out_specs=pl.BlockSpec((T, dm), lambda b, i: (b * ns + i, 0)),
        out_shape=jax.ShapeDtypeStruct((bsz * seq, dm), x.dtype),
        compiler_params=_params(("arbitrary", "arbitrary")), name="final_norm",
    )(h, norm_final.astype(F32).reshape(1, dm))
    return out.reshape(bsz, seq, dm)
```

```python
import functools
import math

import jax
import jax.numpy as jnp
import numpy as np
from jax import lax
from jax.experimental import pallas as pl
from jax.experimental.pallas import tpu as pltpu

F32 = jnp.float32
BF16 = jnp.bfloat16

N_META = 16
EPS = 1e-6
HG_HEADS, HG_DK, HG_DV, HG_CHUNK = 4, 64, 128, 16
RET_HEADS, RET_DK, RET_DV = 4, 64, 128
ROPE_BASE = 10000.0
SSM_WIDTH, SSM_GROUP, SSM_STATE = 512, 16, 64
SSM_GROUPS = SSM_WIDTH // SSM_GROUP
SSM_CHUNK = 16
S5_GROUP_BLOCK = 8
GDN_HEADS, GDN_DK, GDN_DV = 4, 128, 128
CONV_K = 4
CHUNK = 64
N_BRANCH = 4
BRANCH_WIDTH = 512

ROW_TILE = 128
DENSE_TILES = (640, 320, 128)
VMEM_LIMIT = 56 * 1024 * 1024

NT_DIMS = (((1,), (1,)), ((), ()))
TN_DIMS = (((0,), (0,)), ((), ()))


def _mm(a, b):
    return jnp.dot(a, b, preferred_element_type=F32)


def _mm_nt(a, b):
    return lax.dot_general(a, b, NT_DIMS, preferred_element_type=F32)


def _mm_tn(a, b):
    return lax.dot_general(a, b, TN_DIMS, preferred_element_type=F32)


def _split3(x):
    hi = x.astype(BF16)
    r = x - hi.astype(F32)
    mid = r.astype(BF16)
    lo = (r - mid.astype(F32)).astype(BF16)
    return hi, mid, lo


def _mm01(m01, x):
    hi, mid, lo = _split3(x)
    return _mm(m01, hi) + _mm(m01, mid) + _mm(m01, lo)


def _mm01_right(x, m01):
    hi, mid, lo = _split3(x)
    return _mm(hi, m01) + _mm(mid, m01) + _mm(lo, m01)


def _seg_masks(n, chunk):
    sh = int(math.log2(chunk))
    r = lax.broadcasted_iota(jnp.int32, (n, n), 0)
    c = lax.broadcasted_iota(jnp.int32, (n, n), 1)
    same = lax.shift_right_logical(r, sh) == lax.shift_right_logical(c, sh)
    return same, same & (c <= r), same & (c < r), same & (c >= r)


def _sigmoid(x):
    return jax.nn.sigmoid(x)


def _silu(x):
    return x * jax.nn.sigmoid(x)


def _softplus(x):
    return jnp.maximum(x, 0.0) + jnp.log1p(jnp.exp(-jnp.abs(x)))


def _norm_in(x, g, row0, npad):
    y = x * lax.rsqrt(jnp.mean(x * x, axis=-1, keepdims=True) + EPS) * g
    rows = row0 + lax.broadcasted_iota(jnp.int32, y.shape, 0)
    return jnp.where(rows >= npad, y, 0.0).astype(BF16)


def _head_rms(o, width):
    parts = []
    for h in range(o.shape[1] // width):
        oh = o[:, h * width:(h + 1) * width]
        parts.append(oh * lax.rsqrt(jnp.mean(oh * oh, axis=-1, keepdims=True) + EPS))
    return jnp.concatenate(parts, axis=1)


def _pair_block_mask():
    r = lax.broadcasted_iota(jnp.int32, (256, 128), 0)
    c = lax.broadcasted_iota(jnp.int32, (256, 128), 1)
    return lax.shift_right_logical(r, 7) == lax.shift_right_logical(c, 6)


def _head_lane_mask(a):
    c = lax.broadcasted_iota(jnp.int32, (1, 128), 1)
    return lax.shift_right_logical(c, 6) == a


def _hg_kernel(x_ref, g_ref, w_ref, lb_ref, ng_ref, o_ref, u_ref, s_ref, *, npad):
    T = x_ref.shape[0]
    i = pl.program_id(1)

    @pl.when(i == 0)
    def _():
        s_ref[...] = jnp.zeros_like(s_ref)

    hn = _norm_in(x_ref[...], g_ref[...], i * T, npad)
    z = _mm(hn, w_ref[...])
    zq, zf, v, zg = z[:, 0:256], z[:, 256:512], z[:, 512:1024], z[:, 1024:1536]
    u_ref[...] = z[:, 1536:2048]

    log_lb, log1m_lb, one_m_lb = lb_ref[0:1, :], lb_ref[1:2, :], lb_ref[2:3, :]
    e = jnp.exp(-jnp.abs(zf))
    inv1pe = 1.0 / (1.0 + e)
    log_sig = jnp.minimum(zf, 0.0) - jnp.log1p(e)
    y2 = log1m_lb + log_sig
    log_f = jnp.maximum(log_lb, y2) + jnp.log1p(jnp.exp(-jnp.abs(log_lb - y2)))
    k = one_m_lb * jnp.where(zf >= 0.0, e * inv1pe, inv1pe)
    q = _silu(zq) * (HG_DK ** -0.5)

    same, low, _, _ = _seg_masks(T, HG_CHUNK)
    b = _mm01(low.astype(BF16), log_f)
    b_last = _mm01(same.astype(BF16), log_f)
    q_t = q * jnp.exp(b)
    k_t = (k * jnp.exp(-b)).astype(BF16)
    k_end = (k * jnp.exp(b_last - b)).astype(BF16)
    d_end = jnp.exp(b_last)
    q_tb = q_t.astype(BF16)
    vb = v.astype(BF16)

    o_intra = []
    for h in range(HG_HEADS):
        p, a = h // 2, h % 2
        qh = jnp.where(_head_lane_mask(a), q_t[:, 128 * p:128 * p + 128], 0.0).astype(BF16)
        att = jnp.where(low, _mm_nt(qh, k_t[:, 128 * p:128 * p + 128]), 0.0)
        o_intra.append(_mm(att.astype(BF16), vb[:, 128 * h:128 * h + 128]))
    o_intra = jnp.concatenate(o_intra, axis=1)

    blk = _pair_block_mask()
    states = [s_ref[0], s_ref[1]]
    o_rows = []
    for n in range(T // HG_CHUNK):
        r = slice(n * HG_CHUNK, (n + 1) * HG_CHUNK)
        o_pairs = []
        for p in range(2):
            lanes = slice(128 * p, 128 * p + 128)
            st = states[p]
            o_pairs.append(_mm_nt(q_tb[r, lanes], st.astype(BF16)))
            kv = _mm_tn(vb[r, 256 * p:256 * p + 256], k_end[r, lanes])
            states[p] = d_end[n * HG_CHUNK:n * HG_CHUNK + 1, lanes] * st + jnp.where(blk, kv, 0.0)
        o_rows.append(jnp.concatenate(o_pairs, axis=1))
    s_ref[0] = states[0]
    s_ref[1] = states[1]
    o = o_intra + jnp.concatenate(o_rows, axis=0)

    o = _head_rms(o, HG_DV) * ng_ref[...] * _silu(zg)
    o_ref[...] = o.astype(o_ref.dtype)


def _rot_half_swap(x):
    lane = lax.broadcasted_iota(jnp.int32, x.shape, 1)
    first_half = (lane & 63) < 32
    return jnp.where(first_half, pltpu.roll(x, x.shape[1] - 32, 1), pltpu.roll(x, 32, 1))


def _ret_kernel(x_ref, g_ref, w_ref, cos_ref, sin_ref, dec_ref, xi_ref, zeta_ref, gc_ref, ng_ref,
                o_ref, s_ref, *, npad):
    T = x_ref.shape[0]
    i = pl.program_id(1)

    @pl.when(i == 0)
    def _():
        s_ref[...] = jnp.zeros_like(s_ref)

    hn = _norm_in(x_ref[...], g_ref[...], i * T, npad)
    z = _mm(hn, w_ref[...])
    zq, zk, v, zg = z[:, 0:256], z[:, 256:512], z[:, 512:1024], z[:, 1024:1536]
    cos, sin = cos_ref[...], sin_ref[...]
    q = zq * cos + _rot_half_swap(zq) * sin
    k = (zk * cos + _rot_half_swap(zk) * sin) * (RET_DK ** -0.5)
    qb, kb, vb = q.astype(BF16), k.astype(BF16), v.astype(BF16)
    q_xi = (q * xi_ref[...]).astype(BF16)
    k_zeta = (k * zeta_ref[...]).astype(BF16)

    o_intra = []
    for h in range(RET_HEADS):
        p, a = h // 2, h % 2
        qh = jnp.where(_head_lane_mask(a), q[:, 128 * p:128 * p + 128], 0.0).astype(BF16)
        att = _mm_nt(qh, kb[:, 128 * p:128 * p + 128]) * dec_ref[h]
        o_intra.append(_mm(att.astype(BF16), vb[:, 128 * h:128 * h + 128]))
    o_intra = jnp.concatenate(o_intra, axis=1)

    blk = _pair_block_mask()
    states = [s_ref[0], s_ref[1]]
    o_rows = []
    for n in range(T // CHUNK):
        r = slice(n * CHUNK, (n + 1) * CHUNK)
        o_pairs = []
        for p in range(2):
            lanes = slice(128 * p, 128 * p + 128)
            st = states[p]
            o_pairs.append(_mm_nt(q_xi[r, lanes], st.astype(BF16)))
            kv = _mm_tn(vb[r, 256 * p:256 * p + 256], k_zeta[r, lanes])
            states[p] = gc_ref[:, lanes] * st + jnp.where(blk, kv, 0.0)
        o_rows.append(jnp.concatenate(o_pairs, axis=1))
    s_ref[0] = states[0]
    s_ref[1] = states[1]
    o = o_intra + jnp.concatenate(o_rows, axis=0)

    parts = []
    for h in range(RET_HEADS):
        oh = o[:, 128 * h:128 * h + 128]
        oc = oh - jnp.mean(oh, axis=-1, keepdims=True)
        parts.append(oc * lax.rsqrt(jnp.mean(oc * oc, axis=-1, keepdims=True) + EPS))
    o = jnp.concatenate(parts, axis=1) * ng_ref[...] * _silu(zg)
    o_ref[...] = o.astype(o_ref.dtype)


def _gdn_kernel(x_ref, g_ref, w_ref, wba_ref, conv_ref, hp_col_ref, hp_row_ref, ng_ref,
                o_ref, s_ref, carry_ref, *, npad):
    T = x_ref.shape[0]
    H = GDN_HEADS
    assert T == 128
    i = pl.program_id(1)

    @pl.when(i == 0)
    def _():
        s_ref[...] = jnp.zeros_like(s_ref)
        carry_ref[...] = jnp.zeros_like(carry_ref)

    hn = _norm_in(x_ref[...], g_ref[...], i * T, npad)
    z = _mm(hn, w_ref[...])
    zqkv, zg, zba = z[:, 0:1536], z[:, 1536:2048], z[:, 2048:2176]
    ba_row = _mm_nt(wba_ref[...], hn)

    ext = jnp.concatenate([carry_ref[...], zqkv], axis=0)
    carry_ref[...] = zqkv[T - 8:T, :]
    xc = conv_ref[3:4, :] * zqkv
    for j in range(CONV_K - 1):
        s = CONV_K - 1 - j
        xc = xc + conv_ref[j:j + 1, :] * ext[8 - s:8 - s + T, :]
    qkv = _silu(xc)

    beta_col = _sigmoid(zba)
    g_col = hp_col_ref[0:1, :] * _softplus(zba + hp_col_ref[1:2, :])
    g_row = hp_row_ref[0] * _softplus(ba_row + hp_row_ref[1])
    same, low, strict, up = _seg_masks(T, CHUNK)
    low_b, same_b = low.astype(BF16), same.astype(BF16)
    cum_col = _mm01(low_b, g_col)
    last_col = _mm01(same_b, g_col)
    cum_row = _mm01_right(g_row, up.astype(BF16))

    eye = (lax.broadcasted_iota(jnp.int32, (T, T), 0) == lax.broadcasted_iota(jnp.int32, (T, T), 1)).astype(F32)
    o_heads = []
    for h in range(H):
        qh = qkv[:, 128 * h:128 * h + 128]
        kh = qkv[:, 512 + 128 * h:512 + 128 * h + 128]
        vh = qkv[:, 1024 + 128 * h:1024 + 128 * h + 128]
        qn = qh * lax.rsqrt(jnp.sum(qh * qh, axis=-1, keepdims=True) + EPS) * (GDN_DK ** -0.5)
        kn = kh * lax.rsqrt(jnp.sum(kh * kh, axis=-1, keepdims=True) + EPS)
        knb = kn.astype(BF16)

        cum_c = jnp.broadcast_to(cum_col[:, H + h:H + h + 1], (T, 128))
        last_c = jnp.broadcast_to(last_col[:, H + h:H + h + 1], (T, 128))
        beta_c = jnp.broadcast_to(beta_col[:, h:h + 1], (T, 128))
        cum_r = cum_row[H + h:H + h + 1, :]
        gamma = jnp.exp(jnp.where(low, cum_c - cum_r, -jnp.inf))

        a0 = jnp.where(strict, _mm_nt(knb, knb) * beta_c * gamma, 0.0)
        inv = eye - a0
        pw = a0
        for _ in range(int(math.log2(CHUNK)) - 1):
            pwb = pw.astype(BF16)
            pw = _mm(pwb, pwb)
            inv = inv + _mm(inv.astype(BF16), pw.astype(BF16))
        e_cum = jnp.exp(cum_c)
        rhs = jnp.concatenate([vh * beta_c, kn * beta_c * e_cum], axis=1)
        uw = _mm(inv.astype(BF16), rhs.astype(BF16))
        u, w = uw[:, 0:128], uw[:, 128:256]
        att = (_mm_nt(qn.astype(BF16), knb) * gamma).astype(BF16)
        qd = (qn * e_cum).astype(BF16)
        kd = (kn * jnp.exp(last_c - cum_c)).astype(BF16)
        dl = jnp.exp(last_c)
        wb = w.astype(BF16)

        st = s_ref[h]
        v_new_rows = []
        o_rows = []
        for n in range(T // CHUNK):
            r = slice(n * CHUNK, (n + 1) * CHUNK)
            stb = st.astype(BF16)
            v_new = u[r] - _mm(wb[r], stb)
            v_new_rows.append(v_new)
            v_all = jnp.concatenate(v_new_rows + [u[(n + 1) * CHUNK:]], axis=0) if (n + 1) * CHUNK < T \
                else jnp.concatenate(v_new_rows, axis=0)
            o_rows.append(_mm(qd[r], stb) + _mm(att[r], v_all.astype(BF16)))
            st = dl[n * CHUNK:n * CHUNK + 1, :] * st + _mm_tn(kd[r], v_new.astype(BF16))
        s_ref[h] = st
        o_heads.append(jnp.concatenate(o_rows, axis=0))
    o = jnp.concatenate(o_heads, axis=1)
    o = _head_rms(o, GDN_DV) * ng_ref[...] * _silu(zg)
    o_ref[...] = o.astype(o_ref.dtype)


def _shift_rows(x, s):
    n = x.shape[0]
    if s % 8 == 0:
        return jnp.concatenate([jnp.zeros((s, x.shape[1]), x.dtype), x[:n - s]], axis=0)
    rows = lax.broadcasted_iota(jnp.int32, x.shape, 0)
    return jnp.where(rows >= s, pltpu.roll(x, s, 0), 0.0)


def _s5_kernel(u_ref, m_ref, wst_ref, wout_ref, mul_ref, d_ref, y_ref):
    C, P = SSM_CHUNK, SSM_GROUP
    rb = u_ref.shape[0]
    nsteps = mul_ref.shape[1] // 2
    ut = [u_ref[:, t, :] for t in range(C)]
    yg = []
    for gl in range(u_ref.shape[2] // P):
        ug = jnp.concatenate([ut[t][:, gl * P:(gl + 1) * P] for t in range(C)], axis=1)
        ub = ug.astype(BF16)
        y = _mm(ub, m_ref[gl])
        x = _mm(ub, wst_ref[gl])
        for k in range(nsteps):
            s = 1 << k
            if s >= rb:
                break
            xs = _shift_rows(x, s)
            x = x + xs * mul_ref[gl, 2 * k:2 * k + 1, :] + pltpu.roll(xs, 64, 1) * mul_ref[gl, 2 * k + 1:2 * k + 2, :]
        xin = _shift_rows(x, 1)
        yg.append(y + _mm(xin.astype(BF16), wout_ref[gl]))
    for t in range(C):
        yt = jnp.concatenate([y[:, t * P:(t + 1) * P] for y in yg], axis=1)
        y_ref[:, t, :] = yt + d_ref[...] * ut[t]


def _gelu_tanh(x):
    return 0.5 * x * (1.0 + jnp.tanh(math.sqrt(2.0 / math.pi) * (x + 0.044715 * (x * x * x))))


def _merge_kernel(x_ref, g_ref, b0_ref, b1_ref, y5_ref, b3_ref, wg_ref, wglu_ref, wb_ref, wo_ref,
                  o_ref, *, npad, tiles_per_batch):
    T = x_ref.shape[0]
    x = x_ref[...]
    row0 = lax.rem(pl.program_id(0), tiles_per_batch) * T
    hn = _norm_in(x, g_ref[...], row0, npad)
    ya = _gelu_tanh(y5_ref[...])
    b2 = ya * _sigmoid(_mm(ya.astype(BF16), wglu_ref[...]))
    branches = (b0_ref[...], b1_ref[...], b2.astype(BF16), b3_ref[...])
    dm = x.shape[1]
    merged = None
    for m in range(N_BRANCH):
        gate = _sigmoid(_mm(hn, wg_ref[:, m * dm:(m + 1) * dm]))
        term = gate * _mm(branches[m], wb_ref[m])
        merged = term if merged is None else merged + term
    o_ref[...] = x + _mm(merged.astype(BF16), wo_ref[...])


def _ffn_kernel(x_ref, g_ref, wgu_ref, wd_ref, o_ref, *, n_split):
    x = x_ref[...]
    h = (x * lax.rsqrt(jnp.mean(x * x, axis=-1, keepdims=True) + EPS) * g_ref[...]).astype(BF16)
    dff = wd_ref.shape[0]
    cw = dff // n_split
    acc = x
    for c in range(n_split):
        gt = _mm(h, wgu_ref[:, c * cw:(c + 1) * cw])
        up = _mm(h, wgu_ref[:, dff + c * cw:dff + (c + 1) * cw])
        acc = acc + _mm((_silu(gt) * up).astype(BF16), wd_ref[c * cw:(c + 1) * cw, :])
    o_ref[...] = acc


def _final_kernel(x_ref, g_ref, o_ref):
    x = x_ref[...]
    o_ref[...] = x * lax.rsqrt(jnp.mean(x * x, axis=-1, keepdims=True) + EPS) * g_ref[...]


def _const_spec(block, index):
    return pl.BlockSpec(block, index, pipeline_mode=pl.Buffered(1))


def _layer_spec(shape, layer, ngrid):
    zeros = (0,) * len(shape)
    if ngrid == 1:
        return _const_spec((None,) + tuple(shape), lambda i: (layer,) + zeros)
    return _const_spec((None,) + tuple(shape), lambda b, i: (layer,) + zeros)


def _params(sem):
    return pltpu.CompilerParams(dimension_semantics=sem, vmem_limit_bytes=VMEM_LIMIT)


def _s5_constants(a_re, a_im, log_dt, b_re, b_im, c_re, c_im, nsteps):
    hp = lax.Precision.HIGHEST
    C, G, N, P = SSM_CHUNK, SSM_GROUPS, SSM_STATE, SSM_GROUP
    dt = jnp.exp(log_dt)[:, None]
    mag = jnp.exp(dt * a_re)
    ar, ai = mag * jnp.cos(dt * a_im), mag * jnp.sin(dt * a_im)
    den = a_re * a_re + a_im * a_im
    nr, ni = ar - 1.0, ai
    cr, ci = (nr * a_re + ni * a_im) / den, (ni * a_re - nr * a_im) / den
    bbr = cr[..., None] * b_re - ci[..., None] * b_im
    bbi = cr[..., None] * b_im + ci[..., None] * b_re
    pr, pi = [jnp.ones_like(ar)], [jnp.zeros_like(ar)]
    for _ in range(C):
        pr, pi = pr + [pr[-1] * ar - pi[-1] * ai], pi + [pr[-1] * ai + pi[-1] * ar]
    pr, pi = jnp.stack(pr), jnp.stack(pi)
    car = c_re[None] * pr[:C, :, None, :] - c_im[None] * pi[:C, :, None, :]
    cai = c_re[None] * pi[:C, :, None, :] + c_im[None] * pr[:C, :, None, :]
    kern = (jnp.einsum('tgpn,gnq->gtpq', car, bbr, precision=hp)
            - jnp.einsum('tgpn,gnq->gtpq', cai, bbi, precision=hp))
    lag = np.arange(C)[None, :] - np.arange(C)[:, None]
    m = jnp.where((lag >= 0)[None, :, :, None, None], kern[:, np.clip(lag, 0, C - 1)], 0.0)
    m = m.transpose(0, 1, 4, 2, 3).reshape(G, C * P, C * P)
    rev_r, rev_i = pr[C - 1::-1][:C], pi[C - 1::-1][:C]
    wst_r = rev_r[..., None] * bbr[None] - rev_i[..., None] * bbi[None]
    wst_i = rev_r[..., None] * bbi[None] + rev_i[..., None] * bbr[None]
    wst = jnp.concatenate([wst_r, wst_i], axis=2).transpose(1, 0, 3, 2).reshape(G, C * P, 2 * N)
    co_r = c_re[None] * pr[1:, :, None, :] - c_im[None] * pi[1:, :, None, :]
    co_i = c_re[None] * pi[1:, :, None, :] + c_im[None] * pr[1:, :, None, :]
    wout = jnp.concatenate([co_r, -co_i], axis=3).transpose(1, 3, 0, 2).reshape(G, 2 * N, C * P)
    sr, si = pr[C], pi[C]
    muls = []
    for _ in range(nsteps):
        muls += [jnp.concatenate([sr, sr], axis=1), jnp.concatenate([-si, si], axis=1)]
        sr, si = sr * sr - si * si, 2.0 * sr * si
    muls = jnp.stack(muls, axis=1)
    return m.astype(BF16), wst.astype(BF16), wout.astype(BF16), muls.astype(F32)


def _retention_tables(lp, npad, tile):
    half = RET_DK // 2
    pos = (jnp.arange(lp) - npad).astype(F32)
    inv = ROPE_BASE ** (-jnp.arange(half, dtype=F32) / half)
    ang = pos[:, None] * inv[None, :]
    cos, sin = jnp.cos(ang), jnp.sin(ang)
    cos_t = jnp.tile(jnp.concatenate([cos, cos], axis=1), (1, RET_HEADS))
    sin_t = jnp.tile(jnp.concatenate([-sin, sin], axis=1), (1, RET_HEADS))
    log_gamma = jnp.log1p(-jnp.exp2(-5.0 - jnp.arange(RET_HEADS, dtype=F32)))
    idx = jnp.arange(tile)
    loc = (idx % CHUNK).astype(F32)
    diff = (idx[:, None] - idx[None, :]).astype(F32)
    ok = ((idx[:, None] // CHUNK) == (idx[None, :] // CHUNK)) & (diff >= 0)
    dec = jnp.where(ok[None], jnp.exp(log_gamma[:, None, None] * jnp.maximum(diff, 0.0)[None]), 0.0)
    xi = jnp.repeat(jnp.exp(log_gamma[None, :] * (loc[:, None] + 1.0)), RET_DK, axis=1)
    zeta = jnp.repeat(jnp.exp(log_gamma[None, :] * (CHUNK - 1.0 - loc[:, None])), RET_DK, axis=1)
    gc = jnp.repeat(jnp.exp(log_gamma * CHUNK), RET_DK)[None, :]
    return cos_t, sin_t, dec, xi, zeta, gc


def kernel(x, meta, norm_mix, w_in, hg_lb_logits, hg_norm, ret_norm, ssm_a_re, ssm_a_im, ssm_log_dt,
           ssm_b_re, ssm_b_im, ssm_c_re, ssm_c_im, ssm_d, ssm_w_glu, gdn_conv, gdn_a_log, gdn_dt_bias,
           gdn_norm, w_branch, w_out, norm_ffn, w_gu, w_down, norm_final):
    bsz, seq, dm = x.shape
    depth = w_in.shape[0]
    T = ROW_TILE
    assert seq % T == 0 and dm % 128 == 0
    npad = (-(N_META + seq)) % T
    lp = npad + N_META + seq
    nt = lp // T
    rows = bsz * lp
    tm = next(t for t in DENSE_TILES if lp % t == 0)
    dff = w_down.shape[1]

    h = jnp.concatenate([jnp.zeros((bsz, npad, dm), F32),
                         jnp.broadcast_to(meta.astype(F32)[None], (bsz, N_META, dm)), x.astype(F32)], axis=1)
    h = h.reshape(rows, dm)

    wb16 = w_in.astype(BF16)
    w_hg = jnp.concatenate([wb16[:, :, 0:1536], wb16[:, :, 3072:3584]], axis=2)
    w_rt = wb16[:, :, 1536:3072]
    w_gd = jnp.concatenate([wb16[:, :, 3584:5120], wb16[:, :, 5128:5640],
                            jnp.pad(wb16[:, :, 5120:5128], ((0, 0), (0, 0), (0, 120)))], axis=2)
    w_ba = jnp.swapaxes(wb16[:, :, 5120:5128], 1, 2)
    w_gate = wb16[:, :, 5640:5640 + N_BRANCH * dm]
    w_glu16, w_br16, w_out16 = ssm_w_glu.astype(BF16), w_branch.astype(BF16), w_out.astype(BF16)
    w_gu16, w_down16 = w_gu.astype(BF16), w_down.astype(BF16)

    lb = jnp.cumsum(jax.nn.softmax(hg_lb_logits.astype(F32), axis=0), axis=0)
    lb = lb - lb[:1]
    lb_rows = jnp.stack([jnp.log(lb), jnp.log1p(-lb), 1.0 - lb] + [jnp.zeros_like(lb)] * 5, axis=1)

    neg_decay = -jnp.exp(gdn_a_log.astype(F32))
    zeros4 = jnp.zeros_like(neg_decay)
    hp_col = jnp.stack([jnp.pad(jnp.concatenate([zeros4, neg_decay], axis=1), ((0, 0), (0, 120))),
                        jnp.pad(jnp.concatenate([zeros4, gdn_dt_bias.astype(F32)], axis=1), ((0, 0), (0, 120)))]
                       + [jnp.zeros((depth, 128), F32)] * 6, axis=1)
    hp_row = jnp.stack([jnp.concatenate([zeros4, neg_decay], axis=1),
                        jnp.concatenate([zeros4, gdn_dt_bias.astype(F32)], axis=1)], axis=1)
    hp_row = jnp.broadcast_to(hp_row[..., None], (depth, 2, 8, T))

    cos_t, sin_t, dec, xi, zeta, gc = _retention_tables(lp, npad, T)

    rchunks = rows // SSM_CHUNK
    nsteps = max(1, int(math.ceil(math.log2(lp // SSM_CHUNK))))

    row_spec = lambda w: pl.BlockSpec((T, w), lambda b, i: (b * nt + i, 0))
    vec2 = lambda w, l: _const_spec((None, 1, w), lambda b, i: (l, 0, 0))
    mixer_params = _params(("arbitrary", "arbitrary"))

    for l in range(depth):
        gmix = norm_mix.astype(F32).reshape(depth, 1, dm)
        br0, u = pl.pallas_call(
            functools.partial(_hg_kernel, npad=npad),
            grid=(bsz, nt),
            in_specs=[row_spec(dm), vec2(dm, l), _layer_spec((dm, 2048), l, 2),
                      _layer_spec((8, 256), l, 2), vec2(512, l)],
            out_specs=[row_spec(512), row_spec(512)],
            out_shape=[jax.ShapeDtypeStruct((rows, 512), BF16), jax.ShapeDtypeStruct((rows, 512), F32)],
            scratch_shapes=[pltpu.VMEM((2, 256, 128), F32)],
            compiler_params=mixer_params, name="hgrn2_mixer",
        )(h, gmix, w_hg, lb_rows, hg_norm.astype(F32).reshape(depth, 1, 512))

        br1 = pl.pallas_call(
            functools.partial(_ret_kernel, npad=npad),
            grid=(bsz, nt),
            in_specs=[row_spec(dm), vec2(dm, l), _layer_spec((dm, 1536), l, 2),
                      pl.BlockSpec((T, 256), lambda b, i: (i, 0)), pl.BlockSpec((T, 256), lambda b, i: (i, 0)),
                      _const_spec((RET_HEADS, T, T), lambda b, i: (0, 0, 0)),
                      _const_spec((T, 256), lambda b, i: (0, 0)), _const_spec((T, 256), lambda b, i: (0, 0)),
                      _const_spec((1, 256), lambda b, i: (0, 0)), vec2(512, l)],
            out_specs=row_spec(512),
            out_shape=jax.ShapeDtypeStruct((rows, 512), BF16),
            scratch_shapes=[pltpu.VMEM((2, 256, 128), F32)],
            compiler_params=mixer_params, name="retention_mixer",
        )(h, gmix, w_rt, cos_t, sin_t, dec, xi, zeta, gc, ret_norm.astype(F32).reshape(depth, 1, 512))

        br3 = pl.pallas_call(
            functools.partial(_gdn_kernel, npad=npad),
            grid=(bsz, nt),
            in_specs=[row_spec(dm), vec2(dm, l), _layer_spec((dm, 2176), l, 2), _layer_spec((8, dm), l, 2),
                      _layer_spec((CONV_K, 1536), l, 2), _layer_spec((8, 128), l, 2),
                      _layer_spec((2, 8, T), l, 2), vec2(512, l)],
            out_specs=row_spec(512),
            out_shape=jax.ShapeDtypeStruct((rows, 512), BF16),
            scratch_shapes=[pltpu.VMEM((GDN_HEADS, GDN_DK, GDN_DV), F32), pltpu.VMEM((8, 1536), F32)],
            compiler_params=mixer_params, name="gdn_mixer",
        )(h, gmix, w_gd, w_ba, gdn_conv.astype(F32), hp_col, hp_row, gdn_norm.astype(F32).reshape(depth, 1, 512))

        m_k, wst, wout, muls = _s5_constants(
            ssm_a_re[l].astype(F32), ssm_a_im[l].astype(F32), ssm_log_dt[l].astype(F32),
            ssm_b_re[l].astype(F32), ssm_b_im[l].astype(F32), ssm_c_re[l].astype(F32), ssm_c_im[l].astype(F32),
            nsteps)
        cp = SSM_CHUNK * SSM_GROUP
        gb = S5_GROUP_BLOCK
        cb = rchunks // bsz
        gspec = lambda a, b_: pl.BlockSpec((gb, a, b_), lambda b, g: (g, 0, 0))
        uspec = pl.BlockSpec((cb, SSM_CHUNK, gb * SSM_GROUP), lambda b, g: (b, 0, g))
        y5 = pl.pallas_call(
            _s5_kernel,
            grid=(bsz, SSM_GROUPS // gb),
            in_specs=[uspec, gspec(cp, cp), gspec(cp, 2 * SSM_STATE), gspec(2 * SSM_STATE, cp),
                      gspec(2 * nsteps, 2 * SSM_STATE), pl.BlockSpec((1, gb * SSM_GROUP), lambda b, g: (0, g))],
            out_specs=uspec,
            out_shape=jax.ShapeDtypeStruct((rchunks, SSM_CHUNK, SSM_WIDTH), F32),
            compiler_params=_params(("arbitrary", "arbitrary")), name="s5_mixer",
        )(u.reshape(rchunks, SSM_CHUNK, SSM_WIDTH), m_k, wst, wout, muls, ssm_d[l].astype(F32).reshape(1, SSM_WIDTH))
        y5 = y5.reshape(rows, SSM_WIDTH)

        drow = lambda w: pl.BlockSpec((tm, w), lambda i: (i, 0))
        dvec1 = lambda w, a: _const_spec((None, 1, w), lambda i: (l, 0, 0))
        h = pl.pallas_call(
            functools.partial(_merge_kernel, npad=npad, tiles_per_batch=lp // tm),
            grid=(rows // tm,),
            in_specs=[drow(dm), dvec1(dm, None), drow(512), drow(512), drow(512), drow(512),
                      _layer_spec((dm, N_BRANCH * dm), l, 1), _layer_spec((512, 512), l, 1),
                      _layer_spec((N_BRANCH, BRANCH_WIDTH, dm), l, 1), _layer_spec((dm, dm), l, 1)],
            out_specs=drow(dm),
            out_shape=jax.ShapeDtypeStruct((rows, dm), F32),
            compiler_params=_params(("arbitrary",)), name="merge_out",
        )(h, gmix, br0, br1, y5, br3, w_gate, w_glu16, w_br16, w_out16)

        h = pl.pallas_call(
            functools.partial(_ffn_kernel, n_split=2),
            grid=(rows // tm,),
            in_specs=[drow(dm), dvec1(dm, None), _layer_spec((dm, 2 * dff), l, 1), _layer_spec((dff, dm), l, 1)],
            out_specs=drow(dm),
            out_shape=jax.ShapeDtypeStruct((rows, dm), F32),
            compiler_params=_params(("arbitrary",)), name="swiglu_ffn",
        )(h, norm_ffn.astype(F32).reshape(depth, 1, dm), w_gu16, w_down16)

    skip = (npad + N_META) // T
    assert skip * T == npad + N_META
    ns = seq // T
    out = pl.pallas_call(
        _final_kernel,
        grid=(bsz, ns),
        in_specs=[pl.BlockSpec((T, dm), lambda b, i: (b * nt + skip + i, 0)),
                  _const_spec((1, dm), lambda b, i: (0, 0))],
        out_specs=pl.BlockSpec((T, dm), lambda b, i: (b * ns + i, 0)),
        out_shape=jax.ShapeDtypeStruct((bsz * seq, dm), x.dtype),
        compiler_params=_params(("arbitrary", "arbitrary")), name="final_norm",
    )(h, norm_final.astype(F32).reshape(1, dm))
    return out.reshape(bsz, seq, dm)
```

```python
import functools
import math

import jax
import jax.numpy as jnp
import numpy as np
from jax import lax
from jax.experimental import pallas as pl
from jax.experimental.pallas import tpu as pltpu

F32 = jnp.float32
BF16 = jnp.bfloat16

N_META = 16
EPS = 1e-6
HG_HEADS, HG_DK, HG_DV, HG_CHUNK = 4, 64, 128, 16
RET_HEADS, RET_DK, RET_DV = 4, 64, 128
ROPE_BASE = 10000.0
SSM_WIDTH, SSM_GROUP, SSM_STATE = 512, 16, 64
SSM_GROUPS = SSM_WIDTH // SSM_GROUP
SSM_CHUNK = 16
S5_GROUP_BLOCK = 8
GDN_HEADS, GDN_DK, GDN_DV = 4, 128, 128
CONV_K = 4
CHUNK = 64
N_BRANCH = 4
BRANCH_WIDTH = 512

ROW_TILE = 128
DENSE_TILES = (640, 320, 128)
VMEM_LIMIT = 56 * 1024 * 1024

NT_DIMS = (((1,), (1,)), ((), ()))
TN_DIMS = (((0,), (0,)), ((), ()))


def _mm(a, b):
    return jnp.dot(a, b, preferred_element_type=F32)


def _mm_nt(a, b):
    return lax.dot_general(a, b, NT_DIMS, preferred_element_type=F32)


def _mm_tn(a, b):
    return lax.dot_general(a, b, TN_DIMS, preferred_element_type=F32)


def _split3(x):
    hi = x.astype(BF16)
    r = x - hi.astype(F32)
    mid = r.astype(BF16)
    lo = (r - mid.astype(F32)).astype(BF16)
    return hi, mid, lo


def _mm01(m01, x):
    hi, mid, lo = _split3(x)
    return _mm(m01, hi) + _mm(m01, mid) + _mm(m01, lo)


def _mm01_right(x, m01):
    hi, mid, lo = _split3(x)
    return _mm(hi, m01) + _mm(mid, m01) + _mm(lo, m01)


def _seg_masks(n, chunk):
    sh = int(math.log2(chunk))
    r = lax.broadcasted_iota(jnp.int32, (n, n), 0)
    c = lax.broadcasted_iota(jnp.int32, (n, n), 1)
    same = lax.shift_right_logical(r, sh) == lax.shift_right_logical(c, sh)
    return same, same & (c <= r), same & (c < r), same & (c >= r)


def _sigmoid(x):
    return jax.nn.sigmoid(x)


def _silu(x):
    return x * jax.nn.sigmoid(x)


def _softplus(x):
    return jnp.maximum(x, 0.0) + jnp.log1p(jnp.exp(-jnp.abs(x)))


def _norm_in(x, g, row0, npad):
    y = x * lax.rsqrt(jnp.mean(x * x, axis=-1, keepdims=True) + EPS) * g
    rows = row0 + lax.broadcasted_iota(jnp.int32, y.shape, 0)
    return jnp.where(rows >= npad, y, 0.0).astype(BF16)


def _stacked_norm_in(x_ref, g_ref, row0, npad):
    return jnp.concatenate([_norm_in(x_ref[b], g_ref[...], row0, npad) for b in range(x_ref.shape[0])], axis=0)


def _head_rms(o, width):
    parts = []
    for h in range(o.shape[1] // width):
        oh = o[:, h * width:(h + 1) * width]
        parts.append(oh * lax.rsqrt(jnp.mean(oh * oh, axis=-1, keepdims=True) + EPS))
    return jnp.concatenate(parts, axis=1)


def _pair_block_mask():
    r = lax.broadcasted_iota(jnp.int32, (256, 128), 0)
    c = lax.broadcasted_iota(jnp.int32, (256, 128), 1)
    return lax.shift_right_logical(r, 7) == lax.shift_right_logical(c, 6)


def _head_lane_mask(a):
    c = lax.broadcasted_iota(jnp.int32, (1, 128), 1)
    return lax.shift_right_logical(c, 6) == a


def _pair_state_scan(s_ref, kv, decay_row, nb, nchunk):
    entering = {}
    for b in range(nb):
        for p in range(2):
            st = s_ref[b, p]
            for n in range(nchunk):
                entering[b, p, n] = st.astype(BF16)
                st = decay_row(b, p, n) * st + kv[b, p, n]
            s_ref[b, p] = st
    return entering


def _hg_kernel(x_ref, g_ref, w_ref, lb_ref, ng_ref, o_ref, u_ref, s_ref, *, npad):
    nb, T, _ = x_ref.shape
    i = pl.program_id(0)

    @pl.when(i == 0)
    def _():
        s_ref[...] = jnp.zeros_like(s_ref)

    hn = _stacked_norm_in(x_ref, g_ref, i * T, npad)
    z = _mm(hn, w_ref[...])
    zq, zf, v, zg = z[:, 0:256], z[:, 256:512], z[:, 512:1024], z[:, 1024:1536]
    for b in range(nb):
        u_ref[b] = z[b * T:(b + 1) * T, 1536:2048]

    log_lb, log1m_lb, one_m_lb = lb_ref[0:1, :], lb_ref[1:2, :], lb_ref[2:3, :]
    e = jnp.exp(-jnp.abs(zf))
    inv1pe = 1.0 / (1.0 + e)
    log_sig = jnp.minimum(zf, 0.0) - jnp.log1p(e)
    y2 = log1m_lb + log_sig
    log_f = jnp.maximum(log_lb, y2) + jnp.log1p(jnp.exp(-jnp.abs(log_lb - y2)))
    k = one_m_lb * jnp.where(zf >= 0.0, e * inv1pe, inv1pe)
    q = _silu(zq) * (HG_DK ** -0.5)

    same, low, _, _ = _seg_masks(nb * T, HG_CHUNK)
    b_cum = _mm01(low.astype(BF16), log_f)
    b_last = _mm01(same.astype(BF16), log_f)
    q_t = q * jnp.exp(b_cum)
    k_t = (k * jnp.exp(-b_cum)).astype(BF16)
    k_end = (k * jnp.exp(b_last - b_cum)).astype(BF16)
    d_end = jnp.exp(b_last)
    q_tb = q_t.astype(BF16)
    vb = v.astype(BF16)
    low_t = low[0:T, 0:T]

    chains = [(b, h) for b in range(nb) for h in range(HG_HEADS)]
    rows = lambda b: slice(b * T, (b + 1) * T)
    slab = lambda p: slice(128 * p, 128 * p + 128)

    qh = [jnp.where(_head_lane_mask(h % 2), q_t[rows(b), slab(h // 2)], 0.0).astype(BF16) for b, h in chains]
    att = [jnp.where(low_t, _mm_nt(qh[c], k_t[rows(b), slab(h // 2)]), 0.0).astype(BF16)
           for c, (b, h) in enumerate(chains)]
    o_intra = [_mm(att[c], vb[rows(b), slab(h)]) for c, (b, h) in enumerate(chains)]

    nchunk = T // HG_CHUNK
    crow = lambda b, n: slice(b * T + n * HG_CHUNK, b * T + (n + 1) * HG_CHUNK)
    blk = _pair_block_mask()
    kv = {(b, p, n): jnp.where(blk, _mm_tn(vb[crow(b, n), 256 * p:256 * p + 256], k_end[crow(b, n), slab(p)]), 0.0)
          for b in range(nb) for n in range(nchunk) for p in range(2)}
    entering = _pair_state_scan(
        s_ref, kv, lambda b, p, n: d_end[b * T + n * HG_CHUNK:b * T + n * HG_CHUNK + 1, slab(p)], nb, nchunk)
    o_inter = {(b, p, n): _mm_nt(q_tb[crow(b, n), slab(p)], entering[b, p, n])
               for b in range(nb) for n in range(nchunk) for p in range(2)}

    o_rows = []
    for b in range(nb):
        inter = jnp.concatenate([jnp.concatenate([o_inter[b, 0, n], o_inter[b, 1, n]], axis=1)
                                 for n in range(nchunk)], axis=0)
        intra = jnp.concatenate([o_intra[b * HG_HEADS + h] for h in range(HG_HEADS)], axis=1)
        o_rows.append(intra + inter)
    o = jnp.concatenate(o_rows, axis=0)
    o = _head_rms(o, HG_DV) * ng_ref[...] * _silu(zg)
    for b in range(nb):
        o_ref[b] = o[b * T:(b + 1) * T].astype(o_ref.dtype)


def _rot_half_swap(x):
    lane = lax.broadcasted_iota(jnp.int32, x.shape, 1)
    first_half = (lane & 63) < 32
    return jnp.where(first_half, pltpu.roll(x, x.shape[1] - 32, 1), pltpu.roll(x, 32, 1))


def _ret_kernel(x_ref, g_ref, w_ref, cos_ref, sin_ref, dec_ref, xi_ref, zeta_ref, gc_ref, ng_ref,
                o_ref, s_ref, *, npad):
    nb, T, _ = x_ref.shape
    i = pl.program_id(0)

    @pl.when(i == 0)
    def _():
        s_ref[...] = jnp.zeros_like(s_ref)

    hn = _stacked_norm_in(x_ref, g_ref, i * T, npad)
    z = _mm(hn, w_ref[...])
    zq, zk, v, zg = z[:, 0:256], z[:, 256:512], z[:, 512:1024], z[:, 1024:1536]
    tile_b = lambda a: jnp.concatenate([a] * nb, axis=0)
    cos, sin = tile_b(cos_ref[...]), tile_b(sin_ref[...])
    q = zq * cos + _rot_half_swap(zq) * sin
    k = (zk * cos + _rot_half_swap(zk) * sin) * (RET_DK ** -0.5)
    kb, vb = k.astype(BF16), v.astype(BF16)
    q_xi = (q * tile_b(xi_ref[...])).astype(BF16)
    k_zeta = (k * tile_b(zeta_ref[...])).astype(BF16)

    chains = [(b, h) for b in range(nb) for h in range(RET_HEADS)]
    rows = lambda b: slice(b * T, (b + 1) * T)
    slab = lambda p: slice(128 * p, 128 * p + 128)

    qh = [jnp.where(_head_lane_mask(h % 2), q[rows(b), slab(h // 2)], 0.0).astype(BF16) for b, h in chains]
    att = [(_mm_nt(qh[c], kb[rows(b), slab(h // 2)]) * dec_ref[h]).astype(BF16) for c, (b, h) in enumerate(chains)]
    o_intra = [_mm(att[c], vb[rows(b), slab(h)]) for c, (b, h) in enumerate(chains)]

    nchunk = T // CHUNK
    crow = lambda b, n: slice(b * T + n * CHUNK, b * T + (n + 1) * CHUNK)
    blk = _pair_block_mask()
    kv = {(b, p, n): jnp.where(blk, _mm_tn(vb[crow(b, n), 256 * p:256 * p + 256], k_zeta[crow(b, n), slab(p)]), 0.0)
          for b in range(nb) for n in range(nchunk) for p in range(2)}
    entering = _pair_state_scan(s_ref, kv, lambda b, p, n: gc_ref[:, slab(p)], nb, nchunk)
    o_inter = {(b, p, n): _mm_nt(q_xi[crow(b, n), slab(p)], entering[b, p, n])
               for b in range(nb) for n in range(nchunk) for p in range(2)}

    o_rows = []
    for b in range(nb):
        inter = jnp.concatenate([jnp.concatenate([o_inter[b, 0, n], o_inter[b, 1, n]], axis=1)
                                 for n in range(nchunk)], axis=0)
        intra = jnp.concatenate([o_intra[b * RET_HEADS + h] for h in range(RET_HEADS)], axis=1)
        o_rows.append(intra + inter)
    o = jnp.concatenate(o_rows, axis=0)

    parts = []
    for h in range(RET_HEADS):
        oh = o[:, 128 * h:128 * h + 128]
        oc = oh - jnp.mean(oh, axis=-1, keepdims=True)
        parts.append(oc * lax.rsqrt(jnp.mean(oc * oc, axis=-1, keepdims=True) + EPS))
    o = jnp.concatenate(parts, axis=1) * ng_ref[...] * _silu(zg)
    for b in range(nb):
        o_ref[b] = o[b * T:(b + 1) * T].astype(o_ref.dtype)


def _gdn_kernel(x_ref, g_ref, w_ref, wba_ref, conv_ref, hp_col_ref, hp_row_ref, ng_ref,
                o_ref, s_ref, carry_ref, *, npad):
    nb, T, _ = x_ref.shape
    H = GDN_HEADS
    assert T == 128
    i = pl.program_id(0)

    @pl.when(i == 0)
    def _():
        s_ref[...] = jnp.zeros_like(s_ref)
        carry_ref[...] = jnp.zeros_like(carry_ref)

    hn = _stacked_norm_in(x_ref, g_ref, i * T, npad)
    z = _mm(hn, w_ref[...])
    zqkv, zg, zba = z[:, 0:1536], z[:, 1536:2048], z[:, 2048:2176]
    ba_row = _mm_nt(wba_ref[...], hn)

    xc = []
    for b in range(nb):
        zb = zqkv[b * T:(b + 1) * T]
        ext = jnp.concatenate([carry_ref[b], zb], axis=0)
        carry_ref[b] = zb[T - 8:T, :]
        acc = conv_ref[3:4, :] * zb
        for j in range(CONV_K - 1):
            s = CONV_K - 1 - j
            acc = acc + conv_ref[j:j + 1, :] * ext[8 - s:8 - s + T, :]
        xc.append(acc)
    qkv = _silu(jnp.concatenate(xc, axis=0))

    beta_col = _sigmoid(zba)
    g_col = hp_col_ref[0:1, :] * _softplus(zba + hp_col_ref[1:2, :])
    g_row = jnp.concatenate([hp_row_ref[0]] * nb, axis=1) * _softplus(ba_row + jnp.concatenate([hp_row_ref[1]] * nb, axis=1))
    same, low, strict, up = _seg_masks(nb * T, CHUNK)
    cum_col = _mm01(low.astype(BF16), g_col)
    last_col = _mm01(same.astype(BF16), g_col)
    cum_row = _mm01_right(g_row, up.astype(BF16))
    low_t, strict_t = low[0:T, 0:T], strict[0:T, 0:T]
    eye = (lax.broadcasted_iota(jnp.int32, (T, T), 0) == lax.broadcasted_iota(jnp.int32, (T, T), 1)).astype(F32)

    chains = [(b, h) for b in range(nb) for h in range(H)]
    CS = range(len(chains))
    rows = lambda b: slice(b * T, (b + 1) * T)
    qn, kn, knb, vh, cum_c, last_c, beta_c, gamma = [], [], [], [], [], [], [], []
    for b, h in chains:
        qh = qkv[rows(b), 128 * h:128 * h + 128]
        kh = qkv[rows(b), 512 + 128 * h:512 + 128 * h + 128]
        vh.append(qkv[rows(b), 1024 + 128 * h:1024 + 128 * h + 128])
        qn.append(qh * lax.rsqrt(jnp.sum(qh * qh, axis=-1, keepdims=True) + EPS) * (GDN_DK ** -0.5))
        kn.append(kh * lax.rsqrt(jnp.sum(kh * kh, axis=-1, keepdims=True) + EPS))
        knb.append(kn[-1].astype(BF16))
        cum_c.append(jnp.broadcast_to(cum_col[rows(b), H + h:H + h + 1], (T, 128)))
        last_c.append(jnp.broadcast_to(last_col[rows(b), H + h:H + h + 1], (T, 128)))
        beta_c.append(jnp.broadcast_to(beta_col[rows(b), h:h + 1], (T, 128)))
        gamma.append(jnp.exp(jnp.where(low_t, cum_c[-1] - cum_row[H + h:H + h + 1, rows(b)], -jnp.inf)))

    kk = [_mm_nt(knb[c], knb[c]) for c in CS]
    qk = [_mm_nt(qn[c].astype(BF16), knb[c]) for c in CS]
    a0 = [jnp.where(strict_t, kk[c] * beta_c[c] * gamma[c], 0.0) for c in CS]
    inv = [eye - a0[c] for c in CS]
    pw = a0
    for _ in range(int(math.log2(CHUNK)) - 1):
        pwb = [p.astype(BF16) for p in pw]
        pw = [_mm(pwb[c], pwb[c]) for c in CS]
        inv = [inv[c] + _mm(inv[c].astype(BF16), pw[c].astype(BF16)) for c in CS]
    e_cum = [jnp.exp(cum_c[c]) for c in CS]
    uw = [_mm(inv[c].astype(BF16),
              jnp.concatenate([vh[c] * beta_c[c], kn[c] * beta_c[c] * e_cum[c]], axis=1).astype(BF16)) for c in CS]
    u = [x[:, 0:128] for x in uw]
    wb = [x[:, 128:256].astype(BF16) for x in uw]
    att = [(qk[c] * gamma[c]).astype(BF16) for c in CS]
    qd = [(qn[c] * e_cum[c]).astype(BF16) for c in CS]
    kd = [(kn[c] * jnp.exp(last_c[c] - cum_c[c])).astype(BF16) for c in CS]
    dl = [jnp.exp(last_c[c]) for c in CS]

    st = [s_ref[b, h] for b, h in chains]
    v_rows = [[] for _ in CS]
    o_rows = [[] for _ in CS]
    for n in range(T // CHUNK):
        r = slice(n * CHUNK, (n + 1) * CHUNK)
        stb = [st[c].astype(BF16) for c in CS]
        v_new = [u[c][r] - _mm(wb[c][r], stb[c]) for c in CS]
        for c in CS:
            v_rows[c].append(v_new[c])
            rest = [u[c][(n + 1) * CHUNK:]] if (n + 1) * CHUNK < T else []
            v_all = jnp.concatenate(v_rows[c] + rest, axis=0)
            o_rows[c].append(_mm(qd[c][r], stb[c]) + _mm(att[c][r], v_all.astype(BF16)))
        st = [dl[c][n * CHUNK:n * CHUNK + 1, :] * st[c] + _mm_tn(kd[c][r], v_new[c].astype(BF16)) for c in CS]
    for c, (b, h) in enumerate(chains):
        s_ref[b, h] = st[c]
    o = jnp.concatenate([jnp.concatenate([jnp.concatenate(o_rows[b * H + h], axis=0) for h in range(H)], axis=1)
                         for b in range(nb)], axis=0)
    o = _head_rms(o, GDN_DV) * ng_ref[...] * _silu(zg)
    for b in range(nb):
        o_ref[b] = o[b * T:(b + 1) * T].astype(o_ref.dtype)


def _shift_rows(x, s):
    n = x.shape[0]
    if s % 8 == 0:
        return jnp.concatenate([jnp.zeros((s, x.shape[1]), x.dtype), x[:n - s]], axis=0)
    rows = lax.broadcasted_iota(jnp.int32, x.shape, 0)
    return jnp.where(rows >= s, pltpu.roll(x, s, 0), 0.0)


def _s5_kernel(u_ref, m_ref, wst_ref, wout_ref, mul_ref, d_ref, y_ref):
    C, P = SSM_CHUNK, SSM_GROUP
    rb = u_ref.shape[0]
    nsteps = mul_ref.shape[1] // 2
    ut = [u_ref[:, t, :] for t in range(C)]
    yg = []
    for gl in range(u_ref.shape[2] // P):
        ug = jnp.concatenate([ut[t][:, gl * P:(gl + 1) * P] for t in range(C)], axis=1)
        ub = ug.astype(BF16)
        y = _mm(ub, m_ref[gl])
        x = _mm(ub, wst_ref[gl])
        for k in range(nsteps):
            s = 1 << k
            if s >= rb:
                break
            xs = _shift_rows(x, s)
            x = x + xs * mul_ref[gl, 2 * k:2 * k + 1, :] + pltpu.roll(xs, 64, 1) * mul_ref[gl, 2 * k + 1:2 * k + 2, :]
        xin = _shift_rows(x, 1)
        yg.append(y + _mm(xin.astype(BF16), wout_ref[gl]))
    for t in range(C):
        yt = jnp.concatenate([y[:, t * P:(t + 1) * P] for y in yg], axis=1)
        y_ref[:, t, :] = yt + d_ref[...] * ut[t]


def _gelu_tanh(x):
    return 0.5 * x * (1.0 + jnp.tanh(math.sqrt(2.0 / math.pi) * (x + 0.044715 * (x * x * x))))


def _merge_kernel(x_ref, g_ref, b0_ref, b1_ref, y5_ref, b3_ref, wg_ref, wglu_ref, wb_ref, wo_ref,
                  o_ref, *, npad, tiles_per_batch):
    T = x_ref.shape[0]
    x = x_ref[...]
    row0 = lax.rem(pl.program_id(0), tiles_per_batch) * T
    hn = _norm_in(x, g_ref[...], row0, npad)
    ya = _gelu_tanh(y5_ref[...])
    b2 = ya * _sigmoid(_mm(ya.astype(BF16), wglu_ref[...]))
    branches = (b0_ref[...], b1_ref[...], b2.astype(BF16), b3_ref[...])
    dm = x.shape[1]
    merged = None
    for m in range(N_BRANCH):
        gate = _sigmoid(_mm(hn, wg_ref[:, m * dm:(m + 1) * dm]))
        term = gate * _mm(branches[m], wb_ref[m])
        merged = term if merged is None else merged + term
    o_ref[...] = x + _mm(merged.astype(BF16), wo_ref[...])


def _ffn_kernel(x_ref, g_ref, wgu_ref, wd_ref, o_ref, *, n_split):
    x = x_ref[...]
    h = (x * lax.rsqrt(jnp.mean(x * x, axis=-1, keepdims=True) + EPS) * g_ref[...]).astype(BF16)
    dff = wd_ref.shape[0]
    cw = dff // n_split
    acc = x
    for c in range(n_split):
        gt = _mm(h, wgu_ref[:, c * cw:(c + 1) * cw])
        up = _mm(h, wgu_ref[:, dff + c * cw:dff + (c + 1) * cw])
        acc = acc + _mm((_silu(gt) * up).astype(BF16), wd_ref[c * cw:(c + 1) * cw, :])
    o_ref[...] = acc


def _final_kernel(x_ref, g_ref, o_ref):
    x = x_ref[...]
    o_ref[...] = x * lax.rsqrt(jnp.mean(x * x, axis=-1, keepdims=True) + EPS) * g_ref[...]


def _const_spec(block, index):
    return pl.BlockSpec(block, index, pipeline_mode=pl.Buffered(1))


def _layer_spec(shape, layer):
    zeros = (0,) * len(shape)
    return _const_spec((None,) + tuple(shape), lambda i: (layer,) + zeros)


def _params(sem):
    return pltpu.CompilerParams(dimension_semantics=sem, vmem_limit_bytes=VMEM_LIMIT)


def _s5_constants(a_re, a_im, log_dt, b_re, b_im, c_re, c_im, nsteps):
    hp = lax.Precision.HIGHEST
    C, G, N, P = SSM_CHUNK, SSM_GROUPS, SSM_STATE, SSM_GROUP
    L = a_re.shape[0]
    dt = jnp.exp(log_dt)[..., None]
    mag = jnp.exp(dt * a_re)
    ar, ai = mag * jnp.cos(dt * a_im), mag * jnp.sin(dt * a_im)
    den = a_re * a_re + a_im * a_im
    nr, ni = ar - 1.0, ai
    cr, ci = (nr * a_re + ni * a_im) / den, (ni * a_re - nr * a_im) / den
    bbr = cr[..., None] * b_re - ci[..., None] * b_im
    bbi = cr[..., None] * b_im + ci[..., None] * b_re
    pr, pi = [jnp.ones_like(ar)], [jnp.zeros_like(ar)]
    for _ in range(C):
        pr, pi = pr + [pr[-1] * ar - pi[-1] * ai], pi + [pr[-1] * ai + pi[-1] * ar]
    pr, pi = jnp.stack(pr), jnp.stack(pi)
    car = c_re[None] * pr[:C, :, :, None, :] - c_im[None] * pi[:C, :, :, None, :]
    cai = c_re[None] * pi[:C, :, :, None, :] + c_im[None] * pr[:C, :, :, None, :]
    kern = (jnp.einsum('tlgpn,lgnq->lgtpq', car, bbr, precision=hp)
            - jnp.einsum('tlgpn,lgnq->lgtpq', cai, bbi, precision=hp))
    lag = np.arange(C)[None, :] - np.arange(C)[:, None]
    shift = jnp.asarray((lag[None] == np.arange(C)[:, None, None]).astype(np.float32))
    m = jnp.einsum('lgtpq,tsu->lgsqup', kern, shift, precision=hp).reshape(L, G, C * P, C * P)
    rev_r, rev_i = pr[C - 1::-1][:C], pi[C - 1::-1][:C]
    wst_r = rev_r[..., None] * bbr[None] - rev_i[..., None] * bbi[None]
    wst_i = rev_r[..., None] * bbi[None] + rev_i[..., None] * bbr[None]
    wst = jnp.concatenate([wst_r, wst_i], axis=3).transpose(1, 2, 0, 4, 3).reshape(L, G, C * P, 2 * N)
    co_r = c_re[None] * pr[1:, :, :, None, :] - c_im[None] * pi[1:, :, :, None, :]
    co_i = c_re[None] * pi[1:, :, :, None, :] + c_im[None] * pr[1:, :, :, None, :]
    wout = jnp.concatenate([co_r, -co_i], axis=4).transpose(1, 2, 4, 0, 3).reshape(L, G, 2 * N, C * P)
    sr, si = pr[C], pi[C]
    muls = []
    for _ in range(nsteps):
        muls += [jnp.concatenate([sr, sr], axis=-1), jnp.concatenate([-si, si], axis=-1)]
        sr, si = sr * sr - si * si, 2.0 * sr * si
    muls = jnp.stack(muls, axis=2)
    return m.astype(BF16), wst.astype(BF16), wout.astype(BF16), muls.astype(F32)


def _retention_tables(lp, npad, tile):
    half = RET_DK // 2
    pos = (jnp.arange(lp) - npad).astype(F32)
    inv = ROPE_BASE ** (-jnp.arange(half, dtype=F32) / half)
    ang = pos[:, None] * inv[None, :]
    cos, sin = jnp.cos(ang), jnp.sin(ang)
    cos_t = jnp.tile(jnp.concatenate([cos, cos], axis=1), (1, RET_HEADS))
    sin_t = jnp.tile(jnp.concatenate([-sin, sin], axis=1), (1, RET_HEADS))
    log_gamma = jnp.log1p(-jnp.exp2(-5.0 - jnp.arange(RET_HEADS, dtype=F32)))
    idx = jnp.arange(tile)
    loc = (idx % CHUNK).astype(F32)
    diff = (idx[:, None] - idx[None, :]).astype(F32)
    ok = ((idx[:, None] // CHUNK) == (idx[None, :] // CHUNK)) & (diff >= 0)
    dec = jnp.where(ok[None], jnp.exp(log_gamma[:, None, None] * jnp.maximum(diff, 0.0)[None]), 0.0)
    xi = jnp.repeat(jnp.exp(log_gamma[None, :] * (loc[:, None] + 1.0)), RET_DK, axis=1)
    zeta = jnp.repeat(jnp.exp(log_gamma[None, :] * (CHUNK - 1.0 - loc[:, None])), RET_DK, axis=1)
    gc = jnp.repeat(jnp.exp(log_gamma * CHUNK), RET_DK)[None, :]
    return cos_t, sin_t, dec, xi, zeta, gc


def kernel(x, meta, norm_mix, w_in, hg_lb_logits, hg_norm, ret_norm, ssm_a_re, ssm_a_im, ssm_log_dt,
           ssm_b_re, ssm_b_im, ssm_c_re, ssm_c_im, ssm_d, ssm_w_glu, gdn_conv, gdn_a_log, gdn_dt_bias,
           gdn_norm, w_branch, w_out, norm_ffn, w_gu, w_down, norm_final):
    bsz, seq, dm = x.shape
    depth = w_in.shape[0]
    T = ROW_TILE
    assert seq % T == 0 and dm % 128 == 0
    npad = (-(N_META + seq)) % T
    lp = npad + N_META + seq
    nt = lp // T
    rows = bsz * lp
    tm = next(t for t in DENSE_TILES if lp % t == 0)
    dff = w_down.shape[1]

    h = jnp.concatenate([jnp.zeros((bsz, npad, dm), F32),
                         jnp.broadcast_to(meta.astype(F32)[None], (bsz, N_META, dm)), x.astype(F32)], axis=1)
    h = h.reshape(rows, dm)

    wb16 = w_in.astype(BF16)
    w_hg = jnp.concatenate([wb16[:, :, 0:1536], wb16[:, :, 3072:3584]], axis=2)
    w_rt = wb16[:, :, 1536:3072]
    w_gd = jnp.concatenate([wb16[:, :, 3584:5120], wb16[:, :, 5128:5640],
                            jnp.pad(wb16[:, :, 5120:5128], ((0, 0), (0, 0), (0, 120)))], axis=2)
    w_ba = jnp.swapaxes(wb16[:, :, 5120:5128], 1, 2)
    w_gate = wb16[:, :, 5640:5640 + N_BRANCH * dm]
    w_glu16, w_br16, w_out16 = ssm_w_glu.astype(BF16), w_branch.astype(BF16), w_out.astype(BF16)
    w_gu16, w_down16 = w_gu.astype(BF16), w_down.astype(BF16)

    lb = jnp.cumsum(jax.nn.softmax(hg_lb_logits.astype(F32), axis=0), axis=0)
    lb = lb - lb[:1]
    lb_rows = jnp.stack([jnp.log(lb), jnp.log1p(-lb), 1.0 - lb] + [jnp.zeros_like(lb)] * 5, axis=1)

    neg_decay = -jnp.exp(gdn_a_log.astype(F32))
    zeros4 = jnp.zeros_like(neg_decay)
    hp_col = jnp.stack([jnp.pad(jnp.concatenate([zeros4, neg_decay], axis=1), ((0, 0), (0, 120))),
                        jnp.pad(jnp.concatenate([zeros4, gdn_dt_bias.astype(F32)], axis=1), ((0, 0), (0, 120)))]
                       + [jnp.zeros((depth, 128), F32)] * 6, axis=1)
    hp_row = jnp.stack([jnp.concatenate([zeros4, neg_decay], axis=1),
                        jnp.concatenate([zeros4, gdn_dt_bias.astype(F32)], axis=1)], axis=1)
    hp_row = jnp.broadcast_to(hp_row[..., None], (depth, 2, 8, T))

    cos_t, sin_t, dec, xi, zeta, gc = _retention_tables(lp, npad, T)

    rchunks = rows // SSM_CHUNK
    nsteps = max(1, int(math.ceil(math.log2(lp // SSM_CHUNK))))
    f32 = lambda a: a.astype(F32)
    m_k, wst, wout, muls = _s5_constants(f32(ssm_a_re), f32(ssm_a_im), f32(ssm_log_dt), f32(ssm_b_re),
                                         f32(ssm_b_im), f32(ssm_c_re), f32(ssm_c_im), nsteps)

    gmix = norm_mix.astype(F32).reshape(depth, 1, dm)
    row3 = lambda w: pl.BlockSpec((bsz, T, w), lambda i: (0, i, 0))
    tile_tab = lambda w: pl.BlockSpec((T, w), lambda i: (i, 0))
    fixed = lambda *shape: _const_spec(shape, lambda i: (0,) * len(shape))
    vec = lambda w, l: _layer_spec((1, w), l)
    seq_params = _params(("arbitrary",))

    for l in range(depth):
        h3 = h.reshape(bsz, lp, dm)
        br0, u = pl.pallas_call(
            functools.partial(_hg_kernel, npad=npad),
            grid=(nt,),
            in_specs=[row3(dm), vec(dm, l), _layer_spec((dm, 2048), l), _layer_spec((8, 256), l), vec(512, l)],
            out_specs=[row3(512), row3(512)],
            out_shape=[jax.ShapeDtypeStruct((bsz, lp, 512), BF16), jax.ShapeDtypeStruct((bsz, lp, 512), F32)],
            scratch_shapes=[pltpu.VMEM((bsz, 2, 256, 128), F32)],
            compiler_params=seq_params, name="hgrn2_mixer",
        )(h3, gmix, w_hg, lb_rows, hg_norm.astype(F32).reshape(depth, 1, 512))

        br1 = pl.pallas_call(
            functools.partial(_ret_kernel, npad=npad),
            grid=(nt,),
            in_specs=[row3(dm), vec(dm, l), _layer_spec((dm, 1536), l), tile_tab(256), tile_tab(256),
                      fixed(RET_HEADS, T, T), fixed(T, 256), fixed(T, 256), fixed(1, 256), vec(512, l)],
            out_specs=row3(512),
            out_shape=jax.ShapeDtypeStruct((bsz, lp, 512), BF16),
            scratch_shapes=[pltpu.VMEM((bsz, 2, 256, 128), F32)],
            compiler_params=seq_params, name="retention_mixer",
        )(h3, gmix, w_rt, cos_t, sin_t, dec, xi, zeta, gc, ret_norm.astype(F32).reshape(depth, 1, 512))

        br3 = pl.pallas_call(
            functools.partial(_gdn_kernel, npad=npad),
            grid=(nt,),
            in_specs=[row3(dm), vec(dm, l), _layer_spec((dm, 2176), l), _layer_spec((8, dm), l),
                      _layer_spec((CONV_K, 1536), l), _layer_spec((8, 128), l), _layer_spec((2, 8, T), l), vec(512, l)],
            out_specs=row3(512),
            out_shape=jax.ShapeDtypeStruct((bsz, lp, 512), BF16),
            scratch_shapes=[pltpu.VMEM((bsz, GDN_HEADS, GDN_DK, GDN_DV), F32), pltpu.VMEM((bsz, 8, 1536), F32)],
            compiler_params=seq_params, name="gdn_mixer",
        )(h3, gmix, w_gd, w_ba, gdn_conv.astype(F32), hp_col, hp_row, gdn_norm.astype(F32).reshape(depth, 1, 512))
        br0, br1, br3 = (a.reshape(rows, 512) for a in (br0, br1, br3))

        cp = SSM_CHUNK * SSM_GROUP
        gb = S5_GROUP_BLOCK
        cb = rchunks // bsz
        gspec = lambda a, b_: pl.BlockSpec((None, gb, a, b_), lambda b, g: (l, g, 0, 0))
        uspec = pl.BlockSpec((cb, SSM_CHUNK, gb * SSM_GROUP), lambda b, g: (b, 0, g))
        y5 = pl.pallas_call(
            _s5_kernel,
            grid=(bsz, SSM_GROUPS // gb),
            in_specs=[uspec, gspec(cp, cp), gspec(cp, 2 * SSM_STATE), gspec(2 * SSM_STATE, cp),
                      gspec(2 * nsteps, 2 * SSM_STATE),
                      pl.BlockSpec((None, 1, gb * SSM_GROUP), lambda b, g: (l, 0, g))],
            out_specs=uspec,
            out_shape=jax.ShapeDtypeStruct((rchunks, SSM_CHUNK, SSM_WIDTH), F32),
            compiler_params=_params(("arbitrary", "arbitrary")), name="s5_mixer",
        )(u.reshape(rchunks, SSM_CHUNK, SSM_WIDTH), m_k, wst, wout, muls,
          ssm_d.astype(F32).reshape(depth, 1, SSM_WIDTH))
        y5 = y5.reshape(rows, SSM_WIDTH)

        drow = lambda w: pl.BlockSpec((tm, w), lambda i: (i, 0))
        h = pl.pallas_call(
            functools.partial(_merge_kernel, npad=npad, tiles_per_batch=lp // tm),
            grid=(rows // tm,),
            in_specs=[drow(dm), vec(dm, l), drow(512), drow(512), drow(512), drow(512),
                      _layer_spec((dm, N_BRANCH * dm), l), _layer_spec((512, 512), l),
                      _layer_spec((N_BRANCH, BRANCH_WIDTH, dm), l), _layer_spec((dm, dm), l)],
            out_specs=drow(dm),
            out_shape=jax.ShapeDtypeStruct((rows, dm), F32),
            compiler_params=seq_params, name="merge_out",
        )(h, gmix, br0, br1, y5, br3, w_gate, w_glu16, w_br16, w_out16)

        h = pl.pallas_call(
            functools.partial(_ffn_kernel, n_split=2),
            grid=(rows // tm,),
            in_specs=[drow(dm), vec(dm, l), _layer_spec((dm, 2 * dff), l), _layer_spec((dff, dm), l)],
            out_specs=drow(dm),
            out_shape=jax.ShapeDtypeStruct((rows, dm), F32),
            compiler_params=seq_params, name="swiglu_ffn",
        )(h, norm_ffn.astype(F32).reshape(depth, 1, dm), w_gu16, w_down16)

    skip = (npad + N_META) // T
    assert skip * T == npad + N_META
    ns = seq // T
    out = pl.pallas_call(
        _final_kernel,
        grid=(bsz, ns),
        in_specs=[pl.BlockSpec((T, dm), lambda b, i: (b * nt + skip + i, 0)),
                  _const_spec((1, dm), lambda b, i: (0, 0))],
        out_specs=pl.BlockSpec((T, dm), lambda b, i: (b * ns + i, 0)),
        out_shape=jax.ShapeDtypeStruct((bsz * seq, dm), x.dtype),
        compiler_params=_params(("arbitrary", "arbitrary")), name="final_norm",
    )(h, norm_final.astype(F32).reshape(1, dm))
    return out.reshape(bsz, seq, dm)
```

```python
import functools
import math

import jax
import jax.numpy as jnp
from jax import lax
from jax.experimental import pallas as pl
from jax.experimental.pallas import tpu as pltpu

F32 = jnp.float32
BF16 = jnp.bfloat16

N_META = 16
EPS = 1e-6
HG_HEADS, HG_DK, HG_DV, HG_CHUNK = 4, 64, 128, 16
RET_HEADS, RET_DK, RET_DV = 4, 64, 128
ROPE_BASE = 10000.0
SSM_WIDTH, SSM_GROUP, SSM_STATE = 512, 16, 64
SSM_GROUPS = SSM_WIDTH // SSM_GROUP
SSM_CHUNK = 16
S5_GROUP_BLOCK = 8
GDN_HEADS, GDN_DK, GDN_DV = 4, 128, 128
CONV_K = 4
CHUNK = 64
N_BRANCH = 4
BRANCH_WIDTH = 512

ROW_TILE = 128
MIXER_INNER_TILES = (5, 4, 3, 2, 1)
DENSE_TILES = (640, 320, 128)
FINAL_TILES = (1024, 512, 128)
FFN_SPLITS = 1
VMEM_LIMIT = 56 * 1024 * 1024

NT_DIMS = (((1,), (1,)), ((), ()))
TN_DIMS = (((0,), (0,)), ((), ()))


def _mm(a, b):
    return jnp.dot(a, b, preferred_element_type=F32)


def _mm_nt(a, b):
    return lax.dot_general(a, b, NT_DIMS, preferred_element_type=F32)


def _mm_tn(a, b):
    return lax.dot_general(a, b, TN_DIMS, preferred_element_type=F32)


def _split3(x):
    hi = x.astype(BF16)
    r = x - hi.astype(F32)
    mid = r.astype(BF16)
    lo = (r - mid.astype(F32)).astype(BF16)
    return hi, mid, lo


def _mm01(m01, x):
    hi, mid, lo = _split3(x)
    return _mm(m01, hi) + _mm(m01, mid) + _mm(m01, lo)


def _mm01_right(x, m01):
    hi, mid, lo = _split3(x)
    return _mm(hi, m01) + _mm(mid, m01) + _mm(lo, m01)


def _seg_masks(n, chunk):
    sh = int(math.log2(chunk))
    r = lax.broadcasted_iota(jnp.int32, (n, n), 0)
    c = lax.broadcasted_iota(jnp.int32, (n, n), 1)
    same = lax.shift_right_logical(r, sh) == lax.shift_right_logical(c, sh)
    return same, same & (c <= r), same & (c < r), same & (c >= r)


def _sigmoid(x):
    return jax.nn.sigmoid(x)


def _silu(x):
    return x * jax.nn.sigmoid(x)


def _softplus(x):
    return jnp.maximum(x, 0.0) + jnp.log1p(jnp.exp(-jnp.abs(x)))


def _norm_in(x, g, row0, npad):
    y = x * lax.rsqrt(jnp.mean(x * x, axis=-1, keepdims=True) + EPS) * g
    rows = row0 + lax.broadcasted_iota(jnp.int32, y.shape, 0)
    return jnp.where(rows >= npad, y, 0.0).astype(BF16)


def _stacked_norm_in(x_ref, rs, g_ref, row0, npad):
    return jnp.concatenate([_norm_in(x_ref[b, rs, :], g_ref[...], row0, npad) for b in range(x_ref.shape[0])], axis=0)


def _tiled_mixer(tile_fn, n_scratch, *refs, npad, n_inner):
    i = pl.program_id(0)

    @pl.when(i == 0)
    def _():
        for r in refs[len(refs) - n_scratch:]:
            r[...] = jnp.zeros_like(r)

    def step(j, carry):
        tile_fn(i * n_inner + j, pl.ds(pl.multiple_of(j * ROW_TILE, ROW_TILE), ROW_TILE), *refs, npad=npad)
        return carry

    lax.fori_loop(0, n_inner, step, 0)


def _head_rms(o, width):
    parts = []
    for h in range(o.shape[1] // width):
        oh = o[:, h * width:(h + 1) * width]
        parts.append(oh * lax.rsqrt(jnp.mean(oh * oh, axis=-1, keepdims=True) + EPS))
    return jnp.concatenate(parts, axis=1)


def _pair_block_mask():
    r = lax.broadcasted_iota(jnp.int32, (256, 128), 0)
    c = lax.broadcasted_iota(jnp.int32, (256, 128), 1)
    return lax.shift_right_logical(r, 7) == lax.shift_right_logical(c, 6)


def _head_lane_mask(a):
    c = lax.broadcasted_iota(jnp.int32, (1, 128), 1)
    return lax.shift_right_logical(c, 6) == a


def _pair_state_scan(s_ref, kv, decay_row, nb, nchunk):
    entering = {}
    for b in range(nb):
        for p in range(2):
            st = s_ref[b, p]
            for n in range(nchunk):
                entering[b, p, n] = st.astype(BF16)
                st = decay_row(b, p, n) * st + kv[b, p, n]
            s_ref[b, p] = st
    return entering


def _hg_tile(tile, rs, x_ref, g_ref, w_ref, lb_ref, ng_ref, o_ref, u_ref, s_ref, *, npad):
    nb, T = x_ref.shape[0], ROW_TILE
    hn = _stacked_norm_in(x_ref, rs, g_ref, tile * T, npad)
    z = _mm(hn, w_ref[...])
    zq, zf, v, zg = z[:, 0:256], z[:, 256:512], z[:, 512:1024], z[:, 1024:1536]
    for b in range(nb):
        u_ref[b, rs, :] = z[b * T:(b + 1) * T, 1536:2048]

    log_lb, log1m_lb, one_m_lb = lb_ref[0:1, :], lb_ref[1:2, :], lb_ref[2:3, :]
    e = jnp.exp(-jnp.abs(zf))
    inv1pe = 1.0 / (1.0 + e)
    log_sig = jnp.minimum(zf, 0.0) - jnp.log1p(e)
    y2 = log1m_lb + log_sig
    log_f = jnp.maximum(log_lb, y2) + jnp.log1p(jnp.exp(-jnp.abs(log_lb - y2)))
    k = one_m_lb * jnp.where(zf >= 0.0, e * inv1pe, inv1pe)
    q = _silu(zq) * (HG_DK ** -0.5)

    same, low, _, _ = _seg_masks(nb * T, HG_CHUNK)
    b_cum = _mm01(low.astype(BF16), log_f)
    b_last = _mm01(same.astype(BF16), log_f)
    q_t = q * jnp.exp(b_cum)
    k_t = (k * jnp.exp(-b_cum)).astype(BF16)
    k_end = (k * jnp.exp(b_last - b_cum)).astype(BF16)
    d_end = jnp.exp(b_last)
    q_tb = q_t.astype(BF16)
    vb = v.astype(BF16)
    low_t = low[0:T, 0:T]

    chains = [(b, h) for b in range(nb) for h in range(HG_HEADS)]
    rows = lambda b: slice(b * T, (b + 1) * T)
    slab = lambda p: slice(128 * p, 128 * p + 128)

    qh = [jnp.where(_head_lane_mask(h % 2), q_t[rows(b), slab(h // 2)], 0.0).astype(BF16) for b, h in chains]
    att = [jnp.where(low_t, _mm_nt(qh[c], k_t[rows(b), slab(h // 2)]), 0.0).astype(BF16)
           for c, (b, h) in enumerate(chains)]
    o_intra = [_mm(att[c], vb[rows(b), slab(h)]) for c, (b, h) in enumerate(chains)]

    nchunk = T // HG_CHUNK
    crow = lambda b, n: slice(b * T + n * HG_CHUNK, b * T + (n + 1) * HG_CHUNK)
    blk = _pair_block_mask()
    kv = {(b, p, n): jnp.where(blk, _mm_tn(vb[crow(b, n), 256 * p:256 * p + 256], k_end[crow(b, n), slab(p)]), 0.0)
          for b in range(nb) for n in range(nchunk) for p in range(2)}
    entering = _pair_state_scan(
        s_ref, kv, lambda b, p, n: d_end[b * T + n * HG_CHUNK:b * T + n * HG_CHUNK + 1, slab(p)], nb, nchunk)
    o_inter = {(b, p, n): _mm_nt(q_tb[crow(b, n), slab(p)], entering[b, p, n])
               for b in range(nb) for n in range(nchunk) for p in range(2)}

    o_rows = []
    for b in range(nb):
        inter = jnp.concatenate([jnp.concatenate([o_inter[b, 0, n], o_inter[b, 1, n]], axis=1)
                                 for n in range(nchunk)], axis=0)
        intra = jnp.concatenate([o_intra[b * HG_HEADS + h] for h in range(HG_HEADS)], axis=1)
        o_rows.append(intra + inter)
    o = jnp.concatenate(o_rows, axis=0)
    o = _head_rms(o, HG_DV) * ng_ref[...] * _silu(zg)
    for b in range(nb):
        o_ref[b, rs, :] = o[b * T:(b + 1) * T].astype(o_ref.dtype)


def _rot_half_swap(x):
    lane = lax.broadcasted_iota(jnp.int32, x.shape, 1)
    first_half = (lane & 63) < 32
    return jnp.where(first_half, pltpu.roll(x, x.shape[1] - 32, 1), pltpu.roll(x, 32, 1))


def _ret_tile(tile, rs, x_ref, g_ref, w_ref, cos_ref, sin_ref, dec_ref, xi_ref, zeta_ref, gc_ref, ng_ref,
              o_ref, s_ref, *, npad):
    nb, T = x_ref.shape[0], ROW_TILE
    hn = _stacked_norm_in(x_ref, rs, g_ref, tile * T, npad)
    z = _mm(hn, w_ref[...])
    zq, zk, v, zg = z[:, 0:256], z[:, 256:512], z[:, 512:1024], z[:, 1024:1536]
    tile_b = lambda a: jnp.concatenate([a] * nb, axis=0)
    cos, sin = tile_b(cos_ref[rs, :]), tile_b(sin_ref[rs, :])
    q = zq * cos + _rot_half_swap(zq) * sin
    k = (zk * cos + _rot_half_swap(zk) * sin) * (RET_DK ** -0.5)
    kb, vb = k.astype(BF16), v.astype(BF16)
    q_xi = (q * tile_b(xi_ref[...])).astype(BF16)
    k_zeta = (k * tile_b(zeta_ref[...])).astype(BF16)

    chains = [(b, h) for b in range(nb) for h in range(RET_HEADS)]
    rows = lambda b: slice(b * T, (b + 1) * T)
    slab = lambda p: slice(128 * p, 128 * p + 128)

    qh = [jnp.where(_head_lane_mask(h % 2), q[rows(b), slab(h // 2)], 0.0).astype(BF16) for b, h in chains]
    att = [(_mm_nt(qh[c], kb[rows(b), slab(h // 2)]) * dec_ref[h]).astype(BF16) for c, (b, h) in enumerate(chains)]
    o_intra = [_mm(att[c], vb[rows(b), slab(h)]) for c, (b, h) in enumerate(chains)]

    nchunk = T // CHUNK
    crow = lambda b, n: slice(b * T + n * CHUNK, b * T + (n + 1) * CHUNK)
    blk = _pair_block_mask()
    kv = {(b, p, n): jnp.where(blk, _mm_tn(vb[crow(b, n), 256 * p:256 * p + 256], k_zeta[crow(b, n), slab(p)]), 0.0)
          for b in range(nb) for n in range(nchunk) for p in range(2)}
    entering = _pair_state_scan(s_ref, kv, lambda b, p, n: gc_ref[:, slab(p)], nb, nchunk)
    o_inter = {(b, p, n): _mm_nt(q_xi[crow(b, n), slab(p)], entering[b, p, n])
               for b in range(nb) for n in range(nchunk) for p in range(2)}

    o_rows = []
    for b in range(nb):
        inter = jnp.concatenate([jnp.concatenate([o_inter[b, 0, n], o_inter[b, 1, n]], axis=1)
                                 for n in range(nchunk)], axis=0)
        intra = jnp.concatenate([o_intra[b * RET_HEADS + h] for h in range(RET_HEADS)], axis=1)
        o_rows.append(intra + inter)
    o = jnp.concatenate(o_rows, axis=0)

    parts = []
    for h in range(RET_HEADS):
        oh = o[:, 128 * h:128 * h + 128]
        oc = oh - jnp.mean(oh, axis=-1, keepdims=True)
        parts.append(oc * lax.rsqrt(jnp.mean(oc * oc, axis=-1, keepdims=True) + EPS))
    o = jnp.concatenate(parts, axis=1) * ng_ref[...] * _silu(zg)
    for b in range(nb):
        o_ref[b, rs, :] = o[b * T:(b + 1) * T].astype(o_ref.dtype)


def _gdn_tile(tile, rs, x_ref, g_ref, w_ref, wba_ref, conv_ref, hp_col_ref, hp_row_ref, ng_ref,
              o_ref, s_ref, carry_ref, *, npad):
    nb, T = x_ref.shape[0], ROW_TILE
    H = GDN_HEADS
    assert T == 128
    hn = _stacked_norm_in(x_ref, rs, g_ref, tile * T, npad)
    z = _mm(hn, w_ref[...])
    zqkv, zg, zba = z[:, 0:1536], z[:, 1536:2048], z[:, 2048:2176]
    ba_row = _mm_nt(wba_ref[...], hn)

    xc = []
    for b in range(nb):
        zb = zqkv[b * T:(b + 1) * T]
        ext = jnp.concatenate([carry_ref[b], zb], axis=0)
        carry_ref[b] = zb[T - 8:T, :]
        acc = conv_ref[3:4, :] * zb
        for j in range(CONV_K - 1):
            s = CONV_K - 1 - j
            acc = acc + conv_ref[j:j + 1, :] * ext[8 - s:8 - s + T, :]
        xc.append(acc)
    qkv = _silu(jnp.concatenate(xc, axis=0))

    beta_col = _sigmoid(zba)
    g_col = hp_col_ref[0:1, :] * _softplus(zba + hp_col_ref[1:2, :])
    g_row = jnp.concatenate([hp_row_ref[0]] * nb, axis=1) * _softplus(ba_row + jnp.concatenate([hp_row_ref[1]] * nb, axis=1))
    same, low, strict, up = _seg_masks(nb * T, CHUNK)
    cum_col = _mm01(low.astype(BF16), g_col)
    last_col = _mm01(same.astype(BF16), g_col)
    cum_row = _mm01_right(g_row, up.astype(BF16))
    low_t, strict_t = low[0:T, 0:T], strict[0:T, 0:T]
    eye = (lax.broadcasted_iota(jnp.int32, (T, T), 0) == lax.broadcasted_iota(jnp.int32, (T, T), 1)).astype(F32)

    chains = [(b, h) for b in range(nb) for h in range(H)]
    CS = range(len(chains))
    rows = lambda b: slice(b * T, (b + 1) * T)
    qn, kn, knb, vh, cum_c, last_c, beta_c, gamma = [], [], [], [], [], [], [], []
    for b, h in chains:
        qh = qkv[rows(b), 128 * h:128 * h + 128]
        kh = qkv[rows(b), 512 + 128 * h:512 + 128 * h + 128]
        vh.append(qkv[rows(b), 1024 + 128 * h:1024 + 128 * h + 128])
        qn.append(qh * lax.rsqrt(jnp.sum(qh * qh, axis=-1, keepdims=True) + EPS) * (GDN_DK ** -0.5))
        kn.append(kh * lax.rsqrt(jnp.sum(kh * kh, axis=-1, keepdims=True) + EPS))
        knb.append(kn[-1].astype(BF16))
        cum_c.append(jnp.broadcast_to(cum_col[rows(b), H + h:H + h + 1], (T, 128)))
        last_c.append(jnp.broadcast_to(last_col[rows(b), H + h:H + h + 1], (T, 128)))
        beta_c.append(jnp.broadcast_to(beta_col[rows(b), h:h + 1], (T, 128)))
        gamma.append(jnp.exp(jnp.where(low_t, cum_c[-1] - cum_row[H + h:H + h + 1, rows(b)], -jnp.inf)))

    kk = [_mm_nt(knb[c], knb[c]) for c in CS]
    qk = [_mm_nt(qn[c].astype(BF16), knb[c]) for c in CS]
    a0 = [jnp.where(strict_t, kk[c] * beta_c[c] * gamma[c], 0.0) for c in CS]
    inv = [eye - a0[c] for c in CS]
    pw = a0
    for _ in range(int(math.log2(CHUNK)) - 1):
        pwb = [p.astype(BF16) for p in pw]
        pw = [_mm(pwb[c], pwb[c]) for c in CS]
        inv = [inv[c] + _mm(inv[c].astype(BF16), pw[c].astype(BF16)) for c in CS]
    e_cum = [jnp.exp(cum_c[c]) for c in CS]
    uw = [_mm(inv[c].astype(BF16),
              jnp.concatenate([vh[c] * beta_c[c], kn[c] * beta_c[c] * e_cum[c]], axis=1).astype(BF16)) for c in CS]
    u = [x[:, 0:128] for x in uw]
    wb = [x[:, 128:256].astype(BF16) for x in uw]
    att = [(qk[c] * gamma[c]).astype(BF16) for c in CS]
    qd = [(qn[c] * e_cum[c]).astype(BF16) for c in CS]
    kd = [(kn[c] * jnp.exp(last_c[c] - cum_c[c])).astype(BF16) for c in CS]
    dl = [jnp.exp(last_c[c]) for c in CS]

    st = [s_ref[b, h] for b, h in chains]
    v_rows = [[] for _ in CS]
    o_rows = [[] for _ in CS]
    for n in range(T // CHUNK):
        r = slice(n * CHUNK, (n + 1) * CHUNK)
        stb = [st[c].astype(BF16) for c in CS]
        v_new = [u[c][r] - _mm(wb[c][r], stb[c]) for c in CS]
        for c in CS:
            v_rows[c].append(v_new[c])
            rest = [u[c][(n + 1) * CHUNK:]] if (n + 1) * CHUNK < T else []
            v_all = jnp.concatenate(v_rows[c] + rest, axis=0)
            o_rows[c].append(_mm(qd[c][r], stb[c]) + _mm(att[c][r], v_all.astype(BF16)))
        st = [dl[c][n * CHUNK:n * CHUNK + 1, :] * st[c] + _mm_tn(kd[c][r], v_new[c].astype(BF16)) for c in CS]
    for c, (b, h) in enumerate(chains):
        s_ref[b, h] = st[c]
    o = jnp.concatenate([jnp.concatenate([jnp.concatenate(o_rows[b * H + h], axis=0) for h in range(H)], axis=1)
                         for b in range(nb)], axis=0)
    o = _head_rms(o, GDN_DV) * ng_ref[...] * _silu(zg)
    for b in range(nb):
        o_ref[b, rs, :] = o[b * T:(b + 1) * T].astype(o_ref.dtype)


def _shift_rows(x, s):
    n = x.shape[0]
    if s % 8 == 0:
        return jnp.concatenate([jnp.zeros((s, x.shape[1]), x.dtype), x[:n - s]], axis=0)
    rows = lax.broadcasted_iota(jnp.int32, x.shape, 0)
    return jnp.where(rows >= s, pltpu.roll(x, s, 0), 0.0)


def _s5_kernel(u_ref, kc_ref, wst_ref, wout_ref, mul_ref, d_ref, y_ref):
    C, P = SSM_CHUNK, SSM_GROUP
    rb = u_ref.shape[0]
    nsteps = mul_ref.shape[1] // 2
    ut = [u_ref[:, t, :] for t in range(C)]
    yg = []
    for gl in range(u_ref.shape[2] // P):
        ug = jnp.concatenate([ut[t][:, gl * P:(gl + 1) * P] for t in range(C)], axis=1)
        ub = ug.astype(BF16)
        kc = kc_ref[gl]
        toep = jnp.concatenate(
            [kc] + [jnp.concatenate([jnp.zeros((P, s * P), F32), kc[:, :(C - s) * P]], axis=1) for s in range(1, C)],
            axis=0).astype(BF16)
        y = _mm(ub, toep)
        x = _mm(ub, wst_ref[gl])
        for k in range(nsteps):
            s = 1 << k
            if s >= rb:
                break
            xs = _shift_rows(x, s)
            x = x + xs * mul_ref[gl, 2 * k:2 * k + 1, :] + pltpu.roll(xs, 64, 1) * mul_ref[gl, 2 * k + 1:2 * k + 2, :]
        xin = _shift_rows(x, 1)
        yg.append(y + _mm(xin.astype(BF16), wout_ref[gl]))
    for t in range(C):
        yt = jnp.concatenate([y[:, t * P:(t + 1) * P] for y in yg], axis=1)
        y_ref[:, t, :] = yt + d_ref[...] * ut[t]


def _gelu_tanh(x):
    return 0.5 * x * (1.0 + jnp.tanh(math.sqrt(2.0 / math.pi) * (x + 0.044715 * (x * x * x))))


def _merge_kernel(x_ref, g_ref, b0_ref, b1_ref, y5_ref, b3_ref, wg_ref, wglu_ref, wb_ref, wo_ref,
                  o_ref, *, npad, tiles_per_batch):
    T = x_ref.shape[0]
    x = x_ref[...]
    row0 = lax.rem(pl.program_id(0), tiles_per_batch) * T
    hn = _norm_in(x, g_ref[...], row0, npad)
    ya = _gelu_tanh(y5_ref[...])
    b2 = ya * _sigmoid(_mm(ya.astype(BF16), wglu_ref[...]))
    branches = (b0_ref[...], b1_ref[...], b2.astype(BF16), b3_ref[...])
    dm = x.shape[1]
    merged = None
    for m in range(N_BRANCH):
        gate = _sigmoid(_mm(hn, wg_ref[:, m * dm:(m + 1) * dm]))
        term = gate * _mm(branches[m], wb_ref[m])
        merged = term if merged is None else merged + term
    o_ref[...] = x + _mm(merged.astype(BF16), wo_ref[...])


def _ffn_kernel(x_ref, g_ref, wgu_ref, wd_ref, o_ref, *, n_split):
    x = x_ref[...]
    h = (x * lax.rsqrt(jnp.mean(x * x, axis=-1, keepdims=True) + EPS) * g_ref[...]).astype(BF16)
    dff = wd_ref.shape[0]
    cw = dff // n_split
    acts = []
    for c in range(n_split):
        gt = _mm(h, wgu_ref[:, c * cw:(c + 1) * cw])
        up = _mm(h, wgu_ref[:, dff + c * cw:dff + (c + 1) * cw])
        acts.append((_silu(gt) * up).astype(BF16))
    acc = x
    for c in range(n_split):
        acc = acc + _mm(acts[c], wd_ref[c * cw:(c + 1) * cw, :])
    o_ref[...] = acc


def _final_kernel(x_ref, g_ref, o_ref):
    x = x_ref[...]
    o_ref[...] = x * lax.rsqrt(jnp.mean(x * x, axis=-1, keepdims=True) + EPS) * g_ref[...]


def _const_spec(block, index):
    return pl.BlockSpec(block, index, pipeline_mode=pl.Buffered(1))


def _layer_spec(shape, layer):
    zeros = (0,) * len(shape)
    return _const_spec((None,) + tuple(shape), lambda i: (layer,) + zeros)


def _params(sem):
    return pltpu.CompilerParams(dimension_semantics=sem, vmem_limit_bytes=VMEM_LIMIT)


def _s5_constants(a_re, a_im, log_dt, b_re, b_im, c_re, c_im, nsteps):
    hp = lax.Precision.HIGHEST
    C, G, N, P = SSM_CHUNK, SSM_GROUPS, SSM_STATE, SSM_GROUP
    L = a_re.shape[0]
    dt = jnp.exp(log_dt)[..., None]
    mag = jnp.exp(dt * a_re)
    ar, ai = mag * jnp.cos(dt * a_im), mag * jnp.sin(dt * a_im)
    den = a_re * a_re + a_im * a_im
    nr, ni = ar - 1.0, ai
    cr, ci = (nr * a_re + ni * a_im) / den, (ni * a_re - nr * a_im) / den
    bbr = cr[..., None] * b_re - ci[..., None] * b_im
    bbi = cr[..., None] * b_im + ci[..., None] * b_re
    pr, pi = [jnp.ones_like(ar)], [jnp.zeros_like(ar)]
    for _ in range(C):
        pr, pi = pr + [pr[-1] * ar - pi[-1] * ai], pi + [pr[-1] * ai + pi[-1] * ar]
    pr, pi = jnp.stack(pr), jnp.stack(pi)
    car = c_re[None] * pr[:C, :, :, None, :] - c_im[None] * pi[:C, :, :, None, :]
    cai = c_re[None] * pi[:C, :, :, None, :] + c_im[None] * pr[:C, :, :, None, :]
    kern = (jnp.einsum('tlgpn,lgnq->lgtpq', car, bbr, precision=hp)
            - jnp.einsum('tlgpn,lgnq->lgtpq', cai, bbi, precision=hp))
    kc = kern.transpose(0, 1, 4, 2, 3).reshape(L, G, P, C * P)
    rev_r, rev_i = pr[C - 1::-1][:C], pi[C - 1::-1][:C]
    wst_r = rev_r[..., None] * bbr[None] - rev_i[..., None] * bbi[None]
    wst_i = rev_r[..., None] * bbi[None] + rev_i[..., None] * bbr[None]
    wst = jnp.concatenate([wst_r, wst_i], axis=3).transpose(1, 2, 0, 4, 3).reshape(L, G, C * P, 2 * N)
    co_r = c_re[None] * pr[1:, :, :, None, :] - c_im[None] * pi[1:, :, :, None, :]
    co_i = c_re[None] * pi[1:, :, :, None, :] + c_im[None] * pr[1:, :, :, None, :]
    wout = jnp.concatenate([co_r, -co_i], axis=4).transpose(1, 2, 4, 0, 3).reshape(L, G, 2 * N, C * P)
    sr, si = pr[C], pi[C]
    muls = []
    for _ in range(nsteps):
        muls += [jnp.concatenate([sr, sr], axis=-1), jnp.concatenate([-si, si], axis=-1)]
        sr, si = sr * sr - si * si, 2.0 * sr * si
    muls = jnp.stack(muls, axis=2)
    return kc.astype(F32), wst.astype(BF16), wout.astype(BF16), muls.astype(F32)


def _retention_tables(lp, npad, tile):
    half = RET_DK // 2
    pos = (jnp.arange(lp) - npad).astype(F32)
    inv = ROPE_BASE ** (-jnp.arange(half, dtype=F32) / half)
    ang = pos[:, None] * inv[None, :]
    cos, sin = jnp.cos(ang), jnp.sin(ang)
    cos_t = jnp.tile(jnp.concatenate([cos, cos], axis=1), (1, RET_HEADS))
    sin_t = jnp.tile(jnp.concatenate([-sin, sin], axis=1), (1, RET_HEADS))
    log_gamma = jnp.log1p(-jnp.exp2(-5.0 - jnp.arange(RET_HEADS, dtype=F32)))
    idx = jnp.arange(tile)
    loc = (idx % CHUNK).astype(F32)
    diff = (idx[:, None] - idx[None, :]).astype(F32)
    ok = ((idx[:, None] // CHUNK) == (idx[None, :] // CHUNK)) & (diff >= 0)
    dec = jnp.where(ok[None], jnp.exp(log_gamma[:, None, None] * jnp.maximum(diff, 0.0)[None]), 0.0)
    xi = jnp.repeat(jnp.exp(log_gamma[None, :] * (loc[:, None] + 1.0)), RET_DK, axis=1)
    zeta = jnp.repeat(jnp.exp(log_gamma[None, :] * (CHUNK - 1.0 - loc[:, None])), RET_DK, axis=1)
    gc = jnp.repeat(jnp.exp(log_gamma * CHUNK), RET_DK)[None, :]
    return cos_t, sin_t, dec, xi, zeta, gc


def kernel(x, meta, norm_mix, w_in, hg_lb_logits, hg_norm, ret_norm, ssm_a_re, ssm_a_im, ssm_log_dt,
           ssm_b_re, ssm_b_im, ssm_c_re, ssm_c_im, ssm_d, ssm_w_glu, gdn_conv, gdn_a_log, gdn_dt_bias,
           gdn_norm, w_branch, w_out, norm_ffn, w_gu, w_down, norm_final):
    bsz, seq, dm = x.shape
    depth = w_in.shape[0]
    T = ROW_TILE
    assert seq % T == 0 and dm % 128 == 0
    npad = (-(N_META + seq)) % T
    lp = npad + N_META + seq
    nt = lp // T
    rows = bsz * lp
    tm = next(t for t in DENSE_TILES if lp % t == 0)
    dff = w_down.shape[1]

    h = jnp.concatenate([jnp.zeros((bsz, npad, dm), F32),
                         jnp.broadcast_to(meta.astype(F32)[None], (bsz, N_META, dm)), x.astype(F32)], axis=1)
    h = h.reshape(rows, dm)

    wb16 = w_in.astype(BF16)
    w_hg = jnp.concatenate([wb16[:, :, 0:1536], wb16[:, :, 3072:3584]], axis=2)
    w_rt = wb16[:, :, 1536:3072]
    w_gd = jnp.concatenate([wb16[:, :, 3584:5120], wb16[:, :, 5128:5640],
                            jnp.pad(wb16[:, :, 5120:5128], ((0, 0), (0, 0), (0, 120)))], axis=2)
    w_ba = jnp.swapaxes(wb16[:, :, 5120:5128], 1, 2)
    w_gate = wb16[:, :, 5640:5640 + N_BRANCH * dm]
    w_glu16, w_br16, w_out16 = ssm_w_glu.astype(BF16), w_branch.astype(BF16), w_out.astype(BF16)
    w_gu16, w_down16 = w_gu.astype(BF16), w_down.astype(BF16)

    lb = jnp.cumsum(jax.nn.softmax(hg_lb_logits.astype(F32), axis=0), axis=0)
    lb = lb - lb[:1]
    lb_rows = jnp.stack([jnp.log(lb), jnp.log1p(-lb), 1.0 - lb] + [jnp.zeros_like(lb)] * 5, axis=1)

    neg_decay = -jnp.exp(gdn_a_log.astype(F32))
    zeros4 = jnp.zeros_like(neg_decay)
    hp_col = jnp.stack([jnp.pad(jnp.concatenate([zeros4, neg_decay], axis=1), ((0, 0), (0, 120))),
                        jnp.pad(jnp.concatenate([zeros4, gdn_dt_bias.astype(F32)], axis=1), ((0, 0), (0, 120)))]
                       + [jnp.zeros((depth, 128), F32)] * 6, axis=1)
    hp_row = jnp.stack([jnp.concatenate([zeros4, neg_decay], axis=1),
                        jnp.concatenate([zeros4, gdn_dt_bias.astype(F32)], axis=1)], axis=1)
    hp_row = jnp.broadcast_to(hp_row[..., None], (depth, 2, 8, T))

    cos_t, sin_t, dec, xi, zeta, gc = _retention_tables(lp, npad, T)

    rchunks = rows // SSM_CHUNK
    nsteps = max(1, int(math.ceil(math.log2(lp // SSM_CHUNK))))
    f32 = lambda a: a.astype(F32)
    m_k, wst, wout, muls = _s5_constants(f32(ssm_a_re), f32(ssm_a_im), f32(ssm_log_dt), f32(ssm_b_re),
                                         f32(ssm_b_im), f32(ssm_c_re), f32(ssm_c_im), nsteps)

    gmix = norm_mix.astype(F32).reshape(depth, 1, dm)
    ni = next(n for n in MIXER_INNER_TILES if nt % n == 0)
    row3 = lambda w: pl.BlockSpec((bsz, ni * T, w), lambda i: (0, i, 0))
    tile_tab = lambda w: pl.BlockSpec((ni * T, w), lambda i: (i, 0))
    fixed = lambda *shape: _const_spec(shape, lambda i: (0,) * len(shape))
    vec = lambda w, l: _layer_spec((1, w), l)
    seq_params = _params(("arbitrary",))

    for l in range(depth):
        h3 = h.reshape(bsz, lp, dm)
        br0, u = pl.pallas_call(
            functools.partial(_tiled_mixer, _hg_tile, 1, npad=npad, n_inner=ni),
            grid=(nt // ni,),
            in_specs=[row3(dm), vec(dm, l), _layer_spec((dm, 2048), l), _layer_spec((8, 256), l), vec(512, l)],
            out_specs=[row3(512), row3(512)],
            out_shape=[jax.ShapeDtypeStruct((bsz, lp, 512), BF16), jax.ShapeDtypeStruct((bsz, lp, 512), F32)],
            scratch_shapes=[pltpu.VMEM((bsz, 2, 256, 128), F32)],
            compiler_params=seq_params, name="hgrn2_mixer",
        )(h3, gmix, w_hg, lb_rows, hg_norm.astype(F32).reshape(depth, 1, 512))

        br1 = pl.pallas_call(
            functools.partial(_tiled_mixer, _ret_tile, 1, npad=npad, n_inner=ni),
            grid=(nt // ni,),
            in_specs=[row3(dm), vec(dm, l), _layer_spec((dm, 1536), l), tile_tab(256), tile_tab(256),
                      fixed(RET_HEADS, T, T), fixed(T, 256), fixed(T, 256), fixed(1, 256), vec(512, l)],
            out_specs=row3(512),
            out_shape=jax.ShapeDtypeStruct((bsz, lp, 512), BF16),
            scratch_shapes=[pltpu.VMEM((bsz, 2, 256, 128), F32)],
            compiler_params=seq_params, name="retention_mixer",
        )(h3, gmix, w_rt, cos_t, sin_t, dec, xi, zeta, gc, ret_norm.astype(F32).reshape(depth, 1, 512))

        br3 = pl.pallas_call(
            functools.partial(_tiled_mixer, _gdn_tile, 2, npad=npad, n_inner=ni),
            grid=(nt // ni,),
            in_specs=[row3(dm), vec(dm, l), _layer_spec((dm, 2176), l), _layer_spec((8, dm), l),
                      _layer_spec((CONV_K, 1536), l), _layer_spec((8, 128), l), _layer_spec((2, 8, T), l), vec(512, l)],
            out_specs=row3(512),
            out_shape=jax.ShapeDtypeStruct((bsz, lp, 512), BF16),
            scratch_shapes=[pltpu.VMEM((bsz, GDN_HEADS, GDN_DK, GDN_DV), F32), pltpu.VMEM((bsz, 8, 1536), F32)],
            compiler_params=seq_params, name="gdn_mixer",
        )(h3, gmix, w_gd, w_ba, gdn_conv.astype(F32), hp_col, hp_row, gdn_norm.astype(F32).reshape(depth, 1, 512))
        br0, br1, br3 = (a.reshape(rows, 512) for a in (br0, br1, br3))

        cp = SSM_CHUNK * SSM_GROUP
        gb = S5_GROUP_BLOCK
        cb = rchunks // bsz
        gspec = lambda a, b_: pl.BlockSpec((None, gb, a, b_), lambda b, g: (l, g, 0, 0))
        uspec = pl.BlockSpec((cb, SSM_CHUNK, gb * SSM_GROUP), lambda b, g: (b, 0, g))
        y5 = pl.pallas_call(
            _s5_kernel,
            grid=(bsz, SSM_GROUPS // gb),
            in_specs=[uspec, gspec(SSM_GROUP, cp), gspec(cp, 2 * SSM_STATE), gspec(2 * SSM_STATE, cp),
                      gspec(2 * nsteps, 2 * SSM_STATE),
                      pl.BlockSpec((None, 1, gb * SSM_GROUP), lambda b, g: (l, 0, g))],
            out_specs=uspec,
            out_shape=jax.ShapeDtypeStruct((rchunks, SSM_CHUNK, SSM_WIDTH), F32),
            compiler_params=_params(("arbitrary", "arbitrary")), name="s5_mixer",
        )(u.reshape(rchunks, SSM_CHUNK, SSM_WIDTH), m_k, wst, wout, muls,
          ssm_d.astype(F32).reshape(depth, 1, SSM_WIDTH))
        y5 = y5.reshape(rows, SSM_WIDTH)

        drow = lambda w: pl.BlockSpec((tm, w), lambda i: (i, 0))
        h = pl.pallas_call(
            functools.partial(_merge_kernel, npad=npad, tiles_per_batch=lp // tm),
            grid=(rows // tm,),
            in_specs=[drow(dm), vec(dm, l), drow(512), drow(512), drow(512), drow(512),
                      _layer_spec((dm, N_BRANCH * dm), l), _layer_spec((512, 512), l),
                      _layer_spec((N_BRANCH, BRANCH_WIDTH, dm), l), _layer_spec((dm, dm), l)],
            out_specs=drow(dm),
            out_shape=jax.ShapeDtypeStruct((rows, dm), F32),
            compiler_params=seq_params, name="merge_out",
        )(h, gmix, br0, br1, y5, br3, w_gate, w_glu16, w_br16, w_out16)

        h = pl.pallas_call(
            functools.partial(_ffn_kernel, n_split=FFN_SPLITS),
            grid=(rows // tm,),
            in_specs=[drow(dm), vec(dm, l), _layer_spec((dm, 2 * dff), l), _layer_spec((dff, dm), l)],
            out_specs=drow(dm),
            out_shape=jax.ShapeDtypeStruct((rows, dm), F32),
            compiler_params=seq_params, name="swiglu_ffn",
        )(h, norm_ffn.astype(F32).reshape(depth, 1, dm), w_gu16, w_down16)

    tf = next(t for t in FINAL_TILES if seq % t == 0)
    ns = seq // tf
    out = pl.pallas_call(
        _final_kernel,
        grid=(bsz, ns),
        in_specs=[pl.BlockSpec((pl.Element(tf), pl.Element(dm)),
                               lambda b, i: ((b * (lp // 8) + (npad + N_META) // 8 + i * (tf // 8)) * 8, 0)),
                  _const_spec((1, dm), lambda b, i: (0, 0))],
        out_specs=pl.BlockSpec((tf, dm), lambda b, i: (b * ns + i, 0)),
        out_shape=jax.ShapeDtypeStruct((bsz * seq, dm), x.dtype),
        compiler_params=_params(("arbitrary", "arbitrary")), name="final_norm",
    )(h, norm_final.astype(F32).reshape(1, dm))
    return out.reshape(bsz, seq, dm)
```

```python
import functools
import math

import jax
import jax.numpy as jnp
from jax import lax
from jax.experimental import pallas as pl
from jax.experimental.pallas import tpu as pltpu

F32 = jnp.float32
BF16 = jnp.bfloat16

N_META = 16
EPS = 1e-6
HG_HEADS, HG_DK, HG_DV, HG_CHUNK = 4, 64, 128, 16
RET_HEADS, RET_DK, RET_DV = 4, 64, 128
ROPE_BASE = 10000.0
SSM_WIDTH, SSM_GROUP, SSM_STATE = 512, 16, 64
SSM_GROUPS = SSM_WIDTH // SSM_GROUP
SSM_CHUNK = 16
S5_GROUP_BLOCK = 8
GDN_HEADS, GDN_DK, GDN_DV = 4, 128, 128
CONV_K = 4
CHUNK = 64
N_BRANCH = 4
BRANCH_WIDTH = 512

ROW_TILE = 256
SUB_BLOCK = 128
MIXER_INNER_TILES = (5, 4, 3, 2, 1)
DENSE_TILES = (640, 384, 320, 128)
FINAL_TILES = (1024, 512, 128)
FFN_SPLITS = 1
VMEM_LIMIT = 56 * 1024 * 1024

NT_DIMS = (((1,), (1,)), ((), ()))
TN_DIMS = (((0,), (0,)), ((), ()))


def _mm(a, b):
    return jnp.dot(a, b, preferred_element_type=F32)


def _mm_nt(a, b):
    return lax.dot_general(a, b, NT_DIMS, preferred_element_type=F32)


def _mm_tn(a, b):
    return lax.dot_general(a, b, TN_DIMS, preferred_element_type=F32)


def _split3(x):
    hi = x.astype(BF16)
    r = x - hi.astype(F32)
    mid = r.astype(BF16)
    lo = (r - mid.astype(F32)).astype(BF16)
    return hi, mid, lo


def _mm01(m01, x):
    hi, mid, lo = _split3(x)
    return _mm(m01, hi) + _mm(m01, mid) + _mm(m01, lo)


def _mm01_right(x, m01):
    hi, mid, lo = _split3(x)
    return _mm(hi, m01) + _mm(mid, m01) + _mm(lo, m01)


def _seg_masks(n, chunk):
    sh = int(math.log2(chunk))
    r = lax.broadcasted_iota(jnp.int32, (n, n), 0)
    c = lax.broadcasted_iota(jnp.int32, (n, n), 1)
    same = lax.shift_right_logical(r, sh) == lax.shift_right_logical(c, sh)
    return same, same & (c <= r), same & (c < r), same & (c >= r)


def _sigmoid(x):
    return jax.nn.sigmoid(x)


def _silu(x):
    return x * jax.nn.sigmoid(x)


def _softplus(x):
    return jnp.maximum(x, 0.0) + jnp.log1p(jnp.exp(-jnp.abs(x)))


def _norm_in(x, g, row0, npad):
    y = x * lax.rsqrt(jnp.mean(x * x, axis=-1, keepdims=True) + EPS) * g
    rows = row0 + lax.broadcasted_iota(jnp.int32, y.shape, 0)
    return jnp.where(rows >= npad, y, 0.0).astype(BF16)


def _stacked_norm_in(x_ref, rs, g_ref, row0, npad):
    return jnp.concatenate([_norm_in(x_ref[b, rs, :], g_ref[...], row0, npad) for b in range(x_ref.shape[0])], axis=0)


def _tiled_mixer(tile_fn, n_scratch, *refs, npad, n_inner):
    i = pl.program_id(0)

    @pl.when(i == 0)
    def _():
        for r in refs[len(refs) - n_scratch:]:
            r[...] = jnp.zeros_like(r)

    def step(j, carry):
        tile_fn(i * n_inner + j, pl.ds(pl.multiple_of(j * ROW_TILE, ROW_TILE), ROW_TILE), *refs, npad=npad)
        return carry

    lax.fori_loop(0, n_inner, step, 0)


def _head_rms(o, width):
    parts = []
    for h in range(o.shape[1] // width):
        oh = o[:, h * width:(h + 1) * width]
        parts.append(oh * lax.rsqrt(jnp.mean(oh * oh, axis=-1, keepdims=True) + EPS))
    return jnp.concatenate(parts, axis=1)


def _pair_block_mask():
    r = lax.broadcasted_iota(jnp.int32, (256, 128), 0)
    c = lax.broadcasted_iota(jnp.int32, (256, 128), 1)
    return lax.shift_right_logical(r, 7) == lax.shift_right_logical(c, 6)


def _head_lane_mask(a):
    c = lax.broadcasted_iota(jnp.int32, (1, 128), 1)
    return lax.shift_right_logical(c, 6) == a


def _pair_state_scan(s_ref, kv, decay_row, nb, nchunk):
    entering = {}
    for b in range(nb):
        for p in range(2):
            st = s_ref[b, p]
            for n in range(nchunk):
                entering[b, p, n] = st.astype(BF16)
                st = decay_row(b, p, n) * st + kv[b, p, n]
            s_ref[b, p] = st
    return entering


def _hg_tile(tile, rs, x_ref, g_ref, w_ref, lb_ref, ng_ref, o_ref, u_ref, s_ref, *, npad):
    nb, T, SB = x_ref.shape[0], ROW_TILE, SUB_BLOCK
    hn = _stacked_norm_in(x_ref, rs, g_ref, tile * T, npad)
    z = _mm(hn, w_ref[...])
    zq, zf, v, zg = z[:, 0:256], z[:, 256:512], z[:, 512:1024], z[:, 1024:1536]
    for b in range(nb):
        u_ref[b, rs, :] = z[b * T:(b + 1) * T, 1536:2048]

    log_lb, log1m_lb, one_m_lb = lb_ref[0:1, :], lb_ref[1:2, :], lb_ref[2:3, :]
    e = jnp.exp(-jnp.abs(zf))
    inv1pe = 1.0 / (1.0 + e)
    log_sig = jnp.minimum(zf, 0.0) - jnp.log1p(e)
    y2 = log1m_lb + log_sig
    log_f = jnp.maximum(log_lb, y2) + jnp.log1p(jnp.exp(-jnp.abs(log_lb - y2)))
    k = one_m_lb * jnp.where(zf >= 0.0, e * inv1pe, inv1pe)
    q = _silu(zq) * (HG_DK ** -0.5)

    same, low, _, _ = _seg_masks(T, HG_CHUNK)
    per_batch = lambda m01: jnp.concatenate([_mm01(m01, log_f[b * T:(b + 1) * T]) for b in range(nb)], axis=0)
    b_cum = per_batch(low.astype(BF16))
    b_last = per_batch(same.astype(BF16))
    q_t = q * jnp.exp(b_cum)
    k_t = (k * jnp.exp(-b_cum)).astype(BF16)
    k_end = (k * jnp.exp(b_last - b_cum)).astype(BF16)
    d_end = jnp.exp(b_last)
    q_tb = q_t.astype(BF16)
    vb = v.astype(BF16)
    low_t = low[0:SB, 0:SB]

    nblk = nb * T // SB
    chains = [(k, h) for k in range(nblk) for h in range(HG_HEADS)]
    rows = lambda k: slice(k * SB, (k + 1) * SB)
    slab = lambda p: slice(128 * p, 128 * p + 128)

    qh = [jnp.where(_head_lane_mask(h % 2), q_t[rows(k), slab(h // 2)], 0.0).astype(BF16) for k, h in chains]
    att = [jnp.where(low_t, _mm_nt(qh[c], k_t[rows(k), slab(h // 2)]), 0.0).astype(BF16)
           for c, (k, h) in enumerate(chains)]
    o_intra = [_mm(att[c], vb[rows(k), slab(h)]) for c, (k, h) in enumerate(chains)]

    nchunk = T // HG_CHUNK
    crow = lambda b, n: slice(b * T + n * HG_CHUNK, b * T + (n + 1) * HG_CHUNK)
    blk = _pair_block_mask()
    kv = {(b, p, n): jnp.where(blk, _mm_tn(vb[crow(b, n), 256 * p:256 * p + 256], k_end[crow(b, n), slab(p)]), 0.0)
          for b in range(nb) for n in range(nchunk) for p in range(2)}
    entering = _pair_state_scan(
        s_ref, kv, lambda b, p, n: d_end[b * T + n * HG_CHUNK:b * T + n * HG_CHUNK + 1, slab(p)], nb, nchunk)
    o_inter = {(b, p, n): _mm_nt(q_tb[crow(b, n), slab(p)], entering[b, p, n])
               for b in range(nb) for n in range(nchunk) for p in range(2)}

    inter = jnp.concatenate([jnp.concatenate([o_inter[b, 0, n], o_inter[b, 1, n]], axis=1)
                             for b in range(nb) for n in range(nchunk)], axis=0)
    intra = jnp.concatenate([jnp.concatenate([o_intra[k * HG_HEADS + h] for h in range(HG_HEADS)], axis=1)
                             for k in range(nblk)], axis=0)
    o = _head_rms(intra + inter, HG_DV) * ng_ref[...] * _silu(zg)
    for b in range(nb):
        o_ref[b, rs, :] = o[b * T:(b + 1) * T].astype(o_ref.dtype)


def _rot_half_swap(x):
    lane = lax.broadcasted_iota(jnp.int32, x.shape, 1)
    first_half = (lane & 63) < 32
    return jnp.where(first_half, pltpu.roll(x, x.shape[1] - 32, 1), pltpu.roll(x, 32, 1))


def _ret_tile(tile, rs, x_ref, g_ref, w_ref, cos_ref, sin_ref, dec_ref, xi_ref, zeta_ref, gc_ref, ng_ref,
              o_ref, s_ref, *, npad):
    nb, T, SB = x_ref.shape[0], ROW_TILE, SUB_BLOCK
    hn = _stacked_norm_in(x_ref, rs, g_ref, tile * T, npad)
    z = _mm(hn, w_ref[...])
    zq, zk, v, zg = z[:, 0:256], z[:, 256:512], z[:, 512:1024], z[:, 1024:1536]
    tile_rows = lambda a: jnp.concatenate([a] * (nb * T // a.shape[0]), axis=0)
    cos, sin = tile_rows(cos_ref[rs, :]), tile_rows(sin_ref[rs, :])
    q = zq * cos + _rot_half_swap(zq) * sin
    k = (zk * cos + _rot_half_swap(zk) * sin) * (RET_DK ** -0.5)
    kb, vb = k.astype(BF16), v.astype(BF16)
    q_xi = (q * tile_rows(xi_ref[...])).astype(BF16)
    k_zeta = (k * tile_rows(zeta_ref[...])).astype(BF16)

    nblk = nb * T // SB
    chains = [(kk, h) for kk in range(nblk) for h in range(RET_HEADS)]
    rows = lambda kk: slice(kk * SB, (kk + 1) * SB)
    slab = lambda p: slice(128 * p, 128 * p + 128)

    qh = [jnp.where(_head_lane_mask(h % 2), q[rows(kk), slab(h // 2)], 0.0).astype(BF16) for kk, h in chains]
    att = [(_mm_nt(qh[c], kb[rows(kk), slab(h // 2)]) * dec_ref[h]).astype(BF16) for c, (kk, h) in enumerate(chains)]
    o_intra = [_mm(att[c], vb[rows(kk), slab(h)]) for c, (kk, h) in enumerate(chains)]

    nchunk = T // CHUNK
    crow = lambda b, n: slice(b * T + n * CHUNK, b * T + (n + 1) * CHUNK)
    blk = _pair_block_mask()
    kv = {(b, p, n): jnp.where(blk, _mm_tn(vb[crow(b, n), 256 * p:256 * p + 256], k_zeta[crow(b, n), slab(p)]), 0.0)
          for b in range(nb) for n in range(nchunk) for p in range(2)}
    entering = _pair_state_scan(s_ref, kv, lambda b, p, n: gc_ref[:, slab(p)], nb, nchunk)
    o_inter = {(b, p, n): _mm_nt(q_xi[crow(b, n), slab(p)], entering[b, p, n])
               for b in range(nb) for n in range(nchunk) for p in range(2)}

    inter = jnp.concatenate([jnp.concatenate([o_inter[b, 0, n], o_inter[b, 1, n]], axis=1)
                             for b in range(nb) for n in range(nchunk)], axis=0)
    intra = jnp.concatenate([jnp.concatenate([o_intra[kk * RET_HEADS + h] for h in range(RET_HEADS)], axis=1)
                             for kk in range(nblk)], axis=0)
    o = intra + inter

    parts = []
    for h in range(RET_HEADS):
        oh = o[:, 128 * h:128 * h + 128]
        oc = oh - jnp.mean(oh, axis=-1, keepdims=True)
        parts.append(oc * lax.rsqrt(jnp.mean(oc * oc, axis=-1, keepdims=True) + EPS))
    o = jnp.concatenate(parts, axis=1) * ng_ref[...] * _silu(zg)
    for b in range(nb):
        o_ref[b, rs, :] = o[b * T:(b + 1) * T].astype(o_ref.dtype)


def _gdn_tile(tile, rs, x_ref, g_ref, w_ref, wba_ref, conv_ref, hp_col_ref, hp_row_ref, ng_ref,
              o_ref, s_ref, carry_ref, *, npad):
    nb, T, SB = x_ref.shape[0], ROW_TILE, SUB_BLOCK
    H = GDN_HEADS
    assert SB == 128
    hn = _stacked_norm_in(x_ref, rs, g_ref, tile * T, npad)
    z = _mm(hn, w_ref[...])
    zqkv, zg, zba = z[:, 0:1536], z[:, 1536:2048], z[:, 2048:2176]
    ba_row = _mm_nt(wba_ref[...], hn)

    xc = []
    for b in range(nb):
        zb = zqkv[b * T:(b + 1) * T]
        ext = jnp.concatenate([carry_ref[b], zb], axis=0)
        carry_ref[b] = zb[T - 8:T, :]
        acc = conv_ref[3:4, :] * zb
        for j in range(CONV_K - 1):
            s = CONV_K - 1 - j
            acc = acc + conv_ref[j:j + 1, :] * ext[8 - s:8 - s + T, :]
        xc.append(acc)
    qkv = _silu(jnp.concatenate(xc, axis=0))

    beta_col = _sigmoid(zba)
    g_col = hp_col_ref[0:1, :] * _softplus(zba + hp_col_ref[1:2, :])
    rep = nb * T // hp_row_ref.shape[2]
    g_row = (jnp.concatenate([hp_row_ref[0]] * rep, axis=1)
             * _softplus(ba_row + jnp.concatenate([hp_row_ref[1]] * rep, axis=1)))
    same, low, strict, up = _seg_masks(T, CHUNK)
    low_b, same_b, up_b = low.astype(BF16), same.astype(BF16), up.astype(BF16)
    bat = lambda b: slice(b * T, (b + 1) * T)
    cum_col = jnp.concatenate([_mm01(low_b, g_col[bat(b)]) for b in range(nb)], axis=0)
    last_col = jnp.concatenate([_mm01(same_b, g_col[bat(b)]) for b in range(nb)], axis=0)
    cum_row = jnp.concatenate([_mm01_right(g_row[:, bat(b)], up_b) for b in range(nb)], axis=1)
    low_t, strict_t = low[0:SB, 0:SB], strict[0:SB, 0:SB]
    eye = (lax.broadcasted_iota(jnp.int32, (SB, SB), 0) == lax.broadcasted_iota(jnp.int32, (SB, SB), 1)).astype(F32)

    nsb = T // SB
    chains = [(k, h) for k in range(nb * nsb) for h in range(H)]
    CS = range(len(chains))
    rows = lambda k: slice(k * SB, (k + 1) * SB)
    qn, kn, knb, vh, cum_c, last_c, beta_c, gamma = [], [], [], [], [], [], [], []
    for k, h in chains:
        qh = qkv[rows(k), 128 * h:128 * h + 128]
        kh = qkv[rows(k), 512 + 128 * h:512 + 128 * h + 128]
        vh.append(qkv[rows(k), 1024 + 128 * h:1024 + 128 * h + 128])
        qn.append(qh * lax.rsqrt(jnp.sum(qh * qh, axis=-1, keepdims=True) + EPS) * (GDN_DK ** -0.5))
        kn.append(kh * lax.rsqrt(jnp.sum(kh * kh, axis=-1, keepdims=True) + EPS))
        knb.append(kn[-1].astype(BF16))
        cum_c.append(jnp.broadcast_to(cum_col[rows(k), H + h:H + h + 1], (SB, 128)))
        last_c.append(jnp.broadcast_to(last_col[rows(k), H + h:H + h + 1], (SB, 128)))
        beta_c.append(jnp.broadcast_to(beta_col[rows(k), h:h + 1], (SB, 128)))
        gamma.append(jnp.exp(jnp.where(low_t, cum_c[-1] - cum_row[H + h:H + h + 1, rows(k)], -jnp.inf)))

    kk = [_mm_nt(knb[c], knb[c]) for c in CS]
    qk = [_mm_nt(qn[c].astype(BF16), knb[c]) for c in CS]
    a0 = [jnp.where(strict_t, kk[c] * beta_c[c] * gamma[c], 0.0) for c in CS]
    inv = [eye - a0[c] for c in CS]
    a0b = [a.astype(BF16) for a in a0]
    pw = [_mm(a0b[c], a0b[c]) for c in CS]
    nlev = int(math.log2(CHUNK)) - 1
    for lev in range(nlev):
        pwb = [p.astype(BF16) for p in pw]
        if lev + 1 < nlev:
            both = [_mm(pwb[c], jnp.concatenate([inv[c], pw[c]], axis=1).astype(BF16)) for c in CS]
            inv = [inv[c] + both[c][:, 0:SB] for c in CS]
            pw = [both[c][:, SB:2 * SB] for c in CS]
        else:
            inv = [inv[c] + _mm(pwb[c], inv[c].astype(BF16)) for c in CS]
    e_cum = [jnp.exp(cum_c[c]) for c in CS]
    uw = [_mm(inv[c].astype(BF16),
              jnp.concatenate([vh[c] * beta_c[c], kn[c] * beta_c[c] * e_cum[c]], axis=1).astype(BF16)) for c in CS]
    u = [x[:, 0:128] for x in uw]
    wb = [x[:, 128:256].astype(BF16) for x in uw]
    att = [(qk[c] * gamma[c]).astype(BF16) for c in CS]
    qd = [(qn[c] * e_cum[c]).astype(BF16) for c in CS]
    kd = [(kn[c] * jnp.exp(last_c[c] - cum_c[c])).astype(BF16) for c in CS]
    dl = [jnp.exp(last_c[c]) for c in CS]

    seqs = [(b, h) for b in range(nb) for h in range(H)]
    st = [s_ref[b, h] for b, h in seqs]
    o_rows = [[] for _ in CS]
    for s in range(nsb):
        cs = [(b * nsb + s) * H + h for b, h in seqs]
        v_rows = [[] for _ in seqs]
        for n in range(SB // CHUNK):
            r = slice(n * CHUNK, (n + 1) * CHUNK)
            stb = [x.astype(BF16) for x in st]
            v_new = [u[c][r] - _mm(wb[c][r], stb[i]) for i, c in enumerate(cs)]
            for i, c in enumerate(cs):
                v_rows[i].append(v_new[i])
                rest = [u[c][(n + 1) * CHUNK:]] if (n + 1) * CHUNK < SB else []
                v_all = jnp.concatenate(v_rows[i] + rest, axis=0)
                o_rows[c].append(_mm(qd[c][r], stb[i]) + _mm(att[c][r], v_all.astype(BF16)))
            st = [dl[c][n * CHUNK:n * CHUNK + 1, :] * st[i] + _mm_tn(kd[c][r], v_new[i].astype(BF16))
                  for i, c in enumerate(cs)]
    for i, (b, h) in enumerate(seqs):
        s_ref[b, h] = st[i]
    o = jnp.concatenate([jnp.concatenate([jnp.concatenate(o_rows[k * H + h], axis=0) for h in range(H)], axis=1)
                         for k in range(nb * nsb)], axis=0)
    o = _head_rms(o, GDN_DV) * ng_ref[...] * _silu(zg)
    for b in range(nb):
        o_ref[b, rs, :] = o[b * T:(b + 1) * T].astype(o_ref.dtype)


def _shift_rows(x, s):
    n = x.shape[0]
    if s % 8 == 0:
        return jnp.concatenate([jnp.zeros((s, x.shape[1]), x.dtype), x[:n - s]], axis=0)
    rows = lax.broadcasted_iota(jnp.int32, x.shape, 0)
    return jnp.where(rows >= s, pltpu.roll(x, s, 0), 0.0)


def _s5_kernel(u_ref, kc_ref, tab_ref, mul_ref, d_ref, y_ref):
    C, P = SSM_CHUNK, SSM_GROUP
    rb = u_ref.shape[0]
    nsteps = mul_ref.shape[1] // 2
    ut = [u_ref[:, t, :] for t in range(C)]
    yg = []
    for gl in range(u_ref.shape[2] // P):
        ug = jnp.concatenate([ut[t][:, gl * P:(gl + 1) * P] for t in range(C)], axis=1)
        ub = ug.astype(BF16)
        kc = kc_ref[gl]
        toep = jnp.concatenate(
            [kc] + [jnp.concatenate([jnp.zeros((P, s * P), F32), kc[:, :(C - s) * P]], axis=1) for s in range(1, C)],
            axis=0).astype(BF16)
        y = _mm(ub, toep)
        tab = tab_ref[gl]
        ba, bb, ca, cb = tab[0:16], tab[16:32], tab[32:48], tab[48:64]
        p_re = lambda k: tab[64 + k:65 + k]
        p_im = lambda k: tab[88 + k:89 + k]
        w_state = jnp.concatenate([ba * p_re(C - 1 - s) + bb * p_im(C - 1 - s) for s in range(C)], axis=0).astype(BF16)
        w_out_t = jnp.concatenate([ca * p_re(t + 1) + cb * p_im(t + 1) for t in range(C)], axis=0).astype(BF16)
        x = _mm(ub, w_state)
        for k in range(nsteps):
            s = 1 << k
            if s >= rb:
                break
            xs = _shift_rows(x, s)
            x = x + xs * mul_ref[gl, 2 * k:2 * k + 1, :] + pltpu.roll(xs, 64, 1) * mul_ref[gl, 2 * k + 1:2 * k + 2, :]
        xin = _shift_rows(x, 1)
        yg.append(y + _mm_nt(xin.astype(BF16), w_out_t))
    for t in range(C):
        yt = jnp.concatenate([y[:, t * P:(t + 1) * P] for y in yg], axis=1)
        y_ref[:, t, :] = yt + d_ref[...] * ut[t]


def _gelu_tanh(x):
    return 0.5 * x * (1.0 + jnp.tanh(math.sqrt(2.0 / math.pi) * (x + 0.044715 * (x * x * x))))


def _merge_kernel(x_ref, g_ref, b0_ref, b1_ref, y5_ref, b3_ref, wg_ref, wglu_ref, wb_ref, wo_ref,
                  o_ref, *, npad, tiles_per_batch):
    T = x_ref.shape[0]
    x = x_ref[...]
    row0 = lax.rem(pl.program_id(0), tiles_per_batch) * T
    hn = _norm_in(x, g_ref[...], row0, npad)
    ya = _gelu_tanh(y5_ref[...])
    b2 = ya * _sigmoid(_mm(ya.astype(BF16), wglu_ref[...]))
    branches = (b0_ref[...], b1_ref[...], b2.astype(BF16), b3_ref[...])
    dm = x.shape[1]
    merged = None
    for m in range(N_BRANCH):
        gate = _sigmoid(_mm(hn, wg_ref[:, m * dm:(m + 1) * dm]))
        term = gate * _mm(branches[m], wb_ref[m])
        merged = term if merged is None else merged + term
    o_ref[...] = x + _mm(merged.astype(BF16), wo_ref[...])


def _ffn_kernel(x_ref, g_ref, wgu_ref, wd_ref, o_ref, *, n_split):
    x = x_ref[...]
    h = (x * lax.rsqrt(jnp.mean(x * x, axis=-1, keepdims=True) + EPS) * g_ref[...]).astype(BF16)
    dff = wd_ref.shape[0]
    cw = dff // n_split
    acts = []
    for c in range(n_split):
        gt = _mm(h, wgu_ref[:, c * cw:(c + 1) * cw])
        up = _mm(h, wgu_ref[:, dff + c * cw:dff + (c + 1) * cw])
        acts.append((_silu(gt) * up).astype(BF16))
    acc = x
    for c in range(n_split):
        acc = acc + _mm(acts[c], wd_ref[c * cw:(c + 1) * cw, :])
    o_ref[...] = acc


def _final_kernel(x_ref, g_ref, o_ref):
    x = x_ref[...]
    o_ref[...] = x * lax.rsqrt(jnp.mean(x * x, axis=-1, keepdims=True) + EPS) * g_ref[...]


def _const_spec(block, index):
    return pl.BlockSpec(block, index, pipeline_mode=pl.Buffered(1))


def _layer_spec(shape, layer):
    zeros = (0,) * len(shape)
    return _const_spec((None,) + tuple(shape), lambda i: (layer,) + zeros)


def _params(sem):
    return pltpu.CompilerParams(dimension_semantics=sem, vmem_limit_bytes=VMEM_LIMIT)


def _s5_constants(a_re, a_im, log_dt, b_re, b_im, c_re, c_im, nsteps):
    hp = lax.Precision.HIGHEST
    C, G, N, P = SSM_CHUNK, SSM_GROUPS, SSM_STATE, SSM_GROUP
    L = a_re.shape[0]
    dt = jnp.exp(log_dt)[..., None]
    mag = jnp.exp(dt * a_re)
    ar, ai = mag * jnp.cos(dt * a_im), mag * jnp.sin(dt * a_im)
    den = a_re * a_re + a_im * a_im
    nr, ni = ar - 1.0, ai
    cr, ci = (nr * a_re + ni * a_im) / den, (ni * a_re - nr * a_im) / den
    bbr = cr[..., None] * b_re - ci[..., None] * b_im
    bbi = cr[..., None] * b_im + ci[..., None] * b_re
    pr, pi = [jnp.ones_like(ar)], [jnp.zeros_like(ar)]
    for _ in range(C):
        pr, pi = pr + [pr[-1] * ar - pi[-1] * ai], pi + [pr[-1] * ai + pi[-1] * ar]
    pr, pi = jnp.stack(pr), jnp.stack(pi)
    car = c_re[None] * pr[:C, :, :, None, :] - c_im[None] * pi[:C, :, :, None, :]
    cai = c_re[None] * pi[:C, :, :, None, :] + c_im[None] * pr[:C, :, :, None, :]
    kern = (jnp.einsum('tlgpn,lgnq->lgtpq', car, bbr, precision=hp)
            - jnp.einsum('tlgpn,lgnq->lgtpq', cai, bbi, precision=hp))
    kc = kern.transpose(0, 1, 4, 2, 3).reshape(L, G, P, C * P)
    bbr_t, bbi_t = jnp.swapaxes(bbr, 2, 3), jnp.swapaxes(bbi, 2, 3)
    cat = lambda a, b: jnp.concatenate([a, b], axis=-1)
    pw_r = jnp.moveaxis(cat(pr, pr), 0, 2)
    pw_i = jnp.moveaxis(cat(pi, pi), 0, 2)
    gap = jnp.zeros((L, G, 24 - (C + 1), 2 * N), F32)
    tab = jnp.concatenate([cat(bbr_t, bbi_t), cat(-bbi_t, bbr_t), cat(c_re, -c_im), cat(-c_im, -c_re),
                           pw_r, gap, pw_i, gap], axis=2)
    sr, si = pr[C], pi[C]
    muls = []
    for _ in range(nsteps):
        muls += [jnp.concatenate([sr, sr], axis=-1), jnp.concatenate([-si, si], axis=-1)]
        sr, si = sr * sr - si * si, 2.0 * sr * si
    muls = jnp.stack(muls, axis=2)
    return kc.astype(F32), tab.astype(F32), muls.astype(F32)


def _retention_tables(lp, npad, tile):
    half = RET_DK // 2
    pos = (jnp.arange(lp) - npad).astype(F32)
    inv = ROPE_BASE ** (-jnp.arange(half, dtype=F32) / half)
    ang = pos[:, None] * inv[None, :]
    cos, sin = jnp.cos(ang), jnp.sin(ang)
    cos_t = jnp.tile(jnp.concatenate([cos, cos], axis=1), (1, RET_HEADS))
    sin_t = jnp.tile(jnp.concatenate([-sin, sin], axis=1), (1, RET_HEADS))
    log_gamma = jnp.log1p(-jnp.exp2(-5.0 - jnp.arange(RET_HEADS, dtype=F32)))
    idx = jnp.arange(tile)
    loc = (idx % CHUNK).astype(F32)
    diff = (idx[:, None] - idx[None, :]).astype(F32)
    ok = ((idx[:, None] // CHUNK) == (idx[None, :] // CHUNK)) & (diff >= 0)
    dec = jnp.where(ok[None], jnp.exp(log_gamma[:, None, None] * jnp.maximum(diff, 0.0)[None]), 0.0)
    xi = jnp.repeat(jnp.exp(log_gamma[None, :] * (loc[:, None] + 1.0)), RET_DK, axis=1)
    zeta = jnp.repeat(jnp.exp(log_gamma[None, :] * (CHUNK - 1.0 - loc[:, None])), RET_DK, axis=1)
    gc = jnp.repeat(jnp.exp(log_gamma * CHUNK), RET_DK)[None, :]
    return cos_t, sin_t, dec, xi, zeta, gc


def kernel(x, meta, norm_mix, w_in, hg_lb_logits, hg_norm, ret_norm, ssm_a_re, ssm_a_im, ssm_log_dt,
           ssm_b_re, ssm_b_im, ssm_c_re, ssm_c_im, ssm_d, ssm_w_glu, gdn_conv, gdn_a_log, gdn_dt_bias,
           gdn_norm, w_branch, w_out, norm_ffn, w_gu, w_down, norm_final):
    bsz, seq, dm = x.shape
    depth = w_in.shape[0]
    T = ROW_TILE
    assert seq % T == 0 and dm % 128 == 0
    npad = (-(N_META + seq)) % T
    lp = npad + N_META + seq
    nt = lp // T
    rows = bsz * lp
    tm = next(t for t in DENSE_TILES if lp % t == 0)
    dff = w_down.shape[1]

    h = jnp.concatenate([jnp.zeros((bsz, npad, dm), F32),
                         jnp.broadcast_to(meta.astype(F32)[None], (bsz, N_META, dm)), x.astype(F32)], axis=1)
    h = h.reshape(rows, dm)

    wb16 = w_in.astype(BF16)
    w_hg = jnp.concatenate([wb16[:, :, 0:1536], wb16[:, :, 3072:3584]], axis=2)
    w_rt = wb16[:, :, 1536:3072]
    w_gd = jnp.concatenate([wb16[:, :, 3584:5120], wb16[:, :, 5128:5640],
                            jnp.pad(wb16[:, :, 5120:5128], ((0, 0), (0, 0), (0, 120)))], axis=2)
    w_ba = jnp.swapaxes(wb16[:, :, 5120:5128], 1, 2)
    w_gate = wb16[:, :, 5640:5640 + N_BRANCH * dm]
    w_glu16, w_br16, w_out16 = ssm_w_glu.astype(BF16), w_branch.astype(BF16), w_out.astype(BF16)
    w_gu16, w_down16 = w_gu.astype(BF16), w_down.astype(BF16)

    lb = jnp.cumsum(jax.nn.softmax(hg_lb_logits.astype(F32), axis=0), axis=0)
    lb = lb - lb[:1]
    lb_rows = jnp.stack([jnp.log(lb), jnp.log1p(-lb), 1.0 - lb] + [jnp.zeros_like(lb)] * 5, axis=1)

    neg_decay = -jnp.exp(gdn_a_log.astype(F32))
    zeros4 = jnp.zeros_like(neg_decay)
    hp_col = jnp.stack([jnp.pad(jnp.concatenate([zeros4, neg_decay], axis=1), ((0, 0), (0, 120))),
                        jnp.pad(jnp.concatenate([zeros4, gdn_dt_bias.astype(F32)], axis=1), ((0, 0), (0, 120)))]
                       + [jnp.zeros((depth, 128), F32)] * 6, axis=1)
    hp_row = jnp.stack([jnp.concatenate([zeros4, neg_decay], axis=1),
                        jnp.concatenate([zeros4, gdn_dt_bias.astype(F32)], axis=1)], axis=1)
    hp_row = jnp.broadcast_to(hp_row[..., None], (depth, 2, 8, T))

    cos_t, sin_t, dec, xi, zeta, gc = _retention_tables(lp, npad, SUB_BLOCK)

    rchunks = rows // SSM_CHUNK
    nsteps = max(1, int(math.ceil(math.log2(lp // SSM_CHUNK))))
    f32 = lambda a: a.astype(F32)
    s5_kc, s5_tab, s5_muls = _s5_constants(f32(ssm_a_re), f32(ssm_a_im), f32(ssm_log_dt), f32(ssm_b_re),
                                           f32(ssm_b_im), f32(ssm_c_re), f32(ssm_c_im), nsteps)

    gmix = norm_mix.astype(F32).reshape(depth, 1, dm)
    ni = next(n for n in MIXER_INNER_TILES if nt % n == 0)
    row3 = lambda w: pl.BlockSpec((bsz, ni * T, w), lambda i: (0, i, 0))
    tile_tab = lambda w: pl.BlockSpec((ni * T, w), lambda i: (i, 0))
    fixed = lambda *shape: _const_spec(shape, lambda i: (0,) * len(shape))
    vec = lambda w, l: _layer_spec((1, w), l)
    seq_params = _params(("arbitrary",))

    for l in range(depth):
        h3 = h.reshape(bsz, lp, dm)
        br0, u = pl.pallas_call(
            functools.partial(_tiled_mixer, _hg_tile, 1, npad=npad, n_inner=ni),
            grid=(nt // ni,),
            in_specs=[row3(dm), vec(dm, l), _layer_spec((dm, 2048), l), _layer_spec((8, 256), l), vec(512, l)],
            out_specs=[row3(512), row3(512)],
            out_shape=[jax.ShapeDtypeStruct((bsz, lp, 512), BF16), jax.ShapeDtypeStruct((bsz, lp, 512), F32)],
            scratch_shapes=[pltpu.VMEM((bsz, 2, 256, 128), F32)],
            compiler_params=seq_params, name="hgrn2_mixer",
        )(h3, gmix, w_hg, lb_rows, hg_norm.astype(F32).reshape(depth, 1, 512))

        br1 = pl.pallas_call(
            functools.partial(_tiled_mixer, _ret_tile, 1, npad=npad, n_inner=ni),
            grid=(nt // ni,),
            in_specs=[row3(dm), vec(dm, l), _layer_spec((dm, 1536), l), tile_tab(256), tile_tab(256),
                      fixed(RET_HEADS, SUB_BLOCK, SUB_BLOCK), fixed(SUB_BLOCK, 256), fixed(SUB_BLOCK, 256),
                      fixed(1, 256), vec(512, l)],
            out_specs=row3(512),
            out_shape=jax.ShapeDtypeStruct((bsz, lp, 512), BF16),
            scratch_shapes=[pltpu.VMEM((bsz, 2, 256, 128), F32)],
            compiler_params=seq_params, name="retention_mixer",
        )(h3, gmix, w_rt, cos_t, sin_t, dec, xi, zeta, gc, ret_norm.astype(F32).reshape(depth, 1, 512))

        br3 = pl.pallas_call(
            functools.partial(_tiled_mixer, _gdn_tile, 2, npad=npad, n_inner=ni),
            grid=(nt // ni,),
            in_specs=[row3(dm), vec(dm, l), _layer_spec((dm, 2176), l), _layer_spec((8, dm), l),
                      _layer_spec((CONV_K, 1536), l), _layer_spec((8, 128), l), _layer_spec((2, 8, T), l), vec(512, l)],
            out_specs=row3(512),
            out_shape=jax.ShapeDtypeStruct((bsz, lp, 512), BF16),
            scratch_shapes=[pltpu.VMEM((bsz, GDN_HEADS, GDN_DK, GDN_DV), F32), pltpu.VMEM((bsz, 8, 1536), F32)],
            compiler_params=seq_params, name="gdn_mixer",
        )(h3, gmix, w_gd, w_ba, gdn_conv.astype(F32), hp_col, hp_row, gdn_norm.astype(F32).reshape(depth, 1, 512))
        br0, br1, br3 = (a.reshape(rows, 512) for a in (br0, br1, br3))

        cp = SSM_CHUNK * SSM_GROUP
        gb = S5_GROUP_BLOCK
        cb = rchunks // bsz
        gspec = lambda a, b_: pl.BlockSpec((None, gb, a, b_), lambda b, g: (l, g, 0, 0))
        uspec = pl.BlockSpec((cb, SSM_CHUNK, gb * SSM_GROUP), lambda b, g: (b, 0, g))
        y5 = pl.pallas_call(
            _s5_kernel,
            grid=(bsz, SSM_GROUPS // gb),
            in_specs=[uspec, gspec(SSM_GROUP, cp), gspec(s5_tab.shape[2], 2 * SSM_STATE),
                      gspec(2 * nsteps, 2 * SSM_STATE),
                      pl.BlockSpec((None, 1, gb * SSM_GROUP), lambda b, g: (l, 0, g))],
            out_specs=uspec,
            out_shape=jax.ShapeDtypeStruct((rchunks, SSM_CHUNK, SSM_WIDTH), F32),
            compiler_params=_params(("arbitrary", "arbitrary")), name="s5_mixer",
        )(u.reshape(rchunks, SSM_CHUNK, SSM_WIDTH), s5_kc, s5_tab, s5_muls,
          ssm_d.astype(F32).reshape(depth, 1, SSM_WIDTH))
        y5 = y5.reshape(rows, SSM_WIDTH)

        drow = lambda w: pl.BlockSpec((tm, w), lambda i: (i, 0))
        h = pl.pallas_call(
            functools.partial(_merge_kernel, npad=npad, tiles_per_batch=lp // tm),
            grid=(rows // tm,),
            in_specs=[drow(dm), vec(dm, l), drow(512), drow(512), drow(512), drow(512),
                      _layer_spec((dm, N_BRANCH * dm), l), _layer_spec((512, 512), l),
                      _layer_spec((N_BRANCH, BRANCH_WIDTH, dm), l), _layer_spec((dm, dm), l)],
            out_specs=drow(dm),
            out_shape=jax.ShapeDtypeStruct((rows, dm), F32),
            compiler_params=seq_params, name="merge_out",
        )(h, gmix, br0, br1, y5, br3, w_gate, w_glu16, w_br16, w_out16)

        h = pl.pallas_call(
            functools.partial(_ffn_kernel, n_split=FFN_SPLITS),
            grid=(rows // tm,),
            in_specs=[drow(dm), vec(dm, l), _layer_spec((dm, 2 * dff), l), _layer_spec((dff, dm), l)],
            out_specs=drow(dm),
            out_shape=jax.ShapeDtypeStruct((rows, dm), F32),
            compiler_params=seq_params, name="swiglu_ffn",
        )(h, norm_ffn.astype(F32).reshape(depth, 1, dm), w_gu16, w_down16)

    tf = next(t for t in FINAL_TILES if seq % t == 0)
    ns = seq // tf
    out = pl.pallas_call(
        _final_kernel,
        grid=(bsz, ns),
        in_specs=[pl.BlockSpec((pl.Element(tf), pl.Element(dm)),
                               lambda b, i: ((b * (lp // 8) + (npad + N_META) // 8 + i * (tf // 8)) * 8, 0)),
                  _const_spec((1, dm), lambda b, i: (0, 0))],
        out_specs=pl.BlockSpec((tf, dm), lambda b, i: (b * ns + i, 0)),
        out_shape=jax.ShapeDtypeStruct((bsz * seq, dm), x.dtype),
        compiler_params=_params(("arbitrary", "arbitrary")), name="final_norm",
    )(h, norm_final.astype(F32).reshape(1, dm))
    return out.reshape(bsz, seq, dm)
```

```python
import functools
import math

import jax
import jax.numpy as jnp
from jax import lax
from jax.experimental import pallas as pl
from jax.experimental.pallas import tpu as pltpu

F32 = jnp.float32
BF16 = jnp.bfloat16

N_META = 16
EPS = 1e-6
HG_HEADS, HG_DK, HG_DV, HG_CHUNK = 4, 64, 128, 16
RET_HEADS, RET_DK, RET_DV = 4, 64, 128
ROPE_BASE = 10000.0
SSM_WIDTH, SSM_GROUP, SSM_STATE = 512, 16, 64
SSM_GROUPS = SSM_WIDTH // SSM_GROUP
SSM_CHUNK = 16
S5_GROUP_BLOCK = 8
GDN_HEADS, GDN_DK, GDN_DV = 4, 128, 128
CONV_K = 4
CHUNK = 64
N_BRANCH = 4
BRANCH_WIDTH = 512

ROW_TILE = 256
SUB_BLOCK = 128
MIXER_INNER_TILES = (5, 4, 3, 2, 1)
DENSE_TILES = (640, 384, 320, 128)
FINAL_TILES = (1024, 512, 128)
FFN_SPLITS = 1
VMEM_LIMIT = 56 * 1024 * 1024

NT_DIMS = (((1,), (1,)), ((), ()))
TN_DIMS = (((0,), (0,)), ((), ()))


def _mm(a, b):
    return jnp.dot(a, b, preferred_element_type=F32)


def _mm_nt(a, b):
    return lax.dot_general(a, b, NT_DIMS, preferred_element_type=F32)


def _mm_tn(a, b):
    return lax.dot_general(a, b, TN_DIMS, preferred_element_type=F32)


def _split3(x):
    hi = x.astype(BF16)
    r = x - hi.astype(F32)
    mid = r.astype(BF16)
    lo = (r - mid.astype(F32)).astype(BF16)
    return hi, mid, lo


def _mm01(m01, x):
    hi, mid, lo = _split3(x)
    return _mm(m01, hi) + _mm(m01, mid) + _mm(m01, lo)


def _mm01_right(x, m01):
    hi, mid, lo = _split3(x)
    return _mm(hi, m01) + _mm(mid, m01) + _mm(lo, m01)


def _seg_masks(n, chunk):
    sh = int(math.log2(chunk))
    r = lax.broadcasted_iota(jnp.int32, (n, n), 0)
    c = lax.broadcasted_iota(jnp.int32, (n, n), 1)
    same = lax.shift_right_logical(r, sh) == lax.shift_right_logical(c, sh)
    return same, same & (c <= r), same & (c < r), same & (c >= r)


def _sigmoid(x):
    return jax.nn.sigmoid(x)


def _silu(x):
    return x * jax.nn.sigmoid(x)


def _softplus(x):
    return jnp.maximum(x, 0.0) + jnp.log1p(jnp.exp(-jnp.abs(x)))


def _norm_in(x, g, row0, npad):
    y = x * lax.rsqrt(jnp.mean(x * x, axis=-1, keepdims=True) + EPS) * g
    rows = row0 + lax.broadcasted_iota(jnp.int32, y.shape, 0)
    return jnp.where(rows >= npad, y, 0.0).astype(BF16)


def _stacked_norm_in(x_ref, rs, g_ref, row0, npad):
    return jnp.concatenate([_norm_in(x_ref[b, rs, :], g_ref[...], row0, npad) for b in range(x_ref.shape[0])], axis=0)


def _tiled_mixer(tile_fn, n_scratch, *refs, npad, n_inner):
    i = pl.program_id(0)

    @pl.when(i == 0)
    def _():
        for r in refs[len(refs) - n_scratch:]:
            r[...] = jnp.zeros_like(r)

    def step(j, carry):
        tile_fn(i * n_inner + j, pl.ds(pl.multiple_of(j * ROW_TILE, ROW_TILE), ROW_TILE), *refs, npad=npad)
        return carry

    lax.fori_loop(0, n_inner, step, 0)


def _head_rms(o, width):
    parts = []
    for h in range(o.shape[1] // width):
        oh = o[:, h * width:(h + 1) * width]
        parts.append(oh * lax.rsqrt(jnp.mean(oh * oh, axis=-1, keepdims=True) + EPS))
    return jnp.concatenate(parts, axis=1)


def _pair_block_mask():
    r = lax.broadcasted_iota(jnp.int32, (256, 128), 0)
    c = lax.broadcasted_iota(jnp.int32, (256, 128), 1)
    return lax.shift_right_logical(r, 7) == lax.shift_right_logical(c, 6)


def _head_lane_mask(a):
    c = lax.broadcasted_iota(jnp.int32, (1, 128), 1)
    return lax.shift_right_logical(c, 6) == a


def _pair_state_scan(s_ref, kv, decay_row, nb, nchunk):
    entering = {}
    for b in range(nb):
        for p in range(2):
            st = s_ref[b, p]
            for n in range(nchunk):
                entering[b, p, n] = st.astype(BF16)
                st = decay_row(b, p, n) * st + kv[b, p, n]
            s_ref[b, p] = st
    return entering


def _hg_tile(tile, rs, x_ref, g_ref, w_ref, lb_ref, ng_ref, o_ref, u_ref, s_ref, *, npad):
    nb, T, SB = x_ref.shape[0], ROW_TILE, SUB_BLOCK
    hn = _stacked_norm_in(x_ref, rs, g_ref, tile * T, npad)
    z = _mm(hn, w_ref[...])
    zq, zf, v, zg = z[:, 0:256], z[:, 256:512], z[:, 512:1024], z[:, 1024:1536]
    for b in range(nb):
        u_ref[b, rs, :] = z[b * T:(b + 1) * T, 1536:2048]

    log_lb, log1m_lb, one_m_lb = lb_ref[0:1, :], lb_ref[1:2, :], lb_ref[2:3, :]
    e = jnp.exp(-jnp.abs(zf))
    inv1pe = 1.0 / (1.0 + e)
    log_sig = jnp.minimum(zf, 0.0) - jnp.log1p(e)
    y2 = log1m_lb + log_sig
    log_f = jnp.maximum(log_lb, y2) + jnp.log1p(jnp.exp(-jnp.abs(log_lb - y2)))
    k = one_m_lb * jnp.where(zf >= 0.0, e * inv1pe, inv1pe)
    q = _silu(zq) * (HG_DK ** -0.5)

    same, low, _, _ = _seg_masks(T, HG_CHUNK)
    per_batch = lambda m01: jnp.concatenate([_mm01(m01, log_f[b * T:(b + 1) * T]) for b in range(nb)], axis=0)
    b_cum = per_batch(low.astype(BF16))
    b_last = per_batch(same.astype(BF16))
    q_t = q * jnp.exp(b_cum)
    k_t = (k * jnp.exp(-b_cum)).astype(BF16)
    k_end = (k * jnp.exp(b_last - b_cum)).astype(BF16)
    d_end = jnp.exp(b_last)
    q_tb = q_t.astype(BF16)
    vb = v.astype(BF16)
    low_t = low[0:SB, 0:SB]

    nblk = nb * T // SB
    chains = [(k, h) for k in range(nblk) for h in range(HG_HEADS)]
    rows = lambda k: slice(k * SB, (k + 1) * SB)
    slab = lambda p: slice(128 * p, 128 * p + 128)

    qh = [jnp.where(_head_lane_mask(h % 2), q_t[rows(k), slab(h // 2)], 0.0).astype(BF16) for k, h in chains]
    att = [jnp.where(low_t, _mm_nt(qh[c], k_t[rows(k), slab(h // 2)]), 0.0).astype(BF16)
           for c, (k, h) in enumerate(chains)]
    o_intra = [_mm(att[c], vb[rows(k), slab(h)]) for c, (k, h) in enumerate(chains)]

    nchunk = T // HG_CHUNK
    crow = lambda b, n: slice(b * T + n * HG_CHUNK, b * T + (n + 1) * HG_CHUNK)
    blk = _pair_block_mask()
    kv = {(b, p, n): jnp.where(blk, _mm_tn(vb[crow(b, n), 256 * p:256 * p + 256], k_end[crow(b, n), slab(p)]), 0.0)
          for b in range(nb) for n in range(nchunk) for p in range(2)}
    entering = _pair_state_scan(
        s_ref, kv, lambda b, p, n: d_end[b * T + n * HG_CHUNK:b * T + n * HG_CHUNK + 1, slab(p)], nb, nchunk)
    o_inter = {(b, p, n): _mm_nt(q_tb[crow(b, n), slab(p)], entering[b, p, n])
               for b in range(nb) for n in range(nchunk) for p in range(2)}

    inter = jnp.concatenate([jnp.concatenate([o_inter[b, 0, n], o_inter[b, 1, n]], axis=1)
                             for b in range(nb) for n in range(nchunk)], axis=0)
    intra = jnp.concatenate([jnp.concatenate([o_intra[k * HG_HEADS + h] for h in range(HG_HEADS)], axis=1)
                             for k in range(nblk)], axis=0)
    o = _head_rms(intra + inter, HG_DV) * ng_ref[...] * _silu(zg)
    for b in range(nb):
        o_ref[b, rs, :] = o[b * T:(b + 1) * T].astype(o_ref.dtype)


def _rot_half_swap(x):
    lane = lax.broadcasted_iota(jnp.int32, x.shape, 1)
    first_half = (lane & 63) < 32
    return jnp.where(first_half, pltpu.roll(x, x.shape[1] - 32, 1), pltpu.roll(x, 32, 1))


def _ret_tile(tile, rs, x_ref, g_ref, w_ref, cos_ref, sin_ref, dec_ref, xi_ref, zeta_ref, gc_ref, ng_ref,
              o_ref, s_ref, *, npad):
    nb, T, SB = x_ref.shape[0], ROW_TILE, SUB_BLOCK
    hn = _stacked_norm_in(x_ref, rs, g_ref, tile * T, npad)
    z = _mm(hn, w_ref[...])
    zq, zk, v, zg = z[:, 0:256], z[:, 256:512], z[:, 512:1024], z[:, 1024:1536]
    tile_rows = lambda a: jnp.concatenate([a] * (nb * T // a.shape[0]), axis=0)
    cos, sin = tile_rows(cos_ref[rs, :]), tile_rows(sin_ref[rs, :])
    q = zq * cos + _rot_half_swap(zq) * sin
    k = (zk * cos + _rot_half_swap(zk) * sin) * (RET_DK ** -0.5)
    kb, vb = k.astype(BF16), v.astype(BF16)
    q_xi = (q * tile_rows(xi_ref[...])).astype(BF16)
    k_zeta = (k * tile_rows(zeta_ref[...])).astype(BF16)

    nblk = nb * T // SB
    chains = [(kk, h) for kk in range(nblk) for h in range(RET_HEADS)]
    rows = lambda kk: slice(kk * SB, (kk + 1) * SB)
    slab = lambda p: slice(128 * p, 128 * p + 128)

    qh = [jnp.where(_head_lane_mask(h % 2), q[rows(kk), slab(h // 2)], 0.0).astype(BF16) for kk, h in chains]
    att = [(_mm_nt(qh[c], kb[rows(kk), slab(h // 2)]) * dec_ref[h]).astype(BF16) for c, (kk, h) in enumerate(chains)]
    o_intra = [_mm(att[c], vb[rows(kk), slab(h)]) for c, (kk, h) in enumerate(chains)]

    nchunk = T // CHUNK
    crow = lambda b, n: slice(b * T + n * CHUNK, b * T + (n + 1) * CHUNK)
    blk = _pair_block_mask()
    kv = {(b, p, n): jnp.where(blk, _mm_tn(vb[crow(b, n), 256 * p:256 * p + 256], k_zeta[crow(b, n), slab(p)]), 0.0)
          for b in range(nb) for n in range(nchunk) for p in range(2)}
    entering = _pair_state_scan(s_ref, kv, lambda b, p, n: gc_ref[:, slab(p)], nb, nchunk)
    o_inter = {(b, p, n): _mm_nt(q_xi[crow(b, n), slab(p)], entering[b, p, n])
               for b in range(nb) for n in range(nchunk) for p in range(2)}

    inter = jnp.concatenate([jnp.concatenate([o_inter[b, 0, n], o_inter[b, 1, n]], axis=1)
                             for b in range(nb) for n in range(nchunk)], axis=0)
    intra = jnp.concatenate([jnp.concatenate([o_intra[kk * RET_HEADS + h] for h in range(RET_HEADS)], axis=1)
                             for kk in range(nblk)], axis=0)
    o = intra + inter

    parts = []
    for h in range(RET_HEADS):
        oh = o[:, 128 * h:128 * h + 128]
        oc = oh - jnp.mean(oh, axis=-1, keepdims=True)
        parts.append(oc * lax.rsqrt(jnp.mean(oc * oc, axis=-1, keepdims=True) + EPS))
    o = jnp.concatenate(parts, axis=1) * ng_ref[...] * _silu(zg)
    for b in range(nb):
        o_ref[b, rs, :] = o[b * T:(b + 1) * T].astype(o_ref.dtype)


def _gdn_tile(tile, rs, x_ref, g_ref, w_ref, wba_ref, conv_ref, hp_col_ref, hp_row_ref, ng_ref,
              o_ref, s_ref, carry_ref, *, npad):
    nb, T, SB = x_ref.shape[0], ROW_TILE, SUB_BLOCK
    H = GDN_HEADS
    assert SB == 128
    hn = _stacked_norm_in(x_ref, rs, g_ref, tile * T, npad)
    z = _mm(hn, w_ref[...])
    zqkv, zg, zba = z[:, 0:1536], z[:, 1536:2048], z[:, 2048:2176]
    ba_row = _mm_nt(wba_ref[...], hn)

    xc = []
    for b in range(nb):
        zb = zqkv[b * T:(b + 1) * T]
        ext = jnp.concatenate([carry_ref[b], zb], axis=0)
        carry_ref[b] = zb[T - 8:T, :]
        acc = conv_ref[3:4, :] * zb
        for j in range(CONV_K - 1):
            s = CONV_K - 1 - j
            acc = acc + conv_ref[j:j + 1, :] * ext[8 - s:8 - s + T, :]
        xc.append(acc)
    qkv = _silu(jnp.concatenate(xc, axis=0))

    beta_col = _sigmoid(zba)
    g_col = hp_col_ref[0:1, :] * _softplus(zba + hp_col_ref[1:2, :])
    rep = nb * T // hp_row_ref.shape[2]
    g_row = (jnp.concatenate([hp_row_ref[0]] * rep, axis=1)
             * _softplus(ba_row + jnp.concatenate([hp_row_ref[1]] * rep, axis=1)))
    same, low, strict, up = _seg_masks(T, CHUNK)
    low_b, same_b, up_b = low.astype(BF16), same.astype(BF16), up.astype(BF16)
    bat = lambda b: slice(b * T, (b + 1) * T)
    cum_col = jnp.concatenate([_mm01(low_b, g_col[bat(b)]) for b in range(nb)], axis=0)
    last_col = jnp.concatenate([_mm01(same_b, g_col[bat(b)]) for b in range(nb)], axis=0)
    cum_row = jnp.concatenate([_mm01_right(g_row[:, bat(b)], up_b) for b in range(nb)], axis=1)
    low_t, strict_t = low[0:SB, 0:SB], strict[0:SB, 0:SB]
    eye = (lax.broadcasted_iota(jnp.int32, (SB, SB), 0) == lax.broadcasted_iota(jnp.int32, (SB, SB), 1)).astype(F32)

    nsb = T // SB
    chains = [(k, h) for k in range(nb * nsb) for h in range(H)]
    CS = range(len(chains))
    rows = lambda k: slice(k * SB, (k + 1) * SB)
    qn, kn, knb, vh, cum_c, last_c, beta_c, gamma = [], [], [], [], [], [], [], []
    for k, h in chains:
        qh = qkv[rows(k), 128 * h:128 * h + 128]
        kh = qkv[rows(k), 512 + 128 * h:512 + 128 * h + 128]
        vh.append(qkv[rows(k), 1024 + 128 * h:1024 + 128 * h + 128])
        qn.append(qh * lax.rsqrt(jnp.sum(qh * qh, axis=-1, keepdims=True) + EPS) * (GDN_DK ** -0.5))
        kn.append(kh * lax.rsqrt(jnp.sum(kh * kh, axis=-1, keepdims=True) + EPS))
        knb.append(kn[-1].astype(BF16))
        cum_c.append(jnp.broadcast_to(cum_col[rows(k), H + h:H + h + 1], (SB, 128)))
        last_c.append(jnp.broadcast_to(last_col[rows(k), H + h:H + h + 1], (SB, 128)))
        beta_c.append(jnp.broadcast_to(beta_col[rows(k), h:h + 1], (SB, 128)))
        gamma.append(jnp.exp(jnp.where(low_t, cum_c[-1] - cum_row[H + h:H + h + 1, rows(k)], -jnp.inf)))

    kk = [_mm_nt(knb[c], knb[c]) for c in CS]
    qk = [_mm_nt(qn[c].astype(BF16), knb[c]) for c in CS]
    a0 = [jnp.where(strict_t, kk[c] * beta_c[c] * gamma[c], 0.0) for c in CS]
    inv = [eye - a0[c] for c in CS]
    a0b = [a.astype(BF16) for a in a0]
    pw = [_mm(a0b[c], a0b[c]) for c in CS]
    nlev = int(math.log2(CHUNK)) - 1
    for lev in range(nlev):
        pwb = [p.astype(BF16) for p in pw]
        if lev + 1 < nlev:
            both = [_mm(pwb[c], jnp.concatenate([inv[c], pw[c]], axis=1).astype(BF16)) for c in CS]
            inv = [inv[c] + both[c][:, 0:SB] for c in CS]
            pw = [both[c][:, SB:2 * SB] for c in CS]
        else:
            inv = [inv[c] + _mm(pwb[c], inv[c].astype(BF16)) for c in CS]
    e_cum = [jnp.exp(cum_c[c]) for c in CS]
    uw = [_mm(inv[c].astype(BF16),
              jnp.concatenate([vh[c] * beta_c[c], kn[c] * beta_c[c] * e_cum[c]], axis=1).astype(BF16)) for c in CS]
    u = [x[:, 0:128] for x in uw]
    wb = [x[:, 128:256].astype(BF16) for x in uw]
    att = [(qk[c] * gamma[c]).astype(BF16) for c in CS]
    qd = [(qn[c] * e_cum[c]).astype(BF16) for c in CS]
    kd = [(kn[c] * jnp.exp(last_c[c] - cum_c[c])).astype(BF16) for c in CS]
    dl = [jnp.exp(last_c[c]) for c in CS]

    seqs = [(b, h) for b in range(nb) for h in range(H)]
    st = [s_ref[b, h] for b, h in seqs]
    o_rows = [[] for _ in CS]
    for s in range(nsb):
        cs = [(b * nsb + s) * H + h for b, h in seqs]
        v_rows = [[] for _ in seqs]
        for n in range(SB // CHUNK):
            r = slice(n * CHUNK, (n + 1) * CHUNK)
            stb = [x.astype(BF16) for x in st]
            v_new = [u[c][r] - _mm(wb[c][r], stb[i]) for i, c in enumerate(cs)]
            for i, c in enumerate(cs):
                v_rows[i].append(v_new[i])
                rest = [u[c][(n + 1) * CHUNK:]] if (n + 1) * CHUNK < SB else []
                v_all = jnp.concatenate(v_rows[i] + rest, axis=0)
                o_rows[c].append(_mm(qd[c][r], stb[i]) + _mm(att[c][r], v_all.astype(BF16)))
            st = [dl[c][n * CHUNK:n * CHUNK + 1, :] * st[i] + _mm_tn(kd[c][r], v_new[i].astype(BF16))
                  for i, c in enumerate(cs)]
    for i, (b, h) in enumerate(seqs):
        s_ref[b, h] = st[i]
    o = jnp.concatenate([jnp.concatenate([jnp.concatenate(o_rows[k * H + h], axis=0) for h in range(H)], axis=1)
                         for k in range(nb * nsb)], axis=0)
    o = _head_rms(o, GDN_DV) * ng_ref[...] * _silu(zg)
    for b in range(nb):
        o_ref[b, rs, :] = o[b * T:(b + 1) * T].astype(o_ref.dtype)


def _shift_rows(x, s):
    n = x.shape[0]
    if s % 8 == 0:
        return jnp.concatenate([jnp.zeros((s, x.shape[1]), x.dtype), x[:n - s]], axis=0)
    rows = lax.broadcasted_iota(jnp.int32, x.shape, 0)
    return jnp.where(rows >= s, pltpu.roll(x, s, 0), 0.0)


def _s5_kernel(u_ref, kc_ref, tab_ref, mul_ref, d_ref, y_ref):
    C, P = SSM_CHUNK, SSM_GROUP
    rb = u_ref.shape[0]
    ut = [u_ref[:, t, :] for t in range(C)]
    yg = []
    for gl in range(u_ref.shape[2] // P):
        ug = jnp.concatenate([ut[t][:, gl * P:(gl + 1) * P] for t in range(C)], axis=1)
        ub = ug.astype(BF16)
        kc = kc_ref[gl]
        toep = jnp.concatenate(
            [kc] + [jnp.concatenate([jnp.zeros((P, s * P), F32), kc[:, :(C - s) * P]], axis=1) for s in range(1, C)],
            axis=0).astype(BF16)
        y = _mm(ub, toep)
        tab = tab_ref[gl]
        ba, bb, ca, cb = tab[0:16], tab[16:32], tab[32:48], tab[48:64]
        p_re = lambda k: tab[64 + k:65 + k]
        p_im = lambda k: tab[88 + k:89 + k]
        w_state = jnp.concatenate([ba * p_re(C - 1 - s) + bb * p_im(C - 1 - s) for s in range(C)], axis=0).astype(BF16)
        w_out_t = jnp.concatenate([ca * p_re(t + 1) + cb * p_im(t + 1) for t in range(C)], axis=0).astype(BF16)
        x = _mm(ub, w_state)
        row_in_blk = lax.broadcasted_iota(jnp.int32, x.shape, 0) & 7
        for k in range(3):
            s = 1 << k
            xs = jnp.where(row_in_blk >= s, pltpu.roll(x, s, 0), 0.0)
            x = x + xs * mul_ref[gl, 2 * k:2 * k + 1, :] + pltpu.roll(xs, 64, 1) * mul_ref[gl, 2 * k + 1:2 * k + 2, :]
        xsw = pltpu.roll(x, 64, 1)
        p1, p2 = mul_ref[gl, 8:16, :], mul_ref[gl, 16:24, :]
        blocks, last, last_sw = [x[0:8]], x[7:8], xsw[7:8]
        for j in range(1, rb // 8):
            xb = x[8 * j:8 * j + 8] + last * p1 + last_sw * p2
            xb_sw = xsw[8 * j:8 * j + 8] + last_sw * p1 - last * p2
            blocks.append(xb)
            last, last_sw = xb[7:8], xb_sw[7:8]
        xin = _shift_rows(jnp.concatenate(blocks, axis=0), 1)
        yg.append(y + _mm_nt(xin.astype(BF16), w_out_t))
    for t in range(C):
        yt = jnp.concatenate([y[:, t * P:(t + 1) * P] for y in yg], axis=1)
        y_ref[:, t, :] = yt + d_ref[...] * ut[t]


def _gelu_tanh(x):
    return 0.5 * x * (1.0 + jnp.tanh(math.sqrt(2.0 / math.pi) * (x + 0.044715 * (x * x * x))))


def _merge_kernel(x_ref, g_ref, b0_ref, b1_ref, y5_ref, b3_ref, wg_ref, wglu_ref, wb_ref, wo_ref,
                  o_ref, *, npad, tiles_per_batch):
    T = x_ref.shape[0]
    x = x_ref[...]
    row0 = lax.rem(pl.program_id(0), tiles_per_batch) * T
    hn = _norm_in(x, g_ref[...], row0, npad)
    ya = _gelu_tanh(y5_ref[...])
    b2 = ya * _sigmoid(_mm(ya.astype(BF16), wglu_ref[...]))
    branches = (b0_ref[...], b1_ref[...], b2.astype(BF16), b3_ref[...])
    dm = x.shape[1]
    merged = None
    for m in range(N_BRANCH):
        gate = _sigmoid(_mm(hn, wg_ref[:, m * dm:(m + 1) * dm]))
        term = gate * _mm(branches[m], wb_ref[m])
        merged = term if merged is None else merged + term
    o_ref[...] = x + _mm(merged.astype(BF16), wo_ref[...])


def _ffn_kernel(x_ref, g_ref, wgu_ref, wd_ref, o_ref, *, n_split):
    x = x_ref[...]
    h = (x * lax.rsqrt(jnp.mean(x * x, axis=-1, keepdims=True) + EPS) * g_ref[...]).astype(BF16)
    dff = wd_ref.shape[0]
    cw = dff // n_split
    acts = []
    for c in range(n_split):
        gt = _mm(h, wgu_ref[:, c * cw:(c + 1) * cw])
        up = _mm(h, wgu_ref[:, dff + c * cw:dff + (c + 1) * cw])
        acts.append((_silu(gt) * up).astype(BF16))
    acc = x
    for c in range(n_split):
        acc = acc + _mm(acts[c], wd_ref[c * cw:(c + 1) * cw, :])
    o_ref[...] = acc


def _final_kernel(x_ref, g_ref, o_ref):
    x = x_ref[...]
    o_ref[...] = x * lax.rsqrt(jnp.mean(x * x, axis=-1, keepdims=True) + EPS) * g_ref[...]


def _const_spec(block, index):
    return pl.BlockSpec(block, index, pipeline_mode=pl.Buffered(1))


def _layer_spec(shape, layer):
    zeros = (0,) * len(shape)
    return _const_spec((None,) + tuple(shape), lambda i: (layer,) + zeros)


def _params(sem):
    return pltpu.CompilerParams(dimension_semantics=sem, vmem_limit_bytes=VMEM_LIMIT)


def _s5_constants(a_re, a_im, log_dt, b_re, b_im, c_re, c_im):
    C, G, N, P = SSM_CHUNK, SSM_GROUPS, SSM_STATE, SSM_GROUP
    L = a_re.shape[0]
    dt = jnp.exp(log_dt)[..., None]
    mag = jnp.exp(dt * a_re)
    ar, ai = mag * jnp.cos(dt * a_im), mag * jnp.sin(dt * a_im)
    den = a_re * a_re + a_im * a_im
    nr, ni = ar - 1.0, ai
    cr, ci = (nr * a_re + ni * a_im) / den, (ni * a_re - nr * a_im) / den
    bbr = cr[..., None] * b_re - ci[..., None] * b_im
    bbi = cr[..., None] * b_im + ci[..., None] * b_re
    pr, pi = [jnp.ones_like(ar)], [jnp.zeros_like(ar)]
    for _ in range(C):
        pr, pi = pr + [pr[-1] * ar - pi[-1] * ai], pi + [pr[-1] * ai + pi[-1] * ar]
    pr, pi = jnp.stack(pr), jnp.stack(pi)
    car = c_re[None] * pr[:C, :, :, None, :] - c_im[None] * pi[:C, :, :, None, :]
    cai = c_re[None] * pi[:C, :, :, None, :] + c_im[None] * pr[:C, :, :, None, :]
    kern = jnp.sum(car[..., None] * bbr[None, :, :, None] - cai[..., None] * bbi[None, :, :, None], axis=4)
    kc = kern.transpose(1, 2, 4, 0, 3).reshape(L, G, P, C * P)
    bbr_t, bbi_t = jnp.swapaxes(bbr, 2, 3), jnp.swapaxes(bbi, 2, 3)
    cat = lambda a, b: jnp.concatenate([a, b], axis=-1)
    pw_r = jnp.moveaxis(cat(pr, pr), 0, 2)
    pw_i = jnp.moveaxis(cat(pi, pi), 0, 2)
    gap = jnp.zeros((L, G, 24 - (C + 1), 2 * N), F32)
    tab = jnp.concatenate([cat(bbr_t, bbi_t), cat(-bbi_t, bbr_t), cat(c_re, -c_im), cat(-c_im, -c_re),
                           pw_r, gap, pw_i, gap], axis=2)
    sr, si = pr[C], pi[C]
    lr, li, log_rows = sr, si, []
    for _ in range(3):
        log_rows += [cat(lr, lr), cat(-li, li)]
        lr, li = lr * lr - li * li, 2.0 * lr * li
    qr, qi, p1, p2 = sr, si, [], []
    for _ in range(8):
        p1, p2 = p1 + [cat(qr, qr)], p2 + [cat(-qi, qi)]
        qr, qi = qr * sr - qi * si, qr * si + qi * sr
    zero = jnp.zeros_like(log_rows[0])
    muls = jnp.stack(log_rows + [zero, zero] + p1 + p2, axis=2)
    return kc.astype(F32), tab.astype(F32), muls.astype(F32)


def _retention_tables(lp, npad, tile):
    half = RET_DK // 2
    pos = (jnp.arange(lp) - npad).astype(F32)
    inv = ROPE_BASE ** (-jnp.arange(half, dtype=F32) / half)
    ang = pos[:, None] * inv[None, :]
    cos, sin = jnp.cos(ang), jnp.sin(ang)
    cos_t = jnp.tile(jnp.concatenate([cos, cos], axis=1), (1, RET_HEADS))
    sin_t = jnp.tile(jnp.concatenate([-sin, sin], axis=1), (1, RET_HEADS))
    log_gamma = jnp.log1p(-jnp.exp2(-5.0 - jnp.arange(RET_HEADS, dtype=F32)))
    idx = jnp.arange(tile)
    loc = (idx % CHUNK).astype(F32)
    diff = (idx[:, None] - idx[None, :]).astype(F32)
    ok = ((idx[:, None] // CHUNK) == (idx[None, :] // CHUNK)) & (diff >= 0)
    dec = jnp.where(ok[None], jnp.exp(log_gamma[:, None, None] * jnp.maximum(diff, 0.0)[None]), 0.0)
    xi = jnp.repeat(jnp.exp(log_gamma[None, :] * (loc[:, None] + 1.0)), RET_DK, axis=1)
    zeta = jnp.repeat(jnp.exp(log_gamma[None, :] * (CHUNK - 1.0 - loc[:, None])), RET_DK, axis=1)
    gc = jnp.repeat(jnp.exp(log_gamma * CHUNK), RET_DK)[None, :]
    return cos_t, sin_t, dec, xi, zeta, gc


def kernel(x, meta, norm_mix, w_in, hg_lb_logits, hg_norm, ret_norm, ssm_a_re, ssm_a_im, ssm_log_dt,
           ssm_b_re, ssm_b_im, ssm_c_re, ssm_c_im, ssm_d, ssm_w_glu, gdn_conv, gdn_a_log, gdn_dt_bias,
           gdn_norm, w_branch, w_out, norm_ffn, w_gu, w_down, norm_final):
    bsz, seq, dm = x.shape
    depth = w_in.shape[0]
    T = ROW_TILE
    assert seq % T == 0 and dm % 128 == 0
    npad = (-(N_META + seq)) % T
    lp = npad + N_META + seq
    nt = lp // T
    rows = bsz * lp
    tm = next(t for t in DENSE_TILES if lp % t == 0)
    dff = w_down.shape[1]

    h = jnp.concatenate([jnp.zeros((bsz, npad, dm), F32),
                         jnp.broadcast_to(meta.astype(F32)[None], (bsz, N_META, dm)), x.astype(F32)], axis=1)
    h = h.reshape(rows, dm)

    w_hg = jnp.concatenate([w_in[:, :, 0:1536], w_in[:, :, 3072:3584]], axis=2).astype(BF16)
    w_rt = w_in[:, :, 1536:3072].astype(BF16)
    w_gd = jnp.concatenate([w_in[:, :, 3584:5120], w_in[:, :, 5128:5640],
                            jnp.pad(w_in[:, :, 5120:5128], ((0, 0), (0, 0), (0, 120)))], axis=2).astype(BF16)
    w_ba = jnp.swapaxes(w_in[:, :, 5120:5128], 1, 2).astype(BF16)
    w_gate = w_in[:, :, 5640:5640 + N_BRANCH * dm].astype(BF16)
    w_glu16, w_br16, w_out16 = ssm_w_glu.astype(BF16), w_branch.astype(BF16), w_out.astype(BF16)
    w_gu16, w_down16 = w_gu.astype(BF16), w_down.astype(BF16)

    lb = jnp.cumsum(jax.nn.softmax(hg_lb_logits.astype(F32), axis=0), axis=0)
    lb = lb - lb[:1]
    lb_rows = jnp.stack([jnp.log(lb), jnp.log1p(-lb), 1.0 - lb] + [jnp.zeros_like(lb)] * 5, axis=1)

    neg_decay = -jnp.exp(gdn_a_log.astype(F32))
    zeros4 = jnp.zeros_like(neg_decay)
    hp_col = jnp.stack([jnp.pad(jnp.concatenate([zeros4, neg_decay], axis=1), ((0, 0), (0, 120))),
                        jnp.pad(jnp.concatenate([zeros4, gdn_dt_bias.astype(F32)], axis=1), ((0, 0), (0, 120)))]
                       + [jnp.zeros((depth, 128), F32)] * 6, axis=1)
    hp_row = jnp.stack([jnp.concatenate([zeros4, neg_decay], axis=1),
                        jnp.concatenate([zeros4, gdn_dt_bias.astype(F32)], axis=1)], axis=1)
    hp_row = jnp.broadcast_to(hp_row[..., None], (depth, 2, 8, T))

    cos_t, sin_t, dec, xi, zeta, gc = _retention_tables(lp, npad, SUB_BLOCK)

    rchunks = rows // SSM_CHUNK
    f32 = lambda a: a.astype(F32)
    s5_kc, s5_tab, s5_muls = _s5_constants(f32(ssm_a_re), f32(ssm_a_im), f32(ssm_log_dt), f32(ssm_b_re),
                                           f32(ssm_b_im), f32(ssm_c_re), f32(ssm_c_im))

    gmix = norm_mix.astype(F32).reshape(depth, 1, dm)
    ni = next(n for n in MIXER_INNER_TILES if nt % n == 0)
    row3 = lambda w: pl.BlockSpec((bsz, ni * T, w), lambda i: (0, i, 0))
    tile_tab = lambda w: pl.BlockSpec((ni * T, w), lambda i: (i, 0))
    fixed = lambda *shape: _const_spec(shape, lambda i: (0,) * len(shape))
    vec = lambda w, l: _layer_spec((1, w), l)
    seq_params = _params(("arbitrary",))

    for l in range(depth):
        h3 = h.reshape(bsz, lp, dm)
        br0, u = pl.pallas_call(
            functools.partial(_tiled_mixer, _hg_tile, 1, npad=npad, n_inner=ni),
            grid=(nt // ni,),
            in_specs=[row3(dm), vec(dm, l), _layer_spec((dm, 2048), l), _layer_spec((8, 256), l), vec(512, l)],
            out_specs=[row3(512), row3(512)],
            out_shape=[jax.ShapeDtypeStruct((bsz, lp, 512), BF16), jax.ShapeDtypeStruct((bsz, lp, 512), F32)],
            scratch_shapes=[pltpu.VMEM((bsz, 2, 256, 128), F32)],
            compiler_params=seq_params, name="hgrn2_mixer",
        )(h3, gmix, w_hg, lb_rows, hg_norm.astype(F32).reshape(depth, 1, 512))

        br1 = pl.pallas_call(
            functools.partial(_tiled_mixer, _ret_tile, 1, npad=npad, n_inner=ni),
            grid=(nt // ni,),
            in_specs=[row3(dm), vec(dm, l), _layer_spec((dm, 1536), l), tile_tab(256), tile_tab(256),
                      fixed(RET_HEADS, SUB_BLOCK, SUB_BLOCK), fixed(SUB_BLOCK, 256), fixed(SUB_BLOCK, 256),
                      fixed(1, 256), vec(512, l)],
            out_specs=row3(512),
            out_shape=jax.ShapeDtypeStruct((bsz, lp, 512), BF16),
            scratch_shapes=[pltpu.VMEM((bsz, 2, 256, 128), F32)],
            compiler_params=seq_params, name="retention_mixer",
        )(h3, gmix, w_rt, cos_t, sin_t, dec, xi, zeta, gc, ret_norm.astype(F32).reshape(depth, 1, 512))

        br3 = pl.pallas_call(
            functools.partial(_tiled_mixer, _gdn_tile, 2, npad=npad, n_inner=ni),
            grid=(nt // ni,),
            in_specs=[row3(dm), vec(dm, l), _layer_spec((dm, 2176), l), _layer_spec((8, dm), l),
                      _layer_spec((CONV_K, 1536), l), _layer_spec((8, 128), l), _layer_spec((2, 8, T), l), vec(512, l)],
            out_specs=row3(512),
            out_shape=jax.ShapeDtypeStruct((bsz, lp, 512), BF16),
            scratch_shapes=[pltpu.VMEM((bsz, GDN_HEADS, GDN_DK, GDN_DV), F32), pltpu.VMEM((bsz, 8, 1536), F32)],
            compiler_params=seq_params, name="gdn_mixer",
        )(h3, gmix, w_gd, w_ba, gdn_conv.astype(F32), hp_col, hp_row, gdn_norm.astype(F32).reshape(depth, 1, 512))
        br0, br1, br3 = (a.reshape(rows, 512) for a in (br0, br1, br3))

        cp = SSM_CHUNK * SSM_GROUP
        gb = S5_GROUP_BLOCK
        cb = rchunks // bsz
        gspec = lambda a, b_: pl.BlockSpec((None, gb, a, b_), lambda b, g: (l, g, 0, 0))
        uspec = pl.BlockSpec((cb, SSM_CHUNK, gb * SSM_GROUP), lambda b, g: (b, 0, g))
        y5 = pl.pallas_call(
            _s5_kernel,
            grid=(bsz, SSM_GROUPS // gb),
            in_specs=[uspec, gspec(SSM_GROUP, cp), gspec(s5_tab.shape[2], 2 * SSM_STATE),
                      gspec(s5_muls.shape[2], 2 * SSM_STATE),
                      pl.BlockSpec((None, 1, gb * SSM_GROUP), lambda b, g: (l, 0, g))],
            out_specs=uspec,
            out_shape=jax.ShapeDtypeStruct((rchunks, SSM_CHUNK, SSM_WIDTH), F32),
            compiler_params=_params(("arbitrary", "arbitrary")), name="s5_mixer",
        )(u.reshape(rchunks, SSM_CHUNK, SSM_WIDTH), s5_kc, s5_tab, s5_muls,
          ssm_d.astype(F32).reshape(depth, 1, SSM_WIDTH))
        y5 = y5.reshape(rows, SSM_WIDTH)

        drow = lambda w: pl.BlockSpec((tm, w), lambda i: (i, 0))
        h = pl.pallas_call(
            functools.partial(_merge_kernel, npad=npad, tiles_per_batch=lp // tm),
            grid=(rows // tm,),
            in_specs=[drow(dm), vec(dm, l), drow(512), drow(512), drow(512), drow(512),
                      _layer_spec((dm, N_BRANCH * dm), l), _layer_spec((512, 512), l),
                      _layer_spec((N_BRANCH, BRANCH_WIDTH, dm), l), _layer_spec((dm, dm), l)],
            out_specs=drow(dm),
            out_shape=jax.ShapeDtypeStruct((rows, dm), F32),
            compiler_params=seq_params, name="merge_out",
        )(h, gmix, br0, br1, y5, br3, w_gate, w_glu16, w_br16, w_out16)

        h = pl.pallas_call(
            functools.partial(_ffn_kernel, n_split=FFN_SPLITS),
            grid=(rows // tm,),
            in_specs=[drow(dm), vec(dm, l), _layer_spec((dm, 2 * dff), l), _layer_spec((dff, dm), l)],
            out_specs=drow(dm),
            out_shape=jax.ShapeDtypeStruct((rows, dm), F32),
            compiler_params=seq_params, name="swiglu_ffn",
        )(h, norm_ffn.astype(F32).reshape(depth, 1, dm), w_gu16, w_down16)

    tf = next(t for t in FINAL_TILES if seq % t == 0)
    ns = seq // tf
    out = pl.pallas_call(
        _final_kernel,
        grid=(bsz, ns),
        in_specs=[pl.BlockSpec((pl.Element(tf), pl.Element(dm)),
                               lambda b, i: ((b * (lp // 8) + (npad + N_META) // 8 + i * (tf // 8)) * 8, 0)),
                  _const_spec((1, dm), lambda b, i: (0, 0))],
        out_specs=pl.BlockSpec((tf, dm), lambda b, i: (b * ns + i, 0)),
        out_shape=jax.ShapeDtypeStruct((bsz * seq, dm), x.dtype),
        compiler_params=_params(("arbitrary", "arbitrary")), name="final_norm",
    )(h, norm_final.astype(F32).reshape(1, dm))
    return out.reshape(bsz, seq, dm)
```

```python
import functools
import math

import jax
import jax.numpy as jnp
from jax import lax
from jax.experimental import pallas as pl
from jax.experimental.pallas import tpu as pltpu

F32 = jnp.float32
BF16 = jnp.bfloat16

N_META = 16
EPS = 1e-6
HG_HEADS, HG_DK, HG_DV, HG_CHUNK = 4, 64, 128, 16
RET_HEADS, RET_DK, RET_DV = 4, 64, 128
ROPE_BASE = 10000.0
SSM_WIDTH, SSM_GROUP, SSM_STATE = 512, 16, 64
SSM_GROUPS = SSM_WIDTH // SSM_GROUP
SSM_CHUNK = 16
S5_GROUP_BLOCK = 8
GDN_HEADS, GDN_DK, GDN_DV = 4, 128, 128
CONV_K = 4
CHUNK = 64
N_BRANCH = 4
BRANCH_WIDTH = 512

ROW_TILE = 256
SUB_BLOCK = 128
MIXER_INNER_TILES = (5, 4, 3, 2, 1)
DENSE_TILES = (640, 384, 320, 128)
FINAL_TILES = (1024, 512, 128)
FFN_SPLITS = 1
VMEM_LIMIT = 56 * 1024 * 1024

NT_DIMS = (((1,), (1,)), ((), ()))
TN_DIMS = (((0,), (0,)), ((), ()))


def _mm(a, b):
    return jnp.dot(a, b, preferred_element_type=F32)


def _mm_nt(a, b):
    return lax.dot_general(a, b, NT_DIMS, preferred_element_type=F32)


def _mm_tn(a, b):
    return lax.dot_general(a, b, TN_DIMS, preferred_element_type=F32)


def _split3(x):
    hi = x.astype(BF16)
    r = x - hi.astype(F32)
    mid = r.astype(BF16)
    lo = (r - mid.astype(F32)).astype(BF16)
    return hi, mid, lo


def _mm01(m01, x):
    hi, mid, lo = _split3(x)
    return _mm(m01, hi) + _mm(m01, mid) + _mm(m01, lo)


def _mm01_right(x, m01):
    hi, mid, lo = _split3(x)
    return _mm(hi, m01) + _mm(mid, m01) + _mm(lo, m01)


def _seg_masks(n, chunk):
    sh = int(math.log2(chunk))
    r = lax.broadcasted_iota(jnp.int32, (n, n), 0)
    c = lax.broadcasted_iota(jnp.int32, (n, n), 1)
    same = lax.shift_right_logical(r, sh) == lax.shift_right_logical(c, sh)
    return same, same & (c <= r), same & (c < r), same & (c >= r)


def _sigmoid(x):
    return jax.nn.sigmoid(x)


def _silu(x):
    return x * jax.nn.sigmoid(x)


def _softplus(x):
    return jnp.maximum(x, 0.0) + jnp.log1p(jnp.exp(-jnp.abs(x)))


def _norm_in(x, g, row0, npad):
    y = x * lax.rsqrt(jnp.mean(x * x, axis=-1, keepdims=True) + EPS) * g
    rows = row0 + lax.broadcasted_iota(jnp.int32, y.shape, 0)
    return jnp.where(rows >= npad, y, 0.0).astype(BF16)


def _stacked_norm_in(x_ref, rs, g_ref, row0, npad):
    return jnp.concatenate([_norm_in(x_ref[b, rs, :], g_ref[...], row0, npad) for b in range(x_ref.shape[0])], axis=0)


def _tiled_mixer(tile_fn, n_scratch, *refs, npad, n_inner):
    i = pl.program_id(0)

    @pl.when(i == 0)
    def _():
        for r in refs[len(refs) - n_scratch:]:
            r[...] = jnp.zeros_like(r)

    def step(j, carry):
        tile_fn(i * n_inner + j, pl.ds(pl.multiple_of(j * ROW_TILE, ROW_TILE), ROW_TILE), *refs, npad=npad)
        return carry

    lax.fori_loop(0, n_inner, step, 0)


def _head_rms(o, width):
    parts = []
    for h in range(o.shape[1] // width):
        oh = o[:, h * width:(h + 1) * width]
        parts.append(oh * lax.rsqrt(jnp.mean(oh * oh, axis=-1, keepdims=True) + EPS))
    return jnp.concatenate(parts, axis=1)


def _pair_block_mask():
    r = lax.broadcasted_iota(jnp.int32, (256, 128), 0)
    c = lax.broadcasted_iota(jnp.int32, (256, 128), 1)
    return lax.shift_right_logical(r, 7) == lax.shift_right_logical(c, 6)


def _head_lane_mask(a):
    c = lax.broadcasted_iota(jnp.int32, (1, 128), 1)
    return lax.shift_right_logical(c, 6) == a


def _pair_state_scan(s_ref, kv, decay_row, nb, nchunk):
    entering = {}
    for b in range(nb):
        for p in range(2):
            st = s_ref[b, p]
            for n in range(nchunk):
                entering[b, p, n] = st.astype(BF16)
                st = decay_row(b, p, n) * st + kv[b, p, n]
            s_ref[b, p] = st
    return entering


def _hg_tile(tile, rs, x_ref, g_ref, w_ref, wu_ref, lb_ref, ng_ref, o_ref, u_ref, s_ref, *, npad):
    nb, T, SB = x_ref.shape[0], ROW_TILE, SUB_BLOCK
    hn = _stacked_norm_in(x_ref, rs, g_ref, tile * T, npad)
    z = _mm(hn, w_ref[0])
    zq, zf, v, zg = z[:, 0:256], z[:, 256:512], z[:, 512:1024], z[:, 1024:1536]
    u = _mm(hn, wu_ref[0])
    for b in range(nb):
        u_ref[b, rs, :] = u[b * T:(b + 1) * T]

    log_lb, log1m_lb, one_m_lb = lb_ref[0:1, :], lb_ref[1:2, :], lb_ref[2:3, :]
    e = jnp.exp(-jnp.abs(zf))
    inv1pe = 1.0 / (1.0 + e)
    log_sig = jnp.minimum(zf, 0.0) - jnp.log1p(e)
    y2 = log1m_lb + log_sig
    log_f = jnp.maximum(log_lb, y2) + jnp.log1p(jnp.exp(-jnp.abs(log_lb - y2)))
    k = one_m_lb * jnp.where(zf >= 0.0, e * inv1pe, inv1pe)
    q = _silu(zq) * (HG_DK ** -0.5)

    same, low, _, _ = _seg_masks(T, HG_CHUNK)
    per_batch = lambda m01: jnp.concatenate([_mm01(m01, log_f[b * T:(b + 1) * T]) for b in range(nb)], axis=0)
    b_cum = per_batch(low.astype(BF16))
    b_last = per_batch(same.astype(BF16))
    q_t = q * jnp.exp(b_cum)
    k_t = (k * jnp.exp(-b_cum)).astype(BF16)
    k_end = (k * jnp.exp(b_last - b_cum)).astype(BF16)
    d_end = jnp.exp(b_last)
    q_tb = q_t.astype(BF16)
    vb = v.astype(BF16)
    low_t = low[0:SB, 0:SB]

    nblk = nb * T // SB
    chains = [(k, h) for k in range(nblk) for h in range(HG_HEADS)]
    rows = lambda k: slice(k * SB, (k + 1) * SB)
    slab = lambda p: slice(128 * p, 128 * p + 128)

    qh = [jnp.where(_head_lane_mask(h % 2), q_t[rows(k), slab(h // 2)], 0.0).astype(BF16) for k, h in chains]
    att = [jnp.where(low_t, _mm_nt(qh[c], k_t[rows(k), slab(h // 2)]), 0.0).astype(BF16)
           for c, (k, h) in enumerate(chains)]
    o_intra = [_mm(att[c], vb[rows(k), slab(h)]) for c, (k, h) in enumerate(chains)]

    nchunk = T // HG_CHUNK
    crow = lambda b, n: slice(b * T + n * HG_CHUNK, b * T + (n + 1) * HG_CHUNK)
    blk = _pair_block_mask()
    kv = {(b, p, n): jnp.where(blk, _mm_tn(vb[crow(b, n), 256 * p:256 * p + 256], k_end[crow(b, n), slab(p)]), 0.0)
          for b in range(nb) for n in range(nchunk) for p in range(2)}
    entering = _pair_state_scan(
        s_ref, kv, lambda b, p, n: d_end[b * T + n * HG_CHUNK:b * T + n * HG_CHUNK + 1, slab(p)], nb, nchunk)
    o_inter = {(b, p, n): _mm_nt(q_tb[crow(b, n), slab(p)], entering[b, p, n])
               for b in range(nb) for n in range(nchunk) for p in range(2)}

    inter = jnp.concatenate([jnp.concatenate([o_inter[b, 0, n], o_inter[b, 1, n]], axis=1)
                             for b in range(nb) for n in range(nchunk)], axis=0)
    intra = jnp.concatenate([jnp.concatenate([o_intra[k * HG_HEADS + h] for h in range(HG_HEADS)], axis=1)
                             for k in range(nblk)], axis=0)
    o = _head_rms(intra + inter, HG_DV) * ng_ref[...] * _silu(zg)
    for b in range(nb):
        o_ref[b, rs, :] = o[b * T:(b + 1) * T].astype(o_ref.dtype)


def _rot_half_swap(x):
    lane = lax.broadcasted_iota(jnp.int32, x.shape, 1)
    first_half = (lane & 63) < 32
    return jnp.where(first_half, pltpu.roll(x, x.shape[1] - 32, 1), pltpu.roll(x, 32, 1))


def _ret_tile(tile, rs, x_ref, g_ref, w_ref, cos_ref, sin_ref, dec_ref, xi_ref, zeta_ref, gc_ref, ng_ref,
              o_ref, s_ref, *, npad):
    nb, T, SB = x_ref.shape[0], ROW_TILE, SUB_BLOCK
    hn = _stacked_norm_in(x_ref, rs, g_ref, tile * T, npad)
    z = _mm(hn, w_ref[0])
    zq, zk, v, zg = z[:, 0:256], z[:, 256:512], z[:, 512:1024], z[:, 1024:1536]
    tile_rows = lambda a: jnp.concatenate([a] * (nb * T // a.shape[0]), axis=0)
    cos, sin = tile_rows(cos_ref[rs, :]), tile_rows(sin_ref[rs, :])
    q = zq * cos + _rot_half_swap(zq) * sin
    k = (zk * cos + _rot_half_swap(zk) * sin) * (RET_DK ** -0.5)
    kb, vb = k.astype(BF16), v.astype(BF16)
    q_xi = (q * tile_rows(xi_ref[...])).astype(BF16)
    k_zeta = (k * tile_rows(zeta_ref[...])).astype(BF16)

    nblk = nb * T // SB
    chains = [(kk, h) for kk in range(nblk) for h in range(RET_HEADS)]
    rows = lambda kk: slice(kk * SB, (kk + 1) * SB)
    slab = lambda p: slice(128 * p, 128 * p + 128)

    qh = [jnp.where(_head_lane_mask(h % 2), q[rows(kk), slab(h // 2)], 0.0).astype(BF16) for kk, h in chains]
    att = [(_mm_nt(qh[c], kb[rows(kk), slab(h // 2)]) * dec_ref[h]).astype(BF16) for c, (kk, h) in enumerate(chains)]
    o_intra = [_mm(att[c], vb[rows(kk), slab(h)]) for c, (kk, h) in enumerate(chains)]

    nchunk = T // CHUNK
    crow = lambda b, n: slice(b * T + n * CHUNK, b * T + (n + 1) * CHUNK)
    blk = _pair_block_mask()
    kv = {(b, p, n): jnp.where(blk, _mm_tn(vb[crow(b, n), 256 * p:256 * p + 256], k_zeta[crow(b, n), slab(p)]), 0.0)
          for b in range(nb) for n in range(nchunk) for p in range(2)}
    entering = _pair_state_scan(s_ref, kv, lambda b, p, n: gc_ref[:, slab(p)], nb, nchunk)
    o_inter = {(b, p, n): _mm_nt(q_xi[crow(b, n), slab(p)], entering[b, p, n])
               for b in range(nb) for n in range(nchunk) for p in range(2)}

    inter = jnp.concatenate([jnp.concatenate([o_inter[b, 0, n], o_inter[b, 1, n]], axis=1)
                             for b in range(nb) for n in range(nchunk)], axis=0)
    intra = jnp.concatenate([jnp.concatenate([o_intra[kk * RET_HEADS + h] for h in range(RET_HEADS)], axis=1)
                             for kk in range(nblk)], axis=0)
    o = intra + inter

    parts = []
    for h in range(RET_HEADS):
        oh = o[:, 128 * h:128 * h + 128]
        oc = oh - jnp.mean(oh, axis=-1, keepdims=True)
        parts.append(oc * lax.rsqrt(jnp.mean(oc * oc, axis=-1, keepdims=True) + EPS))
    o = jnp.concatenate(parts, axis=1) * ng_ref[...] * _silu(zg)
    for b in range(nb):
        o_ref[b, rs, :] = o[b * T:(b + 1) * T].astype(o_ref.dtype)


def _gdn_tile(tile, rs, x_ref, g_ref, w_ref, wg_ref, wbp_ref, wba_ref, conv_ref, hp_col_ref, hp_row_ref, ng_ref,
              o_ref, s_ref, carry_ref, *, npad):
    nb, T, SB = x_ref.shape[0], ROW_TILE, SUB_BLOCK
    H = GDN_HEADS
    assert SB == 128
    hn = _stacked_norm_in(x_ref, rs, g_ref, tile * T, npad)
    zqkv = _mm(hn, w_ref[0])
    zg = _mm(hn, wg_ref[0])
    zba = _mm(hn, wbp_ref[...])
    ba_row = _mm_nt(wba_ref[...], hn)

    xc = []
    for b in range(nb):
        zb = zqkv[b * T:(b + 1) * T]
        ext = jnp.concatenate([carry_ref[b], zb], axis=0)
        carry_ref[b] = zb[T - 8:T, :]
        acc = conv_ref[3:4, :] * zb
        for j in range(CONV_K - 1):
            s = CONV_K - 1 - j
            acc = acc + conv_ref[j:j + 1, :] * ext[8 - s:8 - s + T, :]
        xc.append(acc)
    qkv = _silu(jnp.concatenate(xc, axis=0))

    beta_col = _sigmoid(zba)
    g_col = hp_col_ref[0:1, :] * _softplus(zba + hp_col_ref[1:2, :])
    rep = nb * T // hp_row_ref.shape[2]
    g_row = (jnp.concatenate([hp_row_ref[0]] * rep, axis=1)
             * _softplus(ba_row + jnp.concatenate([hp_row_ref[1]] * rep, axis=1)))
    same, low, strict, up = _seg_masks(T, CHUNK)
    low_b, same_b, up_b = low.astype(BF16), same.astype(BF16), up.astype(BF16)
    bat = lambda b: slice(b * T, (b + 1) * T)
    cum_col = jnp.concatenate([_mm01(low_b, g_col[bat(b)]) for b in range(nb)], axis=0)
    last_col = jnp.concatenate([_mm01(same_b, g_col[bat(b)]) for b in range(nb)], axis=0)
    cum_row = jnp.concatenate([_mm01_right(g_row[:, bat(b)], up_b) for b in range(nb)], axis=1)
    low_t, strict_t = low[0:SB, 0:SB], strict[0:SB, 0:SB]
    eye = (lax.broadcasted_iota(jnp.int32, (SB, SB), 0) == lax.broadcasted_iota(jnp.int32, (SB, SB), 1)).astype(F32)

    nsb = T // SB
    chains = [(k, h) for k in range(nb * nsb) for h in range(H)]
    CS = range(len(chains))
    rows = lambda k: slice(k * SB, (k + 1) * SB)
    qn, kn, knb, vh, cum_c, last_c, beta_c, gamma = [], [], [], [], [], [], [], []
    for k, h in chains:
        qh = qkv[rows(k), 128 * h:128 * h + 128]
        kh = qkv[rows(k), 512 + 128 * h:512 + 128 * h + 128]
        vh.append(qkv[rows(k), 1024 + 128 * h:1024 + 128 * h + 128])
        qn.append(qh * lax.rsqrt(jnp.sum(qh * qh, axis=-1, keepdims=True) + EPS) * (GDN_DK ** -0.5))
        kn.append(kh * lax.rsqrt(jnp.sum(kh * kh, axis=-1, keepdims=True) + EPS))
        knb.append(kn[-1].astype(BF16))
        cum_c.append(jnp.broadcast_to(cum_col[rows(k), H + h:H + h + 1], (SB, 128)))
        last_c.append(jnp.broadcast_to(last_col[rows(k), H + h:H + h + 1], (SB, 128)))
        beta_c.append(jnp.broadcast_to(beta_col[rows(k), h:h + 1], (SB, 128)))
        gamma.append(jnp.exp(jnp.where(low_t, cum_c[-1] - cum_row[H + h:H + h + 1, rows(k)], -jnp.inf)))

    kk = [_mm_nt(knb[c], knb[c]) for c in CS]
    qk = [_mm_nt(qn[c].astype(BF16), knb[c]) for c in CS]
    a0 = [jnp.where(strict_t, kk[c] * beta_c[c] * gamma[c], 0.0) for c in CS]
    inv = [eye - a0[c] for c in CS]
    a0b = [a.astype(BF16) for a in a0]
    pw = [_mm(a0b[c], a0b[c]) for c in CS]
    nlev = int(math.log2(CHUNK)) - 1
    for lev in range(nlev):
        pwb = [p.astype(BF16) for p in pw]
        if lev + 1 < nlev:
            both = [_mm(pwb[c], jnp.concatenate([inv[c], pw[c]], axis=1).astype(BF16)) for c in CS]
            inv = [inv[c] + both[c][:, 0:SB] for c in CS]
            pw = [both[c][:, SB:2 * SB] for c in CS]
        else:
            inv = [inv[c] + _mm(pwb[c], inv[c].astype(BF16)) for c in CS]
    e_cum = [jnp.exp(cum_c[c]) for c in CS]
    uw = [_mm(inv[c].astype(BF16),
              jnp.concatenate([vh[c] * beta_c[c], kn[c] * beta_c[c] * e_cum[c]], axis=1).astype(BF16)) for c in CS]
    u = [x[:, 0:128] for x in uw]
    wb = [x[:, 128:256].astype(BF16) for x in uw]
    att = [(qk[c] * gamma[c]).astype(BF16) for c in CS]
    qd = [(qn[c] * e_cum[c]).astype(BF16) for c in CS]
    kd = [(kn[c] * jnp.exp(last_c[c] - cum_c[c])).astype(BF16) for c in CS]
    dl = [jnp.exp(last_c[c]) for c in CS]

    seqs = [(b, h) for b in range(nb) for h in range(H)]
    st = [s_ref[b, h] for b, h in seqs]
    o_rows = [[] for _ in CS]
    for s in range(nsb):
        cs = [(b * nsb + s) * H + h for b, h in seqs]
        v_rows = [[] for _ in seqs]
        for n in range(SB // CHUNK):
            r = slice(n * CHUNK, (n + 1) * CHUNK)
            stb = [x.astype(BF16) for x in st]
            v_new = [u[c][r] - _mm(wb[c][r], stb[i]) for i, c in enumerate(cs)]
            for i, c in enumerate(cs):
                v_rows[i].append(v_new[i])
                rest = [u[c][(n + 1) * CHUNK:]] if (n + 1) * CHUNK < SB else []
                v_all = jnp.concatenate(v_rows[i] + rest, axis=0)
                o_rows[c].append(_mm(qd[c][r], stb[i]) + _mm(att[c][r], v_all.astype(BF16)))
            st = [dl[c][n * CHUNK:n * CHUNK + 1, :] * st[i] + _mm_tn(kd[c][r], v_new[i].astype(BF16))
                  for i, c in enumerate(cs)]
    for i, (b, h) in enumerate(seqs):
        s_ref[b, h] = st[i]
    o = jnp.concatenate([jnp.concatenate([jnp.concatenate(o_rows[k * H + h], axis=0) for h in range(H)], axis=1)
                         for k in range(nb * nsb)], axis=0)
    o = _head_rms(o, GDN_DV) * ng_ref[...] * _silu(zg)
    for b in range(nb):
        o_ref[b, rs, :] = o[b * T:(b + 1) * T].astype(o_ref.dtype)


def _shift_rows(x, s):
    n = x.shape[0]
    if s % 8 == 0:
        return jnp.concatenate([jnp.zeros((s, x.shape[1]), x.dtype), x[:n - s]], axis=0)
    rows = lax.broadcasted_iota(jnp.int32, x.shape, 0)
    return jnp.where(rows >= s, pltpu.roll(x, s, 0), 0.0)


def _s5_kernel(u_ref, kc_ref, tab_ref, mul_ref, d_ref, y_ref):
    C, P = SSM_CHUNK, SSM_GROUP
    rb = u_ref.shape[0]
    ut = [u_ref[:, t, :] for t in range(C)]
    yg = []
    for gl in range(u_ref.shape[2] // P):
        ug = jnp.concatenate([ut[t][:, gl * P:(gl + 1) * P] for t in range(C)], axis=1)
        ub = ug.astype(BF16)
        kc = kc_ref[gl]
        toep = jnp.concatenate(
            [kc] + [jnp.concatenate([jnp.zeros((P, s * P), F32), kc[:, :(C - s) * P]], axis=1) for s in range(1, C)],
            axis=0).astype(BF16)
        y = _mm(ub, toep)
        tab = tab_ref[gl]
        ba, bb, ca, cb = tab[0:16], tab[16:32], tab[32:48], tab[48:64]
        p_re = lambda k: tab[64 + k:65 + k]
        p_im = lambda k: tab[88 + k:89 + k]
        w_state = jnp.concatenate([ba * p_re(C - 1 - s) + bb * p_im(C - 1 - s) for s in range(C)], axis=0).astype(BF16)
        w_out_t = jnp.concatenate([ca * p_re(t + 1) + cb * p_im(t + 1) for t in range(C)], axis=0).astype(BF16)
        x = _mm(ub, w_state)
        row_in_blk = lax.broadcasted_iota(jnp.int32, x.shape, 0) & 7
        for k in range(3):
            s = 1 << k
            xs = jnp.where(row_in_blk >= s, pltpu.roll(x, s, 0), 0.0)
            x = x + xs * mul_ref[gl, 2 * k:2 * k + 1, :] + pltpu.roll(xs, 64, 1) * mul_ref[gl, 2 * k + 1:2 * k + 2, :]
        xsw = pltpu.roll(x, 64, 1)
        p1, p2 = mul_ref[gl, 8:16, :], mul_ref[gl, 16:24, :]
        blocks, last, last_sw = [x[0:8]], x[7:8], xsw[7:8]
        for j in range(1, rb // 8):
            xb = x[8 * j:8 * j + 8] + last * p1 + last_sw * p2
            xb_sw = xsw[8 * j:8 * j + 8] + last_sw * p1 - last * p2
            blocks.append(xb)
            last, last_sw = xb[7:8], xb_sw[7:8]
        xin = _shift_rows(jnp.concatenate(blocks, axis=0), 1)
        yg.append(y + _mm_nt(xin.astype(BF16), w_out_t))
    for t in range(C):
        yt = jnp.concatenate([y[:, t * P:(t + 1) * P] for y in yg], axis=1)
        y_ref[:, t, :] = yt + d_ref[...] * ut[t]


def _gelu_tanh(x):
    return 0.5 * x * (1.0 + jnp.tanh(math.sqrt(2.0 / math.pi) * (x + 0.044715 * (x * x * x))))


def _merge_kernel(x_ref, g_ref, b0_ref, b1_ref, y5_ref, b3_ref, wg_ref, wglu_ref, wb_ref, wo_ref,
                  o_ref, *, npad, tiles_per_batch):
    T = x_ref.shape[0]
    x = x_ref[...]
    row0 = lax.rem(pl.program_id(0), tiles_per_batch) * T
    hn = _norm_in(x, g_ref[...], row0, npad)
    ya = _gelu_tanh(y5_ref[...])
    b2 = ya * _sigmoid(_mm(ya.astype(BF16), wglu_ref[...]))
    branches = (b0_ref[...], b1_ref[...], b2.astype(BF16), b3_ref[...])
    dm = x.shape[1]
    merged = None
    for m in range(N_BRANCH):
        gate = _sigmoid(_mm(hn, wg_ref[0, :, m * dm:(m + 1) * dm]))
        term = gate * _mm(branches[m], wb_ref[m])
        merged = term if merged is None else merged + term
    o_ref[...] = x + _mm(merged.astype(BF16), wo_ref[...])


def _ffn_kernel(x_ref, g_ref, wgu_ref, wd_ref, o_ref, *, n_split):
    x = x_ref[...]
    h = (x * lax.rsqrt(jnp.mean(x * x, axis=-1, keepdims=True) + EPS) * g_ref[...]).astype(BF16)
    dff = wd_ref.shape[0]
    cw = dff // n_split
    acts = []
    for c in range(n_split):
        gt = _mm(h, wgu_ref[:, c * cw:(c + 1) * cw])
        up = _mm(h, wgu_ref[:, dff + c * cw:dff + (c + 1) * cw])
        acts.append((_silu(gt) * up).astype(BF16))
    acc = x
    for c in range(n_split):
        acc = acc + _mm(acts[c], wd_ref[c * cw:(c + 1) * cw, :])
    o_ref[...] = acc


def _final_kernel(x_ref, g_ref, o_ref):
    x = x_ref[...]
    o_ref[...] = x * lax.rsqrt(jnp.mean(x * x, axis=-1, keepdims=True) + EPS) * g_ref[...]


def _const_spec(block, index):
    return pl.BlockSpec(block, index, pipeline_mode=pl.Buffered(1))


def _layer_spec(shape, layer):
    zeros = (0,) * len(shape)
    return _const_spec((None,) + tuple(shape), lambda i: (layer,) + zeros)


def _params(sem):
    return pltpu.CompilerParams(dimension_semantics=sem, vmem_limit_bytes=VMEM_LIMIT)


def _s5_constants(a_re, a_im, log_dt, b_re, b_im, c_re, c_im):
    C, G, N, P = SSM_CHUNK, SSM_GROUPS, SSM_STATE, SSM_GROUP
    L = a_re.shape[0]
    dt = jnp.exp(log_dt)[..., None]
    mag = jnp.exp(dt * a_re)
    ar, ai = mag * jnp.cos(dt * a_im), mag * jnp.sin(dt * a_im)
    den = a_re * a_re + a_im * a_im
    nr, ni = ar - 1.0, ai
    cr, ci = (nr * a_re + ni * a_im) / den, (ni * a_re - nr * a_im) / den
    bbr = cr[..., None] * b_re - ci[..., None] * b_im
    bbi = cr[..., None] * b_im + ci[..., None] * b_re
    pr, pi = [jnp.ones_like(ar)], [jnp.zeros_like(ar)]
    for _ in range(C):
        pr, pi = pr + [pr[-1] * ar - pi[-1] * ai], pi + [pr[-1] * ai + pi[-1] * ar]
    pr, pi = jnp.stack(pr), jnp.stack(pi)
    car = c_re[None] * pr[:C, :, :, None, :] - c_im[None] * pi[:C, :, :, None, :]
    cai = c_re[None] * pi[:C, :, :, None, :] + c_im[None] * pr[:C, :, :, None, :]
    bbr_t, bbi_t = jnp.swapaxes(bbr, 2, 3), jnp.swapaxes(bbi, 2, 3)
    kern = jnp.sum(car[:, :, :, :, None, :] * bbr_t[None, :, :, None] - cai[:, :, :, :, None, :] * bbi_t[None, :, :, None],
                   axis=-1)
    kc = kern.transpose(1, 2, 4, 0, 3).reshape(L, G, P, C * P)
    cat = lambda a, b: jnp.concatenate([a, b], axis=-1)
    pw_r = jnp.moveaxis(cat(pr, pr), 0, 2)
    pw_i = jnp.moveaxis(cat(pi, pi), 0, 2)
    gap = jnp.zeros((L, G, 24 - (C + 1), 2 * N), F32)
    tab = jnp.concatenate([cat(bbr_t, bbi_t), cat(-bbi_t, bbr_t), cat(c_re, -c_im), cat(-c_im, -c_re),
                           pw_r, gap, pw_i, gap], axis=2)
    sr, si = pr[C], pi[C]
    lr, li, log_rows = sr, si, []
    for _ in range(3):
        log_rows += [cat(lr, lr), cat(-li, li)]
        lr, li = lr * lr - li * li, 2.0 * lr * li
    qr, qi, p1, p2 = sr, si, [], []
    for _ in range(8):
        p1, p2 = p1 + [cat(qr, qr)], p2 + [cat(-qi, qi)]
        qr, qi = qr * sr - qi * si, qr * si + qi * sr
    zero = jnp.zeros_like(log_rows[0])
    muls = jnp.stack(log_rows + [zero, zero] + p1 + p2, axis=2)
    return kc.astype(F32), tab.astype(F32), muls.astype(F32)


def _retention_tables(lp, npad, tile):
    half = RET_DK // 2
    pos = (jnp.arange(lp) - npad).astype(F32)
    inv = ROPE_BASE ** (-jnp.arange(half, dtype=F32) / half)
    ang = pos[:, None] * inv[None, :]
    cos, sin = jnp.cos(ang), jnp.sin(ang)
    cos_t = jnp.tile(jnp.concatenate([cos, cos], axis=1), (1, RET_HEADS))
    sin_t = jnp.tile(jnp.concatenate([-sin, sin], axis=1), (1, RET_HEADS))
    log_gamma = jnp.log1p(-jnp.exp2(-5.0 - jnp.arange(RET_HEADS, dtype=F32)))
    idx = jnp.arange(tile)
    loc = (idx % CHUNK).astype(F32)
    diff = (idx[:, None] - idx[None, :]).astype(F32)
    ok = ((idx[:, None] // CHUNK) == (idx[None, :] // CHUNK)) & (diff >= 0)
    dec = jnp.where(ok[None], jnp.exp(log_gamma[:, None, None] * jnp.maximum(diff, 0.0)[None]), 0.0)
    xi = jnp.repeat(jnp.exp(log_gamma[None, :] * (loc[:, None] + 1.0)), RET_DK, axis=1)
    zeta = jnp.repeat(jnp.exp(log_gamma[None, :] * (CHUNK - 1.0 - loc[:, None])), RET_DK, axis=1)
    gc = jnp.repeat(jnp.exp(log_gamma * CHUNK), RET_DK)[None, :]
    return cos_t, sin_t, dec, xi, zeta, gc


def kernel(x, meta, norm_mix, w_in, hg_lb_logits, hg_norm, ret_norm, ssm_a_re, ssm_a_im, ssm_log_dt,
           ssm_b_re, ssm_b_im, ssm_c_re, ssm_c_im, ssm_d, ssm_w_glu, gdn_conv, gdn_a_log, gdn_dt_bias,
           gdn_norm, w_branch, w_out, norm_ffn, w_gu, w_down, norm_final):
    bsz, seq, dm = x.shape
    depth = w_in.shape[0]
    T = ROW_TILE
    assert seq % T == 0 and dm % 128 == 0
    npad = (-(N_META + seq)) % T
    lp = npad + N_META + seq
    nt = lp // T
    rows = bsz * lp
    tm = next(t for t in DENSE_TILES if lp % t == 0)
    dff = w_down.shape[1]

    h = jnp.concatenate([jnp.zeros((bsz, npad, dm), F32),
                         jnp.broadcast_to(meta.astype(F32)[None], (bsz, N_META, dm)), x.astype(F32)], axis=1)
    h = h.reshape(rows, dm)

    wb16 = w_in.astype(BF16)
    w_bpad = jnp.pad(wb16[:, :, 5120:5128], ((0, 0), (0, 0), (0, 120)))
    w_ba = jnp.swapaxes(wb16[:, :, 5120:5128], 1, 2)
    w_tail = wb16[:, :, 5128:5640 + N_BRANCH * dm]
    w_glu16, w_br16, w_out16 = ssm_w_glu.astype(BF16), w_branch.astype(BF16), w_out.astype(BF16)
    w_gu16, w_down16 = w_gu.astype(BF16), w_down.astype(BF16)

    lb = jnp.cumsum(jax.nn.softmax(hg_lb_logits.astype(F32), axis=0), axis=0)
    lb = lb - lb[:1]
    lb_rows = jnp.stack([jnp.log(lb), jnp.log1p(-lb), 1.0 - lb] + [jnp.zeros_like(lb)] * 5, axis=1)

    neg_decay = -jnp.exp(gdn_a_log.astype(F32))
    zeros4 = jnp.zeros_like(neg_decay)
    hp_col = jnp.stack([jnp.pad(jnp.concatenate([zeros4, neg_decay], axis=1), ((0, 0), (0, 120))),
                        jnp.pad(jnp.concatenate([zeros4, gdn_dt_bias.astype(F32)], axis=1), ((0, 0), (0, 120)))]
                       + [jnp.zeros((depth, 128), F32)] * 6, axis=1)
    hp_row = jnp.stack([jnp.concatenate([zeros4, neg_decay], axis=1),
                        jnp.concatenate([zeros4, gdn_dt_bias.astype(F32)], axis=1)], axis=1)
    hp_row = jnp.broadcast_to(hp_row[..., None], (depth, 2, 8, T))

    cos_t, sin_t, dec, xi, zeta, gc = _retention_tables(lp, npad, SUB_BLOCK)

    rchunks = rows // SSM_CHUNK
    f32 = lambda a: a.astype(F32)
    s5_kc, s5_tab, s5_muls = _s5_constants(f32(ssm_a_re), f32(ssm_a_im), f32(ssm_log_dt), f32(ssm_b_re),
                                           f32(ssm_b_im), f32(ssm_c_re), f32(ssm_c_im))

    gmix = norm_mix.astype(F32).reshape(depth, 1, dm)
    ni = next(n for n in MIXER_INNER_TILES if nt % n == 0)
    row3 = lambda w: pl.BlockSpec((bsz, ni * T, w), lambda i: (0, i, 0))
    tile_tab = lambda w: pl.BlockSpec((ni * T, w), lambda i: (i, 0))
    fixed = lambda *shape: _const_spec(shape, lambda i: (0,) * len(shape))
    vec = lambda w, l: _layer_spec((1, w), l)
    cols = lambda l, start, width: pl.BlockSpec((pl.Element(1), pl.Element(dm), pl.Element(width)),
                                                lambda i: (l, 0, start), pipeline_mode=pl.Buffered(1))
    seq_params = _params(("arbitrary",))

    for l in range(depth):
        h3 = h.reshape(bsz, lp, dm)
        br0, u = pl.pallas_call(
            functools.partial(_tiled_mixer, _hg_tile, 1, npad=npad, n_inner=ni),
            grid=(nt // ni,),
            in_specs=[row3(dm), vec(dm, l), cols(l, 0, 1536), cols(l, 3072, 512), _layer_spec((8, 256), l),
                      vec(512, l)],
            out_specs=[row3(512), row3(512)],
            out_shape=[jax.ShapeDtypeStruct((bsz, lp, 512), BF16), jax.ShapeDtypeStruct((bsz, lp, 512), F32)],
            scratch_shapes=[pltpu.VMEM((bsz, 2, 256, 128), F32)],
            compiler_params=seq_params, name="hgrn2_mixer",
        )(h3, gmix, wb16, wb16, lb_rows, hg_norm.astype(F32).reshape(depth, 1, 512))

        br1 = pl.pallas_call(
            functools.partial(_tiled_mixer, _ret_tile, 1, npad=npad, n_inner=ni),
            grid=(nt // ni,),
            in_specs=[row3(dm), vec(dm, l), cols(l, 1536, 1536), tile_tab(256), tile_tab(256),
                      fixed(RET_HEADS, SUB_BLOCK, SUB_BLOCK), fixed(SUB_BLOCK, 256), fixed(SUB_BLOCK, 256),
                      fixed(1, 256), vec(512, l)],
            out_specs=row3(512),
            out_shape=jax.ShapeDtypeStruct((bsz, lp, 512), BF16),
            scratch_shapes=[pltpu.VMEM((bsz, 2, 256, 128), F32)],
            compiler_params=seq_params, name="retention_mixer",
        )(h3, gmix, wb16, cos_t, sin_t, dec, xi, zeta, gc, ret_norm.astype(F32).reshape(depth, 1, 512))

        br3 = pl.pallas_call(
            functools.partial(_tiled_mixer, _gdn_tile, 2, npad=npad, n_inner=ni),
            grid=(nt // ni,),
            in_specs=[row3(dm), vec(dm, l), cols(l, 3584, 1536), cols(l, 0, 512), _layer_spec((dm, 128), l),
                      _layer_spec((8, dm), l), _layer_spec((CONV_K, 1536), l), _layer_spec((8, 128), l),
                      _layer_spec((2, 8, T), l), vec(512, l)],
            out_specs=row3(512),
            out_shape=jax.ShapeDtypeStruct((bsz, lp, 512), BF16),
            scratch_shapes=[pltpu.VMEM((bsz, GDN_HEADS, GDN_DK, GDN_DV), F32), pltpu.VMEM((bsz, 8, 1536), F32)],
            compiler_params=seq_params, name="gdn_mixer",
        )(h3, gmix, wb16, w_tail, w_bpad, w_ba, gdn_conv.astype(F32), hp_col, hp_row,
          gdn_norm.astype(F32).reshape(depth, 1, 512))
        br0, br1, br3 = (a.reshape(rows, 512) for a in (br0, br1, br3))

        cp = SSM_CHUNK * SSM_GROUP
        gb = S5_GROUP_BLOCK
        cb = rchunks // bsz
        gspec = lambda a, b_: pl.BlockSpec((None, gb, a, b_), lambda b, g: (l, g, 0, 0))
        uspec = pl.BlockSpec((cb, SSM_CHUNK, gb * SSM_GROUP), lambda b, g: (b, 0, g))
        y5 = pl.pallas_call(
            _s5_kernel,
            grid=(bsz, SSM_GROUPS // gb),
            in_specs=[uspec, gspec(SSM_GROUP, cp), gspec(s5_tab.shape[2], 2 * SSM_STATE),
                      gspec(s5_muls.shape[2], 2 * SSM_STATE),
                      pl.BlockSpec((None, 1, gb * SSM_GROUP), lambda b, g: (l, 0, g))],
            out_specs=uspec,
            out_shape=jax.ShapeDtypeStruct((rchunks, SSM_CHUNK, SSM_WIDTH), F32),
            compiler_params=_params(("arbitrary", "arbitrary")), name="s5_mixer",
        )(u.reshape(rchunks, SSM_CHUNK, SSM_WIDTH), s5_kc, s5_tab, s5_muls,
          ssm_d.astype(F32).reshape(depth, 1, SSM_WIDTH))
        y5 = y5.reshape(rows, SSM_WIDTH)

        drow = lambda w: pl.BlockSpec((tm, w), lambda i: (i, 0))
        h = pl.pallas_call(
            functools.partial(_merge_kernel, npad=npad, tiles_per_batch=lp // tm),
            grid=(rows // tm,),
            in_specs=[drow(dm), vec(dm, l), drow(512), drow(512), drow(512), drow(512),
                      cols(l, 512, N_BRANCH * dm), _layer_spec((512, 512), l),
                      _layer_spec((N_BRANCH, BRANCH_WIDTH, dm), l), _layer_spec((dm, dm), l)],
            out_specs=drow(dm),
            out_shape=jax.ShapeDtypeStruct((rows, dm), F32),
            compiler_params=seq_params, name="merge_out",
        )(h, gmix, br0, br1, y5, br3, w_tail, w_glu16, w_br16, w_out16)

        h = pl.pallas_call(
            functools.partial(_ffn_kernel, n_split=FFN_SPLITS),
            grid=(rows // tm,),
            in_specs=[drow(dm), vec(dm, l), _layer_spec((dm, 2 * dff), l), _layer_spec((dff, dm), l)],
            out_specs=drow(dm),
            out_shape=jax.ShapeDtypeStruct((rows, dm), F32),
            compiler_params=seq_params, name="swiglu_ffn",
        )(h, norm_ffn.astype(F32).reshape(depth, 1, dm), w_gu16, w_down16)

    tf = next(t for t in FINAL_TILES if seq % t == 0)
    ns = seq // tf
    out = pl.pallas_call(
        _final_kernel,
        grid=(bsz, ns),
        in_specs=[pl.BlockSpec((pl.Element(tf), pl.Element(dm)),
                               lambda b, i: ((b * (lp // 8) + (npad + N_META) // 8 + i * (tf // 8)) * 8, 0)),
                  _const_spec((1, dm), lambda b, i: (0, 0))],
        out_specs=pl.BlockSpec((tf, dm), lambda b, i: (b * ns + i, 0)),
        out_shape=jax.ShapeDtypeStruct((bsz * seq, dm), x.dtype),
        compiler_params=_params(("arbitrary", "arbitrary")), name="final_norm",
    )(h, norm_final.astype(F32).reshape(1, dm))
    return out.reshape(bsz, seq, dm)
```

```python
import functools
import math

import jax
import jax.numpy as jnp
from jax import lax
from jax.experimental import pallas as pl
from jax.experimental.pallas import tpu as pltpu

F32 = jnp.float32
BF16 = jnp.bfloat16

N_META = 16
EPS = 1e-6
HG_HEADS, HG_DK, HG_DV, HG_CHUNK = 4, 64, 128, 16
RET_HEADS, RET_DK, RET_DV = 4, 64, 128
ROPE_BASE = 10000.0
SSM_WIDTH, SSM_GROUP, SSM_STATE = 512, 16, 64
SSM_GROUPS = SSM_WIDTH // SSM_GROUP
SSM_CHUNK = 16
S5_GROUP_BLOCK = 8
GDN_HEADS, GDN_DK, GDN_DV = 4, 128, 128
CONV_K = 4
CHUNK = 64
N_BRANCH = 4
BRANCH_WIDTH = 512

ROW_TILE = 256
SUB_BLOCK = 128
MIXER_INNER_TILES = (5, 4, 3, 2, 1)
DENSE_TILES = (640, 384, 320, 128)
FINAL_TILES = (1024, 512, 128)
FFN_SPLITS = 1
VMEM_LIMIT = 56 * 1024 * 1024

NT_DIMS = (((1,), (1,)), ((), ()))
TN_DIMS = (((0,), (0,)), ((), ()))


def _mm(a, b):
    return jnp.dot(a, b, preferred_element_type=F32)


def _mm_nt(a, b):
    return lax.dot_general(a, b, NT_DIMS, preferred_element_type=F32)


def _mm_tn(a, b):
    return lax.dot_general(a, b, TN_DIMS, preferred_element_type=F32)


def _split3(x):
    hi = x.astype(BF16)
    r = x - hi.astype(F32)
    mid = r.astype(BF16)
    lo = (r - mid.astype(F32)).astype(BF16)
    return hi, mid, lo


def _mm01(m01, x):
    hi, mid, lo = _split3(x)
    return _mm(m01, hi) + _mm(m01, mid) + _mm(m01, lo)


def _mm01_right(x, m01):
    hi, mid, lo = _split3(x)
    return _mm(hi, m01) + _mm(mid, m01) + _mm(lo, m01)


def _seg_masks(n, chunk):
    sh = int(math.log2(chunk))
    r = lax.broadcasted_iota(jnp.int32, (n, n), 0)
    c = lax.broadcasted_iota(jnp.int32, (n, n), 1)
    same = lax.shift_right_logical(r, sh) == lax.shift_right_logical(c, sh)
    return same, same & (c <= r), same & (c < r), same & (c >= r)


def _sigmoid(x):
    return jax.nn.sigmoid(x)


def _silu(x):
    return x * jax.nn.sigmoid(x)


def _softplus(x):
    return jnp.maximum(x, 0.0) + jnp.log1p(jnp.exp(-jnp.abs(x)))


def _norm_in(x, g, row0, npad):
    y = x * lax.rsqrt(jnp.mean(x * x, axis=-1, keepdims=True) + EPS) * g
    rows = row0 + lax.broadcasted_iota(jnp.int32, y.shape, 0)
    return jnp.where(rows >= npad, y, 0.0).astype(BF16)


def _stacked_norm_in(x_ref, rs, g_ref, row0, npad):
    return jnp.concatenate([_norm_in(x_ref[b, rs, :], g_ref[...], row0, npad) for b in range(x_ref.shape[0])], axis=0)


def _tiled_mixer(tile_fn, n_scratch, *refs, npad, n_inner):
    i = pl.program_id(0)

    @pl.when(i == 0)
    def _():
        for r in refs[len(refs) - n_scratch:]:
            r[...] = jnp.zeros_like(r)

    def step(j, carry):
        tile_fn(i * n_inner + j, pl.ds(pl.multiple_of(j * ROW_TILE, ROW_TILE), ROW_TILE), *refs, npad=npad)
        return carry

    lax.fori_loop(0, n_inner, step, 0)


def _head_rms(o, width):
    parts = []
    for h in range(o.shape[1] // width):
        oh = o[:, h * width:(h + 1) * width]
        parts.append(oh * lax.rsqrt(jnp.mean(oh * oh, axis=-1, keepdims=True) + EPS))
    return jnp.concatenate(parts, axis=1)


def _pair_block_mask():
    r = lax.broadcasted_iota(jnp.int32, (256, 128), 0)
    c = lax.broadcasted_iota(jnp.int32, (256, 128), 1)
    return lax.shift_right_logical(r, 7) == lax.shift_right_logical(c, 6)


def _head_lane_mask(a):
    c = lax.broadcasted_iota(jnp.int32, (1, 128), 1)
    return lax.shift_right_logical(c, 6) == a


def _pair_state_scan(s_ref, kv, decay_row, nb, nchunk):
    entering = {}
    for b in range(nb):
        for p in range(2):
            st = s_ref[b, p]
            for n in range(nchunk):
                entering[b, p, n] = st.astype(BF16)
                st = decay_row(b, p, n) * st + kv[b, p, n]
            s_ref[b, p] = st
    return entering


def _hg_tile(tile, rs, x_ref, g_ref, w_ref, wu_ref, lb_ref, ng_ref, o_ref, u_ref, s_ref, *, npad):
    nb, T, SB = x_ref.shape[0], ROW_TILE, SUB_BLOCK
    hn = _stacked_norm_in(x_ref, rs, g_ref, tile * T, npad)
    z = _mm(hn, w_ref[0])
    zq, zf, v, zg = z[:, 0:256], z[:, 256:512], z[:, 512:1024], z[:, 1024:1536]
    u = _mm(hn, wu_ref[0])
    for b in range(nb):
        u_ref[b, rs, :] = u[b * T:(b + 1) * T]

    log_lb, log1m_lb, one_m_lb = lb_ref[0:1, :], lb_ref[1:2, :], lb_ref[2:3, :]
    e = jnp.exp(-jnp.abs(zf))
    inv1pe = 1.0 / (1.0 + e)
    log_sig = jnp.minimum(zf, 0.0) - jnp.log1p(e)
    y2 = log1m_lb + log_sig
    log_f = jnp.maximum(log_lb, y2) + jnp.log1p(jnp.exp(-jnp.abs(log_lb - y2)))
    k = one_m_lb * jnp.where(zf >= 0.0, e * inv1pe, inv1pe)
    q = _silu(zq) * (HG_DK ** -0.5)

    same, low, _, _ = _seg_masks(T, HG_CHUNK)
    per_batch = lambda m01: jnp.concatenate([_mm01(m01, log_f[b * T:(b + 1) * T]) for b in range(nb)], axis=0)
    b_cum = per_batch(low.astype(BF16))
    b_last = per_batch(same.astype(BF16))
    q_t = q * jnp.exp(b_cum)
    k_t = (k * jnp.exp(-b_cum)).astype(BF16)
    k_end = (k * jnp.exp(b_last - b_cum)).astype(BF16)
    d_end = jnp.exp(b_last)
    q_tb = q_t.astype(BF16)
    vb = v.astype(BF16)
    low_t = low[0:SB, 0:SB]

    nblk = nb * T // SB
    chains = [(k, h) for k in range(nblk) for h in range(HG_HEADS)]
    rows = lambda k: slice(k * SB, (k + 1) * SB)
    slab = lambda p: slice(128 * p, 128 * p + 128)

    qh = [jnp.where(_head_lane_mask(h % 2), q_t[rows(k), slab(h // 2)], 0.0).astype(BF16) for k, h in chains]
    att = [jnp.where(low_t, _mm_nt(qh[c], k_t[rows(k), slab(h // 2)]), 0.0).astype(BF16)
           for c, (k, h) in enumerate(chains)]
    o_intra = [_mm(att[c], vb[rows(k), slab(h)]) for c, (k, h) in enumerate(chains)]

    nchunk = T // HG_CHUNK
    crow = lambda b, n: slice(b * T + n * HG_CHUNK, b * T + (n + 1) * HG_CHUNK)
    blk = _pair_block_mask()
    kv = {(b, p, n): jnp.where(blk, _mm_tn(vb[crow(b, n), 256 * p:256 * p + 256], k_end[crow(b, n), slab(p)]), 0.0)
          for b in range(nb) for n in range(nchunk) for p in range(2)}
    entering = _pair_state_scan(
        s_ref, kv, lambda b, p, n: d_end[b * T + n * HG_CHUNK:b * T + n * HG_CHUNK + 1, slab(p)], nb, nchunk)
    o_inter = {(b, p, n): _mm_nt(q_tb[crow(b, n), slab(p)], entering[b, p, n])
               for b in range(nb) for n in range(nchunk) for p in range(2)}

    inter = jnp.concatenate([jnp.concatenate([o_inter[b, 0, n], o_inter[b, 1, n]], axis=1)
                             for b in range(nb) for n in range(nchunk)], axis=0)
    intra = jnp.concatenate([jnp.concatenate([o_intra[k * HG_HEADS + h] for h in range(HG_HEADS)], axis=1)
                             for k in range(nblk)], axis=0)
    o = _head_rms(intra + inter, HG_DV) * ng_ref[...] * _silu(zg)
    for b in range(nb):
        o_ref[b, rs, :] = o[b * T:(b + 1) * T].astype(o_ref.dtype)


def _rot_half_swap(x):
    lane = lax.broadcasted_iota(jnp.int32, x.shape, 1)
    first_half = (lane & 63) < 32
    return jnp.where(first_half, pltpu.roll(x, x.shape[1] - 32, 1), pltpu.roll(x, 32, 1))


def _ret_tile(tile, rs, x_ref, g_ref, w_ref, cos_ref, sin_ref, dec_ref, xi_ref, zeta_ref, gc_ref, ng_ref,
              o_ref, s_ref, *, npad):
    nb, T, SB = x_ref.shape[0], ROW_TILE, SUB_BLOCK
    hn = _stacked_norm_in(x_ref, rs, g_ref, tile * T, npad)
    z = _mm(hn, w_ref[0])
    zq, zk, v, zg = z[:, 0:256], z[:, 256:512], z[:, 512:1024], z[:, 1024:1536]
    tile_rows = lambda a: jnp.concatenate([a] * (nb * T // a.shape[0]), axis=0)
    cos, sin = tile_rows(cos_ref[rs, :]), tile_rows(sin_ref[rs, :])
    q = zq * cos + _rot_half_swap(zq) * sin
    k = (zk * cos + _rot_half_swap(zk) * sin) * (RET_DK ** -0.5)
    kb, vb = k.astype(BF16), v.astype(BF16)
    q_xi = (q * tile_rows(xi_ref[...])).astype(BF16)
    k_zeta = (k * tile_rows(zeta_ref[...])).astype(BF16)

    nblk = nb * T // SB
    chains = [(kk, h) for kk in range(nblk) for h in range(RET_HEADS)]
    rows = lambda kk: slice(kk * SB, (kk + 1) * SB)
    slab = lambda p: slice(128 * p, 128 * p + 128)

    qh = [jnp.where(_head_lane_mask(h % 2), q[rows(kk), slab(h // 2)], 0.0).astype(BF16) for kk, h in chains]
    att = [(_mm_nt(qh[c], kb[rows(kk), slab(h // 2)]) * dec_ref[h]).astype(BF16) for c, (kk, h) in enumerate(chains)]
    o_intra = [_mm(att[c], vb[rows(kk), slab(h)]) for c, (kk, h) in enumerate(chains)]

    nchunk = T // CHUNK
    crow = lambda b, n: slice(b * T + n * CHUNK, b * T + (n + 1) * CHUNK)
    blk = _pair_block_mask()
    kv = {(b, p, n): jnp.where(blk, _mm_tn(vb[crow(b, n), 256 * p:256 * p + 256], k_zeta[crow(b, n), slab(p)]), 0.0)
          for b in range(nb) for n in range(nchunk) for p in range(2)}
    entering = _pair_state_scan(s_ref, kv, lambda b, p, n: gc_ref[:, slab(p)], nb, nchunk)
    o_inter = {(b, p, n): _mm_nt(q_xi[crow(b, n), slab(p)], entering[b, p, n])
               for b in range(nb) for n in range(nchunk) for p in range(2)}

    inter = jnp.concatenate([jnp.concatenate([o_inter[b, 0, n], o_inter[b, 1, n]], axis=1)
                             for b in range(nb) for n in range(nchunk)], axis=0)
    intra = jnp.concatenate([jnp.concatenate([o_intra[kk * RET_HEADS + h] for h in range(RET_HEADS)], axis=1)
                             for kk in range(nblk)], axis=0)
    o = intra + inter

    parts = []
    for h in range(RET_HEADS):
        oh = o[:, 128 * h:128 * h + 128]
        oc = oh - jnp.mean(oh, axis=-1, keepdims=True)
        parts.append(oc * lax.rsqrt(jnp.mean(oc * oc, axis=-1, keepdims=True) + EPS))
    o = jnp.concatenate(parts, axis=1) * ng_ref[...] * _silu(zg)
    for b in range(nb):
        o_ref[b, rs, :] = o[b * T:(b + 1) * T].astype(o_ref.dtype)


def _gdn_tile(tile, rs, x_ref, g_ref, w_ref, wg_ref, wbp_ref, wba_ref, conv_ref, hp_col_ref, hp_row_ref, ng_ref,
              o_ref, s_ref, carry_ref, *, npad):
    nb, T, SB = x_ref.shape[0], ROW_TILE, SUB_BLOCK
    H = GDN_HEADS
    assert SB == 128
    hn = _stacked_norm_in(x_ref, rs, g_ref, tile * T, npad)
    zqkv = _mm(hn, w_ref[0])
    zg = _mm(hn, wg_ref[0])
    zba = _mm(hn, wbp_ref[...])
    ba_row = _mm_nt(wba_ref[...], hn)

    xc = []
    for b in range(nb):
        zb = zqkv[b * T:(b + 1) * T]
        ext = jnp.concatenate([carry_ref[b], zb], axis=0)
        carry_ref[b] = zb[T - 8:T, :]
        acc = conv_ref[3:4, :] * zb
        for j in range(CONV_K - 1):
            s = CONV_K - 1 - j
            acc = acc + conv_ref[j:j + 1, :] * ext[8 - s:8 - s + T, :]
        xc.append(acc)
    qkv = _silu(jnp.concatenate(xc, axis=0))

    beta_col = _sigmoid(zba)
    g_col = hp_col_ref[0:1, :] * _softplus(zba + hp_col_ref[1:2, :])
    rep = nb * T // hp_row_ref.shape[2]
    g_row = (jnp.concatenate([hp_row_ref[0]] * rep, axis=1)
             * _softplus(ba_row + jnp.concatenate([hp_row_ref[1]] * rep, axis=1)))
    same, low, strict, up = _seg_masks(T, CHUNK)
    low_b, same_b, up_b = low.astype(BF16), same.astype(BF16), up.astype(BF16)
    bat = lambda b: slice(b * T, (b + 1) * T)
    cum_col = jnp.concatenate([_mm01(low_b, g_col[bat(b)]) for b in range(nb)], axis=0)
    last_col = jnp.concatenate([_mm01(same_b, g_col[bat(b)]) for b in range(nb)], axis=0)
    cum_row = jnp.concatenate([_mm01_right(g_row[:, bat(b)], up_b) for b in range(nb)], axis=1)
    low_t, strict_t = low[0:SB, 0:SB], strict[0:SB, 0:SB]
    eye = (lax.broadcasted_iota(jnp.int32, (SB, SB), 0) == lax.broadcasted_iota(jnp.int32, (SB, SB), 1)).astype(F32)

    nsb = T // SB
    chains = [(k, h) for k in range(nb * nsb) for h in range(H)]
    CS = range(len(chains))
    rows = lambda k: slice(k * SB, (k + 1) * SB)
    qn, kn, knb, vh, cum_c, last_c, beta_c, gamma = [], [], [], [], [], [], [], []
    for k, h in chains:
        qh = qkv[rows(k), 128 * h:128 * h + 128]
        kh = qkv[rows(k), 512 + 128 * h:512 + 128 * h + 128]
        vh.append(qkv[rows(k), 1024 + 128 * h:1024 + 128 * h + 128])
        qn.append(qh * lax.rsqrt(jnp.sum(qh * qh, axis=-1, keepdims=True) + EPS) * (GDN_DK ** -0.5))
        kn.append(kh * lax.rsqrt(jnp.sum(kh * kh, axis=-1, keepdims=True) + EPS))
        knb.append(kn[-1].astype(BF16))
        cum_c.append(jnp.broadcast_to(cum_col[rows(k), H + h:H + h + 1], (SB, 128)))
        last_c.append(jnp.broadcast_to(last_col[rows(k), H + h:H + h + 1], (SB, 128)))
        beta_c.append(jnp.broadcast_to(beta_col[rows(k), h:h + 1], (SB, 128)))
        gamma.append(jnp.exp(jnp.where(low_t, cum_c[-1] - cum_row[H + h:H + h + 1, rows(k)], -jnp.inf)))

    kk = [_mm_nt(knb[c], knb[c]) for c in CS]
    qk = [_mm_nt(qn[c].astype(BF16), knb[c]) for c in CS]
    a0 = [jnp.where(strict_t, kk[c] * beta_c[c] * gamma[c], 0.0) for c in CS]
    inv = [eye - a0[c] for c in CS]
    a0b = [a.astype(BF16) for a in a0]
    pw = [_mm(a0b[c], a0b[c]) for c in CS]
    nlev = int(math.log2(CHUNK)) - 1
    for lev in range(nlev):
        pwb = [p.astype(BF16) for p in pw]
        if lev + 1 < nlev:
            both = [_mm(pwb[c], jnp.concatenate([inv[c], pw[c]], axis=1).astype(BF16)) for c in CS]
            inv = [inv[c] + both[c][:, 0:SB] for c in CS]
            pw = [both[c][:, SB:2 * SB] for c in CS]
        else:
            inv = [inv[c] + _mm(pwb[c], inv[c].astype(BF16)) for c in CS]
    e_cum = [jnp.exp(cum_c[c]) for c in CS]
    uw = [_mm(inv[c].astype(BF16),
              jnp.concatenate([vh[c] * beta_c[c], kn[c] * beta_c[c] * e_cum[c]], axis=1).astype(BF16)) for c in CS]
    u = [x[:, 0:128] for x in uw]
    wb = [x[:, 128:256].astype(BF16) for x in uw]
    att = [(qk[c] * gamma[c]).astype(BF16) for c in CS]
    qd = [(qn[c] * e_cum[c]).astype(BF16) for c in CS]
    kd = [(kn[c] * jnp.exp(last_c[c] - cum_c[c])).astype(BF16) for c in CS]
    dl = [jnp.exp(last_c[c]) for c in CS]

    seqs = [(b, h) for b in range(nb) for h in range(H)]
    st = [s_ref[b, h] for b, h in seqs]
    o_rows = [[] for _ in CS]
    for s in range(nsb):
        cs = [(b * nsb + s) * H + h for b, h in seqs]
        v_rows = [[] for _ in seqs]
        for n in range(SB // CHUNK):
            r = slice(n * CHUNK, (n + 1) * CHUNK)
            stb = [x.astype(BF16) for x in st]
            v_new = [u[c][r] - _mm(wb[c][r], stb[i]) for i, c in enumerate(cs)]
            for i, c in enumerate(cs):
                v_rows[i].append(v_new[i])
                rest = [u[c][(n + 1) * CHUNK:]] if (n + 1) * CHUNK < SB else []
                v_all = jnp.concatenate(v_rows[i] + rest, axis=0)
                o_rows[c].append(_mm(qd[c][r], stb[i]) + _mm(att[c][r], v_all.astype(BF16)))
            st = [dl[c][n * CHUNK:n * CHUNK + 1, :] * st[i] + _mm_tn(kd[c][r], v_new[i].astype(BF16))
                  for i, c in enumerate(cs)]
    for i, (b, h) in enumerate(seqs):
        s_ref[b, h] = st[i]
    o = jnp.concatenate([jnp.concatenate([jnp.concatenate(o_rows[k * H + h], axis=0) for h in range(H)], axis=1)
                         for k in range(nb * nsb)], axis=0)
    o = _head_rms(o, GDN_DV) * ng_ref[...] * _silu(zg)
    for b in range(nb):
        o_ref[b, rs, :] = o[b * T:(b + 1) * T].astype(o_ref.dtype)


def _shift_rows(x, s):
    n = x.shape[0]
    if s % 8 == 0:
        return jnp.concatenate([jnp.zeros((s, x.shape[1]), x.dtype), x[:n - s]], axis=0)
    rows = lax.broadcasted_iota(jnp.int32, x.shape, 0)
    return jnp.where(rows >= s, pltpu.roll(x, s, 0), 0.0)


def _s5_kernel(u_ref, kc_ref, tab_ref, mul_ref, d_ref, y_ref, m_ref, wst_ref, wot_ref):
    C, P = SSM_CHUNK, SSM_GROUP
    W = u_ref.shape[2]
    GB = W // P
    rb = u_ref.shape[0]

    @pl.when(pl.program_id(1) == 0)
    def _():
        shift = int(math.log2(P))
        row_g = lax.shift_right_logical(lax.broadcasted_iota(jnp.int32, (W, W), 0), shift)
        lane_g = lax.shift_right_logical(lax.broadcasted_iota(jnp.int32, (W, W), 1), shift)
        kstack = jnp.concatenate([kc_ref[g] for g in range(GB)], axis=0)
        lag_blocks = [jnp.where(row_g == lane_g, jnp.concatenate([kstack[:, lag * P:(lag + 1) * P]] * GB, axis=1), 0.0)
                      .astype(BF16) for lag in range(C)]
        zero = jnp.zeros((W, W), BF16)
        for s in range(C):
            for t in range(C):
                m_ref[s * W:(s + 1) * W, t * W:(t + 1) * W] = lag_blocks[t - s] if t >= s else zero
        tabs = [tab_ref[g] for g in range(GB)]
        row_g = lax.shift_right_logical(lax.broadcasted_iota(jnp.int32, (W, 128), 0), shift)
        for s in range(C):
            k_in, k_out = C - 1 - s, s + 1
            w_in_s = jnp.concatenate([tb[0:16] * tb[64 + k_in:65 + k_in] + tb[16:32] * tb[88 + k_in:89 + k_in]
                                      for tb in tabs], axis=0)
            w_out_s = jnp.concatenate([tb[32:48] * tb[64 + k_out:65 + k_out] + tb[48:64] * tb[88 + k_out:89 + k_out]
                                       for tb in tabs], axis=0)
            for g in range(GB):
                wst_ref[s * W:(s + 1) * W, g * 128:(g + 1) * 128] = jnp.where(row_g == g, w_in_s, 0.0).astype(BF16)
                wot_ref[s * W:(s + 1) * W, g * 128:(g + 1) * 128] = jnp.where(row_g == g, w_out_s, 0.0).astype(BF16)

    ut = [u_ref[:, t, :] for t in range(C)]
    v = jnp.concatenate(ut, axis=1).astype(BF16)
    conv = [_mm(v[:, 0:(2 * j + 2) * W], m_ref[0:(2 * j + 2) * W, 2 * j * W:(2 * j + 2) * W]) for j in range(C // 2)]
    z = _mm(v, wst_ref[...])
    xin = []
    for g in range(GB):
        x = z[:, g * 128:(g + 1) * 128]
        row_in_blk = lax.broadcasted_iota(jnp.int32, x.shape, 0) & 7
        for k in range(3):
            s = 1 << k
            xs = jnp.where(row_in_blk >= s, pltpu.roll(x, s, 0), 0.0)
            x = x + xs * mul_ref[g, 2 * k:2 * k + 1, :] + pltpu.roll(xs, 64, 1) * mul_ref[g, 2 * k + 1:2 * k + 2, :]
        xsw = pltpu.roll(x, 64, 1)
        p1, p2 = mul_ref[g, 8:16, :], mul_ref[g, 16:24, :]
        blocks, last, last_sw = [x[0:8]], x[7:8], xsw[7:8]
        for j in range(1, rb // 8):
            xb = x[8 * j:8 * j + 8] + last * p1 + last_sw * p2
            xb_sw = xsw[8 * j:8 * j + 8] + last_sw * p1 - last * p2
            blocks.append(xb)
            last, last_sw = xb[7:8], xb_sw[7:8]
        xin.append(_shift_rows(jnp.concatenate(blocks, axis=0), 1))
    from_state = _mm_nt(jnp.concatenate(xin, axis=1).astype(BF16), wot_ref[...])
    for t in range(C):
        y_ref[:, t, :] = (conv[t // 2][:, (t % 2) * W:(t % 2 + 1) * W] + from_state[:, t * W:(t + 1) * W]
                          + d_ref[...] * ut[t])


def _gelu_tanh(x):
    return 0.5 * x * (1.0 + jnp.tanh(math.sqrt(2.0 / math.pi) * (x + 0.044715 * (x * x * x))))


def _merge_kernel(x_ref, g_ref, b0_ref, b1_ref, y5_ref, b3_ref, wg_ref, wglu_ref, wb_ref, wo_ref,
                  o_ref, *, npad, tiles_per_batch):
    T = x_ref.shape[0]
    x = x_ref[...]
    row0 = lax.rem(pl.program_id(0), tiles_per_batch) * T
    hn = _norm_in(x, g_ref[...], row0, npad)
    ya = _gelu_tanh(y5_ref[...])
    b2 = ya * _sigmoid(_mm(ya.astype(BF16), wglu_ref[...]))
    branches = (b0_ref[...], b1_ref[...], b2.astype(BF16), b3_ref[...])
    dm = x.shape[1]
    merged = None
    for m in range(N_BRANCH):
        gate = _sigmoid(_mm(hn, wg_ref[0, :, m * dm:(m + 1) * dm]))
        term = gate * _mm(branches[m], wb_ref[m])
        merged = term if merged is None else merged + term
    o_ref[...] = x + _mm(merged.astype(BF16), wo_ref[...])


def _ffn_kernel(x_ref, g_ref, wgu_ref, wd_ref, o_ref, *, n_split):
    x = x_ref[...]
    h = (x * lax.rsqrt(jnp.mean(x * x, axis=-1, keepdims=True) + EPS) * g_ref[...]).astype(BF16)
    dff = wd_ref.shape[0]
    cw = dff // n_split
    acts = []
    for c in range(n_split):
        gt = _mm(h, wgu_ref[:, c * cw:(c + 1) * cw])
        up = _mm(h, wgu_ref[:, dff + c * cw:dff + (c + 1) * cw])
        acts.append((_silu(gt) * up).astype(BF16))
    acc = x
    for c in range(n_split):
        acc = acc + _mm(acts[c], wd_ref[c * cw:(c + 1) * cw, :])
    o_ref[...] = acc


def _final_kernel(x_ref, g_ref, o_ref):
    x = x_ref[...]
    o_ref[...] = x * lax.rsqrt(jnp.mean(x * x, axis=-1, keepdims=True) + EPS) * g_ref[...]


def _const_spec(block, index):
    return pl.BlockSpec(block, index, pipeline_mode=pl.Buffered(1))


def _layer_spec(shape, layer):
    zeros = (0,) * len(shape)
    return _const_spec((None,) + tuple(shape), lambda i: (layer,) + zeros)


def _params(sem):
    return pltpu.CompilerParams(dimension_semantics=sem, vmem_limit_bytes=VMEM_LIMIT)


def _s5_constants(a_re, a_im, log_dt, b_re, b_im, c_re, c_im):
    C, G, N, P = SSM_CHUNK, SSM_GROUPS, SSM_STATE, SSM_GROUP
    L = a_re.shape[0]
    dt = jnp.exp(log_dt)[..., None]
    mag = jnp.exp(dt * a_re)
    ar, ai = mag * jnp.cos(dt * a_im), mag * jnp.sin(dt * a_im)
    den = a_re * a_re + a_im * a_im
    nr, ni = ar - 1.0, ai
    cr, ci = (nr * a_re + ni * a_im) / den, (ni * a_re - nr * a_im) / den
    bbr = cr[..., None] * b_re - ci[..., None] * b_im
    bbi = cr[..., None] * b_im + ci[..., None] * b_re
    pr, pi = [jnp.ones_like(ar)], [jnp.zeros_like(ar)]
    for _ in range(C):
        pr, pi = pr + [pr[-1] * ar - pi[-1] * ai], pi + [pr[-1] * ai + pi[-1] * ar]
    pr, pi = jnp.stack(pr), jnp.stack(pi)
    car = c_re[None] * pr[:C, :, :, None, :] - c_im[None] * pi[:C, :, :, None, :]
    cai = c_re[None] * pi[:C, :, :, None, :] + c_im[None] * pr[:C, :, :, None, :]
    bbr_t, bbi_t = jnp.swapaxes(bbr, 2, 3), jnp.swapaxes(bbi, 2, 3)
    kern = jnp.sum(car[:, :, :, :, None, :] * bbr_t[None, :, :, None] - cai[:, :, :, :, None, :] * bbi_t[None, :, :, None],
                   axis=-1)
    kc = kern.transpose(1, 2, 4, 0, 3).reshape(L, G, P, C * P)
    cat = lambda a, b: jnp.concatenate([a, b], axis=-1)
    pw_r = jnp.moveaxis(cat(pr, pr), 0, 2)
    pw_i = jnp.moveaxis(cat(pi, pi), 0, 2)
    gap = jnp.zeros((L, G, 24 - (C + 1), 2 * N), F32)
    tab = jnp.concatenate([cat(bbr_t, bbi_t), cat(-bbi_t, bbr_t), cat(c_re, -c_im), cat(-c_im, -c_re),
                           pw_r, gap, pw_i, gap], axis=2)
    sr, si = pr[C], pi[C]
    lr, li, log_rows = sr, si, []
    for _ in range(3):
        log_rows += [cat(lr, lr), cat(-li, li)]
        lr, li = lr * lr - li * li, 2.0 * lr * li
    qr, qi, p1, p2 = sr, si, [], []
    for _ in range(8):
        p1, p2 = p1 + [cat(qr, qr)], p2 + [cat(-qi, qi)]
        qr, qi = qr * sr - qi * si, qr * si + qi * sr
    zero = jnp.zeros_like(log_rows[0])
    muls = jnp.stack(log_rows + [zero, zero] + p1 + p2, axis=2)
    return kc.astype(F32), tab.astype(F32), muls.astype(F32)


def _retention_tables(lp, npad, tile):
    half = RET_DK // 2
    pos = (jnp.arange(lp) - npad).astype(F32)
    inv = ROPE_BASE ** (-jnp.arange(half, dtype=F32) / half)
    ang = pos[:, None] * inv[None, :]
    cos, sin = jnp.cos(ang), jnp.sin(ang)
    cos_t = jnp.tile(jnp.concatenate([cos, cos], axis=1), (1, RET_HEADS))
    sin_t = jnp.tile(jnp.concatenate([-sin, sin], axis=1), (1, RET_HEADS))
    log_gamma = jnp.log1p(-jnp.exp2(-5.0 - jnp.arange(RET_HEADS, dtype=F32)))
    idx = jnp.arange(tile)
    loc = (idx % CHUNK).astype(F32)
    diff = (idx[:, None] - idx[None, :]).astype(F32)
    ok = ((idx[:, None] // CHUNK) == (idx[None, :] // CHUNK)) & (diff >= 0)
    dec = jnp.where(ok[None], jnp.exp(log_gamma[:, None, None] * jnp.maximum(diff, 0.0)[None]), 0.0)
    xi = jnp.repeat(jnp.exp(log_gamma[None, :] * (loc[:, None] + 1.0)), RET_DK, axis=1)
    zeta = jnp.repeat(jnp.exp(log_gamma[None, :] * (CHUNK - 1.0 - loc[:, None])), RET_DK, axis=1)
    gc = jnp.repeat(jnp.exp(log_gamma * CHUNK), RET_DK)[None, :]
    return cos_t, sin_t, dec, xi, zeta, gc


def kernel(x, meta, norm_mix, w_in, hg_lb_logits, hg_norm, ret_norm, ssm_a_re, ssm_a_im, ssm_log_dt,
           ssm_b_re, ssm_b_im, ssm_c_re, ssm_c_im, ssm_d, ssm_w_glu, gdn_conv, gdn_a_log, gdn_dt_bias,
           gdn_norm, w_branch, w_out, norm_ffn, w_gu, w_down, norm_final):
    bsz, seq, dm = x.shape
    depth = w_in.shape[0]
    T = ROW_TILE
    assert seq % T == 0 and dm % 128 == 0
    npad = (-(N_META + seq)) % T
    lp = npad + N_META + seq
    nt = lp // T
    rows = bsz * lp
    tm = next(t for t in DENSE_TILES if lp % t == 0)
    dff = w_down.shape[1]

    h = jnp.concatenate([jnp.zeros((bsz, npad, dm), F32),
                         jnp.broadcast_to(meta.astype(F32)[None], (bsz, N_META, dm)), x.astype(F32)], axis=1)
    h = h.reshape(rows, dm)

    wb16 = w_in.astype(BF16)
    w_bpad = jnp.pad(wb16[:, :, 5120:5128], ((0, 0), (0, 0), (0, 120)))
    w_ba = jnp.swapaxes(wb16[:, :, 5120:5128], 1, 2)
    w_tail = wb16[:, :, 5128:5640 + N_BRANCH * dm]
    w_glu16, w_br16, w_out16 = ssm_w_glu.astype(BF16), w_branch.astype(BF16), w_out.astype(BF16)
    w_gu16, w_down16 = w_gu.astype(BF16), w_down.astype(BF16)

    lb = jnp.cumsum(jax.nn.softmax(hg_lb_logits.astype(F32), axis=0), axis=0)
    lb = lb - lb[:1]
    lb_rows = jnp.stack([jnp.log(lb), jnp.log1p(-lb), 1.0 - lb] + [jnp.zeros_like(lb)] * 5, axis=1)

    neg_decay = -jnp.exp(gdn_a_log.astype(F32))
    zeros4 = jnp.zeros_like(neg_decay)
    hp_col = jnp.stack([jnp.pad(jnp.concatenate([zeros4, neg_decay], axis=1), ((0, 0), (0, 120))),
                        jnp.pad(jnp.concatenate([zeros4, gdn_dt_bias.astype(F32)], axis=1), ((0, 0), (0, 120)))]
                       + [jnp.zeros((depth, 128), F32)] * 6, axis=1)
    hp_row = jnp.stack([jnp.concatenate([zeros4, neg_decay], axis=1),
                        jnp.concatenate([zeros4, gdn_dt_bias.astype(F32)], axis=1)], axis=1)
    hp_row = jnp.broadcast_to(hp_row[..., None], (depth, 2, 8, T))

    cos_t, sin_t, dec, xi, zeta, gc = _retention_tables(lp, npad, SUB_BLOCK)

    rchunks = rows // SSM_CHUNK
    f32 = lambda a: a.astype(F32)
    s5_kc, s5_tab, s5_muls = _s5_constants(f32(ssm_a_re), f32(ssm_a_im), f32(ssm_log_dt), f32(ssm_b_re),
                                           f32(ssm_b_im), f32(ssm_c_re), f32(ssm_c_im))

    gmix = norm_mix.astype(F32).reshape(depth, 1, dm)
    ni = next(n for n in MIXER_INNER_TILES if nt % n == 0)
    row3 = lambda w: pl.BlockSpec((bsz, ni * T, w), lambda i: (0, i, 0))
    tile_tab = lambda w: pl.BlockSpec((ni * T, w), lambda i: (i, 0))
    fixed = lambda *shape: _const_spec(shape, lambda i: (0,) * len(shape))
    vec = lambda w, l: _layer_spec((1, w), l)
    cols = lambda l, start, width: pl.BlockSpec((pl.Element(1), pl.Element(dm), pl.Element(width)),
                                                lambda i: (l, 0, start), pipeline_mode=pl.Buffered(1))
    seq_params = _params(("arbitrary",))

    for l in range(depth):
        h3 = h.reshape(bsz, lp, dm)
        br0, u = pl.pallas_call(
            functools.partial(_tiled_mixer, _hg_tile, 1, npad=npad, n_inner=ni),
            grid=(nt // ni,),
            in_specs=[row3(dm), vec(dm, l), cols(l, 0, 1536), cols(l, 3072, 512), _layer_spec((8, 256), l),
                      vec(512, l)],
            out_specs=[row3(512), row3(512)],
            out_shape=[jax.ShapeDtypeStruct((bsz, lp, 512), BF16), jax.ShapeDtypeStruct((bsz, lp, 512), F32)],
            scratch_shapes=[pltpu.VMEM((bsz, 2, 256, 128), F32)],
            compiler_params=seq_params, name="hgrn2_mixer",
        )(h3, gmix, wb16, wb16, lb_rows, hg_norm.astype(F32).reshape(depth, 1, 512))

        br1 = pl.pallas_call(
            functools.partial(_tiled_mixer, _ret_tile, 1, npad=npad, n_inner=ni),
            grid=(nt // ni,),
            in_specs=[row3(dm), vec(dm, l), cols(l, 1536, 1536), tile_tab(256), tile_tab(256),
                      fixed(RET_HEADS, SUB_BLOCK, SUB_BLOCK), fixed(SUB_BLOCK, 256), fixed(SUB_BLOCK, 256),
                      fixed(1, 256), vec(512, l)],
            out_specs=row3(512),
            out_shape=jax.ShapeDtypeStruct((bsz, lp, 512), BF16),
            scratch_shapes=[pltpu.VMEM((bsz, 2, 256, 128), F32)],
            compiler_params=seq_params, name="retention_mixer",
        )(h3, gmix, wb16, cos_t, sin_t, dec, xi, zeta, gc, ret_norm.astype(F32).reshape(depth, 1, 512))

        br3 = pl.pallas_call(
            functools.partial(_tiled_mixer, _gdn_tile, 2, npad=npad, n_inner=ni),
            grid=(nt // ni,),
            in_specs=[row3(dm), vec(dm, l), cols(l, 3584, 1536), cols(l, 0, 512), _layer_spec((dm, 128), l),
                      _layer_spec((8, dm), l), _layer_spec((CONV_K, 1536), l), _layer_spec((8, 128), l),
                      _layer_spec((2, 8, T), l), vec(512, l)],
            out_specs=row3(512),
            out_shape=jax.ShapeDtypeStruct((bsz, lp, 512), BF16),
            scratch_shapes=[pltpu.VMEM((bsz, GDN_HEADS, GDN_DK, GDN_DV), F32), pltpu.VMEM((bsz, 8, 1536), F32)],
            compiler_params=seq_params, name="gdn_mixer",
        )(h3, gmix, wb16, w_tail, w_bpad, w_ba, gdn_conv.astype(F32), hp_col, hp_row,
          gdn_norm.astype(F32).reshape(depth, 1, 512))
        br0, br1, br3 = (a.reshape(rows, 512) for a in (br0, br1, br3))

        cp = SSM_CHUNK * SSM_GROUP
        gb = S5_GROUP_BLOCK
        cb = rchunks // bsz
        gspec = lambda a, b_: pl.BlockSpec((None, gb, a, b_), lambda g, b: (l, g, 0, 0))
        uspec = pl.BlockSpec((cb, SSM_CHUNK, gb * SSM_GROUP), lambda g, b: (b, 0, g))
        cw = SSM_CHUNK * gb * SSM_GROUP
        y5 = pl.pallas_call(
            _s5_kernel,
            grid=(SSM_GROUPS // gb, bsz),
            in_specs=[uspec, gspec(SSM_GROUP, cp), gspec(s5_tab.shape[2], 2 * SSM_STATE),
                      gspec(s5_muls.shape[2], 2 * SSM_STATE),
                      pl.BlockSpec((None, 1, gb * SSM_GROUP), lambda g, b: (l, 0, g))],
            out_specs=uspec,
            out_shape=jax.ShapeDtypeStruct((rchunks, SSM_CHUNK, SSM_WIDTH), F32),
            scratch_shapes=[pltpu.VMEM((cw, cw), BF16), pltpu.VMEM((cw, gb * 2 * SSM_STATE), BF16),
                            pltpu.VMEM((cw, gb * 2 * SSM_STATE), BF16)],
            compiler_params=_params(("arbitrary", "arbitrary")), name="s5_mixer",
        )(u.reshape(rchunks, SSM_CHUNK, SSM_WIDTH), s5_kc, s5_tab, s5_muls,
          ssm_d.astype(F32).reshape(depth, 1, SSM_WIDTH))
        y5 = y5.reshape(rows, SSM_WIDTH)

        drow = lambda w: pl.BlockSpec((tm, w), lambda i: (i, 0))
        h = pl.pallas_call(
            functools.partial(_merge_kernel, npad=npad, tiles_per_batch=lp // tm),
            grid=(rows // tm,),
            in_specs=[drow(dm), vec(dm, l), drow(512), drow(512), drow(512), drow(512),
                      cols(l, 512, N_BRANCH * dm), _layer_spec((512, 512), l),
                      _layer_spec((N_BRANCH, BRANCH_WIDTH, dm), l), _layer_spec((dm, dm), l)],
            out_specs=drow(dm),
            out_shape=jax.ShapeDtypeStruct((rows, dm), F32),
            compiler_params=seq_params, name="merge_out",
        )(h, gmix, br0, br1, y5, br3, w_tail, w_glu16, w_br16, w_out16)

        h = pl.pallas_call(
            functools.partial(_ffn_kernel, n_split=FFN_SPLITS),
            grid=(rows // tm,),
            in_specs=[drow(dm), vec(dm, l), _layer_spec((dm, 2 * dff), l), _layer_spec((dff, dm), l)],
            out_specs=drow(dm),
            out_shape=jax.ShapeDtypeStruct((rows, dm), F32),
            compiler_params=seq_params, name="swiglu_ffn",
        )(h, norm_ffn.astype(F32).reshape(depth, 1, dm), w_gu16, w_down16)

    tf = next(t for t in FINAL_TILES if seq % t == 0)
    ns = seq // tf
    out = pl.pallas_call(
        _final_kernel,
        grid=(bsz, ns),
        in_specs=[pl.BlockSpec((pl.Element(tf), pl.Element(dm)),
                               lambda b, i: ((b * (lp // 8) + (npad + N_META) // 8 + i * (tf // 8)) * 8, 0)),
                  _const_spec((1, dm), lambda b, i: (0, 0))],
        out_specs=pl.BlockSpec((tf, dm), lambda b, i: (b * ns + i, 0)),
        out_shape=jax.ShapeDtypeStruct((bsz * seq, dm), x.dtype),
        compiler_params=_params(("arbitrary", "arbitrary")), name="final_norm",
    )(h, norm_final.astype(F32).reshape(1, dm))
    return out.reshape(bsz, seq, dm)
```

```python
import functools
import math

import jax
import jax.numpy as jnp
from jax import lax
from jax.experimental import pallas as pl
from jax.experimental.pallas import tpu as pltpu

F32 = jnp.float32
BF16 = jnp.bfloat16

N_META = 16
EPS = 1e-6
HG_HEADS, HG_DK, HG_DV, HG_CHUNK = 4, 64, 128, 16
RET_HEADS, RET_DK, RET_DV = 4, 64, 128
ROPE_BASE = 10000.0
SSM_WIDTH, SSM_GROUP, SSM_STATE = 512, 16, 64
SSM_GROUPS = SSM_WIDTH // SSM_GROUP
SSM_CHUNK = 16
S5_GROUP_BLOCK = 8
GDN_HEADS, GDN_DK, GDN_DV = 4, 128, 128
CONV_K = 4
CHUNK = 64
N_BRANCH = 4
BRANCH_WIDTH = 512

ROW_TILE = 256
SUB_BLOCK = 128
MIXER_INNER_TILES = (5, 4, 3, 2, 1)
DENSE_TILES = (640, 384, 320, 128)
FINAL_TILES = (1024, 512, 128)
FFN_SPLITS = 1
VMEM_LIMIT = 56 * 1024 * 1024

NT_DIMS = (((1,), (1,)), ((), ()))
TN_DIMS = (((0,), (0,)), ((), ()))


def _mm(a, b):
    return jnp.dot(a, b, preferred_element_type=F32)


def _mm_nt(a, b):
    return lax.dot_general(a, b, NT_DIMS, preferred_element_type=F32)


def _mm_tn(a, b):
    return lax.dot_general(a, b, TN_DIMS, preferred_element_type=F32)


def _split3(x):
    hi = x.astype(BF16)
    r = x - hi.astype(F32)
    mid = r.astype(BF16)
    lo = (r - mid.astype(F32)).astype(BF16)
    return hi, mid, lo


def _mm01(m01, x):
    hi, mid, lo = _split3(x)
    return _mm(m01, hi) + _mm(m01, mid) + _mm(m01, lo)


def _mm01_right(x, m01):
    hi, mid, lo = _split3(x)
    return _mm(hi, m01) + _mm(mid, m01) + _mm(lo, m01)


def _seg_masks(n, chunk):
    sh = int(math.log2(chunk))
    r = lax.broadcasted_iota(jnp.int32, (n, n), 0)
    c = lax.broadcasted_iota(jnp.int32, (n, n), 1)
    same = lax.shift_right_logical(r, sh) == lax.shift_right_logical(c, sh)
    return same, same & (c <= r), same & (c < r), same & (c >= r)


def _sigmoid(x):
    return jax.nn.sigmoid(x)


def _silu(x):
    return x * jax.nn.sigmoid(x)


def _softplus(x):
    return jnp.maximum(x, 0.0) + jnp.log1p(jnp.exp(-jnp.abs(x)))


def _norm_in(x, g, row0, npad):
    y = x * lax.rsqrt(jnp.mean(x * x, axis=-1, keepdims=True) + EPS) * g
    rows = row0 + lax.broadcasted_iota(jnp.int32, y.shape, 0)
    return jnp.where(rows >= npad, y, 0.0).astype(BF16)


def _stacked_norm_in(x_ref, rs, g_ref, row0, npad):
    return jnp.concatenate([_norm_in(x_ref[b, rs, :], g_ref[...], row0, npad) for b in range(x_ref.shape[0])], axis=0)


def _tiled_mixer(tile_fn, n_scratch, *refs, npad, n_inner):
    i = pl.program_id(0)

    @pl.when(i == 0)
    def _():
        for r in refs[len(refs) - n_scratch:]:
            r[...] = jnp.zeros_like(r)

    def step(j, carry):
        tile_fn(i * n_inner + j, pl.ds(pl.multiple_of(j * ROW_TILE, ROW_TILE), ROW_TILE), *refs, npad=npad)
        return carry

    lax.fori_loop(0, n_inner, step, 0)


def _head_rms(o, width):
    parts = []
    for h in range(o.shape[1] // width):
        oh = o[:, h * width:(h + 1) * width]
        parts.append(oh * lax.rsqrt(jnp.mean(oh * oh, axis=-1, keepdims=True) + EPS))
    return jnp.concatenate(parts, axis=1)


def _pair_block_mask():
    r = lax.broadcasted_iota(jnp.int32, (256, 128), 0)
    c = lax.broadcasted_iota(jnp.int32, (256, 128), 1)
    return lax.shift_right_logical(r, 7) == lax.shift_right_logical(c, 6)


def _head_lane_mask(a):
    c = lax.broadcasted_iota(jnp.int32, (1, 128), 1)
    return lax.shift_right_logical(c, 6) == a


def _pair_state_scan(s_ref, kv, decay_row, nb, nchunk):
    entering = {}
    for b in range(nb):
        for p in range(2):
            st = s_ref[b, p]
            for n in range(nchunk):
                entering[b, p, n] = st.astype(BF16)
                st = decay_row(b, p, n) * st + kv[b, p, n]
            s_ref[b, p] = st
    return entering


def _hg_tile(tile, rs, x_ref, g_ref, w_ref, wu_ref, lb_ref, ng_ref, o_ref, u_ref, s_ref, *, npad):
    nb, T, SB = x_ref.shape[0], ROW_TILE, SUB_BLOCK
    hn = _stacked_norm_in(x_ref, rs, g_ref, tile * T, npad)
    z = _mm(hn, w_ref[0])
    zq, zf, v, zg = z[:, 0:256], z[:, 256:512], z[:, 512:1024], z[:, 1024:1536]
    u = _mm(hn, wu_ref[0])
    for b in range(nb):
        u_ref[b, rs, :] = u[b * T:(b + 1) * T]

    log_lb, log1m_lb, one_m_lb = lb_ref[0:1, :], lb_ref[1:2, :], lb_ref[2:3, :]
    e = jnp.exp(-jnp.abs(zf))
    inv1pe = 1.0 / (1.0 + e)
    log_sig = jnp.minimum(zf, 0.0) - jnp.log1p(e)
    y2 = log1m_lb + log_sig
    log_f = jnp.maximum(log_lb, y2) + jnp.log1p(jnp.exp(-jnp.abs(log_lb - y2)))
    k = one_m_lb * jnp.where(zf >= 0.0, e * inv1pe, inv1pe)
    q = _silu(zq) * (HG_DK ** -0.5)

    same, low, _, _ = _seg_masks(T, HG_CHUNK)
    per_batch = lambda m01: jnp.concatenate([_mm01(m01, log_f[b * T:(b + 1) * T]) for b in range(nb)], axis=0)
    b_cum = per_batch(low.astype(BF16))
    b_last = per_batch(same.astype(BF16))
    q_t = q * jnp.exp(b_cum)
    k_t = (k * jnp.exp(-b_cum)).astype(BF16)
    k_end = (k * jnp.exp(b_last - b_cum)).astype(BF16)
    d_end = jnp.exp(b_last)
    q_tb = q_t.astype(BF16)
    vb = v.astype(BF16)
    low_t = low[0:SB, 0:SB]

    nblk = nb * T // SB
    chains = [(k, h) for k in range(nblk) for h in range(HG_HEADS)]
    rows = lambda k: slice(k * SB, (k + 1) * SB)
    slab = lambda p: slice(128 * p, 128 * p + 128)

    qh = [jnp.where(_head_lane_mask(h % 2), q_t[rows(k), slab(h // 2)], 0.0).astype(BF16) for k, h in chains]
    att = [jnp.where(low_t, _mm_nt(qh[c], k_t[rows(k), slab(h // 2)]), 0.0).astype(BF16)
           for c, (k, h) in enumerate(chains)]
    o_intra = [_mm(att[c], vb[rows(k), slab(h)]) for c, (k, h) in enumerate(chains)]

    nchunk = T // HG_CHUNK
    crow = lambda b, n: slice(b * T + n * HG_CHUNK, b * T + (n + 1) * HG_CHUNK)
    blk = _pair_block_mask()
    kv = {(b, p, n): jnp.where(blk, _mm_tn(vb[crow(b, n), 256 * p:256 * p + 256], k_end[crow(b, n), slab(p)]), 0.0)
          for b in range(nb) for n in range(nchunk) for p in range(2)}
    entering = _pair_state_scan(
        s_ref, kv, lambda b, p, n: d_end[b * T + n * HG_CHUNK:b * T + n * HG_CHUNK + 1, slab(p)], nb, nchunk)
    o_inter = {(b, p, n): _mm_nt(q_tb[crow(b, n), slab(p)], entering[b, p, n])
               for b in range(nb) for n in range(nchunk) for p in range(2)}

    inter = jnp.concatenate([jnp.concatenate([o_inter[b, 0, n], o_inter[b, 1, n]], axis=1)
                             for b in range(nb) for n in range(nchunk)], axis=0)
    intra = jnp.concatenate([jnp.concatenate([o_intra[k * HG_HEADS + h] for h in range(HG_HEADS)], axis=1)
                             for k in range(nblk)], axis=0)
    o = _head_rms(intra + inter, HG_DV) * ng_ref[...] * _silu(zg)
    for b in range(nb):
        o_ref[b, rs, :] = o[b * T:(b + 1) * T].astype(o_ref.dtype)


def _rot_half_swap(x):
    lane = lax.broadcasted_iota(jnp.int32, x.shape, 1)
    first_half = (lane & 63) < 32
    return jnp.where(first_half, pltpu.roll(x, x.shape[1] - 32, 1), pltpu.roll(x, 32, 1))


def _ret_tile(tile, rs, x_ref, g_ref, w_ref, cos_ref, sin_ref, dec_ref, xi_ref, zeta_ref, gc_ref, ng_ref,
              o_ref, s_ref, *, npad):
    nb, T, SB = x_ref.shape[0], ROW_TILE, SUB_BLOCK
    hn = _stacked_norm_in(x_ref, rs, g_ref, tile * T, npad)
    z = _mm(hn, w_ref[0])
    zq, zk, v, zg = z[:, 0:256], z[:, 256:512], z[:, 512:1024], z[:, 1024:1536]
    tile_rows = lambda a: jnp.concatenate([a] * (nb * T // a.shape[0]), axis=0)
    cos, sin = tile_rows(cos_ref[rs, :]), tile_rows(sin_ref[rs, :])
    q = zq * cos + _rot_half_swap(zq) * sin
    k = (zk * cos + _rot_half_swap(zk) * sin) * (RET_DK ** -0.5)
    kb, vb = k.astype(BF16), v.astype(BF16)
    q_xi = (q * tile_rows(xi_ref[...])).astype(BF16)
    k_zeta = (k * tile_rows(zeta_ref[...])).astype(BF16)

    nblk = nb * T // SB
    chains = [(kk, h) for kk in range(nblk) for h in range(RET_HEADS)]
    rows = lambda kk: slice(kk * SB, (kk + 1) * SB)
    slab = lambda p: slice(128 * p, 128 * p + 128)

    qh = [jnp.where(_head_lane_mask(h % 2), q[rows(kk), slab(h // 2)], 0.0).astype(BF16) for kk, h in chains]
    att = [(_mm_nt(qh[c], kb[rows(kk), slab(h // 2)]) * dec_ref[h]).astype(BF16) for c, (kk, h) in enumerate(chains)]
    o_intra = [_mm(att[c], vb[rows(kk), slab(h)]) for c, (kk, h) in enumerate(chains)]

    nchunk = T // CHUNK
    crow = lambda b, n: slice(b * T + n * CHUNK, b * T + (n + 1) * CHUNK)
    blk = _pair_block_mask()
    kv = {(b, p, n): jnp.where(blk, _mm_tn(vb[crow(b, n), 256 * p:256 * p + 256], k_zeta[crow(b, n), slab(p)]), 0.0)
          for b in range(nb) for n in range(nchunk) for p in range(2)}
    entering = _pair_state_scan(s_ref, kv, lambda b, p, n: gc_ref[:, slab(p)], nb, nchunk)
    o_inter = {(b, p, n): _mm_nt(q_xi[crow(b, n), slab(p)], entering[b, p, n])
               for b in range(nb) for n in range(nchunk) for p in range(2)}

    inter = jnp.concatenate([jnp.concatenate([o_inter[b, 0, n], o_inter[b, 1, n]], axis=1)
                             for b in range(nb) for n in range(nchunk)], axis=0)
    intra = jnp.concatenate([jnp.concatenate([o_intra[kk * RET_HEADS + h] for h in range(RET_HEADS)], axis=1)
                             for kk in range(nblk)], axis=0)
    o = intra + inter

    parts = []
    for h in range(RET_HEADS):
        oh = o[:, 128 * h:128 * h + 128]
        oc = oh - jnp.mean(oh, axis=-1, keepdims=True)
        parts.append(oc * lax.rsqrt(jnp.mean(oc * oc, axis=-1, keepdims=True) + EPS))
    o = jnp.concatenate(parts, axis=1) * ng_ref[...] * _silu(zg)
    for b in range(nb):
        o_ref[b, rs, :] = o[b * T:(b + 1) * T].astype(o_ref.dtype)


def _gdn_tile(tile, rs, x_ref, g_ref, w_ref, wg_ref, wbp_ref, wba_ref, conv_ref, hp_col_ref, hp_row_ref, ng_ref,
              o_ref, s_ref, carry_ref, *, npad):
    nb, T, SB = x_ref.shape[0], ROW_TILE, SUB_BLOCK
    H = GDN_HEADS
    assert SB == 128
    hn = _stacked_norm_in(x_ref, rs, g_ref, tile * T, npad)
    zqkv = _mm(hn, w_ref[0])
    zg = _mm(hn, wg_ref[0])
    zba = _mm(hn, wbp_ref[...])
    ba_row = _mm_nt(wba_ref[...], hn)

    xc = []
    for b in range(nb):
        zb = zqkv[b * T:(b + 1) * T]
        ext = jnp.concatenate([carry_ref[b], zb], axis=0)
        carry_ref[b] = zb[T - 8:T, :]
        acc = conv_ref[3:4, :] * zb
        for j in range(CONV_K - 1):
            s = CONV_K - 1 - j
            acc = acc + conv_ref[j:j + 1, :] * ext[8 - s:8 - s + T, :]
        xc.append(acc)
    qkv = _silu(jnp.concatenate(xc, axis=0))

    beta_col = _sigmoid(zba)
    g_col = hp_col_ref[0:1, :] * _softplus(zba + hp_col_ref[1:2, :])
    rep = nb * T // hp_row_ref.shape[2]
    g_row = (jnp.concatenate([hp_row_ref[0]] * rep, axis=1)
             * _softplus(ba_row + jnp.concatenate([hp_row_ref[1]] * rep, axis=1)))
    same, low, strict, up = _seg_masks(T, CHUNK)
    low_b, same_b, up_b = low.astype(BF16), same.astype(BF16), up.astype(BF16)
    bat = lambda b: slice(b * T, (b + 1) * T)
    cum_col = jnp.concatenate([_mm01(low_b, g_col[bat(b)]) for b in range(nb)], axis=0)
    last_col = jnp.concatenate([_mm01(same_b, g_col[bat(b)]) for b in range(nb)], axis=0)
    cum_row = jnp.concatenate([_mm01_right(g_row[:, bat(b)], up_b) for b in range(nb)], axis=1)
    low_t, strict_t = low[0:SB, 0:SB], strict[0:SB, 0:SB]
    eye = (lax.broadcasted_iota(jnp.int32, (SB, SB), 0) == lax.broadcasted_iota(jnp.int32, (SB, SB), 1)).astype(F32)

    nsb = T // SB
    chains = [(k, h) for k in range(nb * nsb) for h in range(H)]
    CS = range(len(chains))
    rows = lambda k: slice(k * SB, (k + 1) * SB)
    qn, kn, knb, vh, cum_c, last_c, beta_c, gamma = [], [], [], [], [], [], [], []
    for k, h in chains:
        qh = qkv[rows(k), 128 * h:128 * h + 128]
        kh = qkv[rows(k), 512 + 128 * h:512 + 128 * h + 128]
        vh.append(qkv[rows(k), 1024 + 128 * h:1024 + 128 * h + 128])
        qn.append(qh * lax.rsqrt(jnp.sum(qh * qh, axis=-1, keepdims=True) + EPS) * (GDN_DK ** -0.5))
        kn.append(kh * lax.rsqrt(jnp.sum(kh * kh, axis=-1, keepdims=True) + EPS))
        knb.append(kn[-1].astype(BF16))
        cum_c.append(jnp.broadcast_to(cum_col[rows(k), H + h:H + h + 1], (SB, 128)))
        last_c.append(jnp.broadcast_to(last_col[rows(k), H + h:H + h + 1], (SB, 128)))
        beta_c.append(jnp.broadcast_to(beta_col[rows(k), h:h + 1], (SB, 128)))
        gamma.append(jnp.exp(jnp.where(low_t, cum_c[-1] - cum_row[H + h:H + h + 1, rows(k)], -jnp.inf)))

    kk = [_mm_nt(knb[c], knb[c]) for c in CS]
    qk = [_mm_nt(qn[c].astype(BF16), knb[c]) for c in CS]
    a0 = [jnp.where(strict_t, kk[c] * beta_c[c] * gamma[c], 0.0) for c in CS]
    inv = [eye - a0[c] for c in CS]
    a0b = [a.astype(BF16) for a in a0]
    pw = [_mm(a0b[c], a0b[c]) for c in CS]
    nlev = int(math.log2(CHUNK)) - 1
    for lev in range(nlev):
        pwb = [p.astype(BF16) for p in pw]
        if lev + 1 < nlev:
            both = [_mm(pwb[c], jnp.concatenate([inv[c], pw[c]], axis=1).astype(BF16)) for c in CS]
            inv = [inv[c] + both[c][:, 0:SB] for c in CS]
            pw = [both[c][:, SB:2 * SB] for c in CS]
        else:
            inv = [inv[c] + _mm(pwb[c], inv[c].astype(BF16)) for c in CS]
    e_cum = [jnp.exp(cum_c[c]) for c in CS]
    uw = [_mm(inv[c].astype(BF16),
              jnp.concatenate([vh[c] * beta_c[c], kn[c] * beta_c[c] * e_cum[c]], axis=1).astype(BF16)) for c in CS]
    u = [x[:, 0:128] for x in uw]
    wb = [x[:, 128:256].astype(BF16) for x in uw]
    att = [(qk[c] * gamma[c]).astype(BF16) for c in CS]
    qd = [(qn[c] * e_cum[c]).astype(BF16) for c in CS]
    kd = [(kn[c] * jnp.exp(last_c[c] - cum_c[c])).astype(BF16) for c in CS]
    dl = [jnp.exp(last_c[c]) for c in CS]

    seqs = [(b, h) for b in range(nb) for h in range(H)]
    st = [s_ref[b, h] for b, h in seqs]
    o_rows = [[] for _ in CS]
    for s in range(nsb):
        cs = [(b * nsb + s) * H + h for b, h in seqs]
        v_rows = [[] for _ in seqs]
        for n in range(SB // CHUNK):
            r = slice(n * CHUNK, (n + 1) * CHUNK)
            stb = [x.astype(BF16) for x in st]
            v_new = [u[c][r] - _mm(wb[c][r], stb[i]) for i, c in enumerate(cs)]
            for i, c in enumerate(cs):
                v_rows[i].append(v_new[i])
                rest = [u[c][(n + 1) * CHUNK:]] if (n + 1) * CHUNK < SB else []
                v_all = jnp.concatenate(v_rows[i] + rest, axis=0)
                o_rows[c].append(_mm(qd[c][r], stb[i]) + _mm(att[c][r], v_all.astype(BF16)))
            st = [dl[c][n * CHUNK:n * CHUNK + 1, :] * st[i] + _mm_tn(kd[c][r], v_new[i].astype(BF16))
                  for i, c in enumerate(cs)]
    for i, (b, h) in enumerate(seqs):
        s_ref[b, h] = st[i]
    o = jnp.concatenate([jnp.concatenate([jnp.concatenate(o_rows[k * H + h], axis=0) for h in range(H)], axis=1)
                         for k in range(nb * nsb)], axis=0)
    o = _head_rms(o, GDN_DV) * ng_ref[...] * _silu(zg)
    for b in range(nb):
        o_ref[b, rs, :] = o[b * T:(b + 1) * T].astype(o_ref.dtype)


def _shift_rows(x, s):
    n = x.shape[0]
    if s % 8 == 0:
        return jnp.concatenate([jnp.zeros((s, x.shape[1]), x.dtype), x[:n - s]], axis=0)
    rows = lax.broadcasted_iota(jnp.int32, x.shape, 0)
    return jnp.where(rows >= s, pltpu.roll(x, s, 0), 0.0)


def _s5_kernel(u_ref, tab_ref, mul_ref, d_ref, y_ref, m_ref, wst_ref, wot_ref):
    C, P = SSM_CHUNK, SSM_GROUP
    W = u_ref.shape[2]
    GB = W // P
    rb = u_ref.shape[0]

    @pl.when(pl.program_id(1) == 0)
    def _():
        shift = int(math.log2(P))
        row_g = lax.shift_right_logical(lax.broadcasted_iota(jnp.int32, (W, W), 0), shift)
        lane_g = lax.shift_right_logical(lax.broadcasted_iota(jnp.int32, (W, W), 1), shift)
        tabs = [tab_ref[g] for g in range(GB)]
        i_hi, i_mid, i_lo = _split3(jnp.concatenate([tb[0:16] for tb in tabs], axis=0))
        lag_blocks = []
        for lag in range(C):
            o_hi, o_mid, o_lo = _split3(jnp.concatenate(
                [tb[32:48] * tb[64 + lag:65 + lag] + tb[48:64] * tb[88 + lag:89 + lag] for tb in tabs], axis=0))
            k_lag = (_mm_nt(i_hi, o_hi) + _mm_nt(i_hi, o_mid) + _mm_nt(i_mid, o_hi)
                     + _mm_nt(i_hi, o_lo) + _mm_nt(i_lo, o_hi) + _mm_nt(i_mid, o_mid))
            lag_blocks.append(jnp.where(row_g == lane_g, k_lag, 0.0).astype(BF16))
        zero = jnp.zeros((W, W), BF16)
        for s in range(C):
            for t in range(C):
                m_ref[s * W:(s + 1) * W, t * W:(t + 1) * W] = lag_blocks[t - s] if t >= s else zero
        row_g = lax.shift_right_logical(lax.broadcasted_iota(jnp.int32, (W, 128), 0), shift)
        for s in range(C):
            k_in, k_out = C - 1 - s, s + 1
            w_in_s = jnp.concatenate([tb[0:16] * tb[64 + k_in:65 + k_in] + tb[16:32] * tb[88 + k_in:89 + k_in]
                                      for tb in tabs], axis=0)
            w_out_s = jnp.concatenate([tb[32:48] * tb[64 + k_out:65 + k_out] + tb[48:64] * tb[88 + k_out:89 + k_out]
                                       for tb in tabs], axis=0)
            for g in range(GB):
                wst_ref[s * W:(s + 1) * W, g * 128:(g + 1) * 128] = jnp.where(row_g == g, w_in_s, 0.0).astype(BF16)
                wot_ref[s * W:(s + 1) * W, g * 128:(g + 1) * 128] = jnp.where(row_g == g, w_out_s, 0.0).astype(BF16)

    ut = [u_ref[:, t, :] for t in range(C)]
    v = jnp.concatenate(ut, axis=1).astype(BF16)
    conv = [_mm(v[:, 0:(2 * j + 2) * W], m_ref[0:(2 * j + 2) * W, 2 * j * W:(2 * j + 2) * W]) for j in range(C // 2)]
    z = _mm(v, wst_ref[...])
    xin = []
    for g in range(GB):
        x = z[:, g * 128:(g + 1) * 128]
        row_in_blk = lax.broadcasted_iota(jnp.int32, x.shape, 0) & 7
        for k in range(3):
            s = 1 << k
            xs = jnp.where(row_in_blk >= s, pltpu.roll(x, s, 0), 0.0)
            x = x + xs * mul_ref[g, 2 * k:2 * k + 1, :] + pltpu.roll(xs, 64, 1) * mul_ref[g, 2 * k + 1:2 * k + 2, :]
        xsw = pltpu.roll(x, 64, 1)
        p1, p2 = mul_ref[g, 8:16, :], mul_ref[g, 16:24, :]
        blocks, last, last_sw = [x[0:8]], x[7:8], xsw[7:8]
        for j in range(1, rb // 8):
            xb = x[8 * j:8 * j + 8] + last * p1 + last_sw * p2
            xb_sw = xsw[8 * j:8 * j + 8] + last_sw * p1 - last * p2
            blocks.append(xb)
            last, last_sw = xb[7:8], xb_sw[7:8]
        xin.append(_shift_rows(jnp.concatenate(blocks, axis=0), 1))
    from_state = _mm_nt(jnp.concatenate(xin, axis=1).astype(BF16), wot_ref[...])
    for t in range(C):
        y_ref[:, t, :] = (conv[t // 2][:, (t % 2) * W:(t % 2 + 1) * W] + from_state[:, t * W:(t + 1) * W]
                          + d_ref[...] * ut[t])


def _gelu_tanh(x):
    return 0.5 * x * (1.0 + jnp.tanh(math.sqrt(2.0 / math.pi) * (x + 0.044715 * (x * x * x))))


def _merge_kernel(x_ref, g_ref, b0_ref, b1_ref, y5_ref, b3_ref, wg_ref, wglu_ref, wb_ref, wo_ref,
                  o_ref, *, npad, tiles_per_batch):
    T = x_ref.shape[0]
    x = x_ref[...]
    row0 = lax.rem(pl.program_id(0), tiles_per_batch) * T
    hn = _norm_in(x, g_ref[...], row0, npad)
    ya = _gelu_tanh(y5_ref[...])
    b2 = ya * _sigmoid(_mm(ya.astype(BF16), wglu_ref[...]))
    branches = (b0_ref[...], b1_ref[...], b2.astype(BF16), b3_ref[...])
    dm = x.shape[1]
    merged = None
    for m in range(N_BRANCH):
        gate = _sigmoid(_mm(hn, wg_ref[0, :, m * dm:(m + 1) * dm]))
        term = gate * _mm(branches[m], wb_ref[m])
        merged = term if merged is None else merged + term
    o_ref[...] = x + _mm(merged.astype(BF16), wo_ref[...])


def _ffn_kernel(x_ref, g_ref, wgu_ref, wd_ref, o_ref, *, n_split):
    x = x_ref[...]
    h = (x * lax.rsqrt(jnp.mean(x * x, axis=-1, keepdims=True) + EPS) * g_ref[...]).astype(BF16)
    dff = wd_ref.shape[0]
    cw = dff // n_split
    acts = []
    for c in range(n_split):
        gt = _mm(h, wgu_ref[:, c * cw:(c + 1) * cw])
        up = _mm(h, wgu_ref[:, dff + c * cw:dff + (c + 1) * cw])
        acts.append((_silu(gt) * up).astype(BF16))
    acc = x
    for c in range(n_split):
        acc = acc + _mm(acts[c], wd_ref[c * cw:(c + 1) * cw, :])
    o_ref[...] = acc


def _final_kernel(x_ref, g_ref, o_ref):
    x = x_ref[...]
    o_ref[...] = x * lax.rsqrt(jnp.mean(x * x, axis=-1, keepdims=True) + EPS) * g_ref[...]


def _const_spec(block, index):
    return pl.BlockSpec(block, index, pipeline_mode=pl.Buffered(1))


def _layer_spec(shape, layer):
    zeros = (0,) * len(shape)
    return _const_spec((None,) + tuple(shape), lambda i: (layer,) + zeros)


def _params(sem):
    return pltpu.CompilerParams(dimension_semantics=sem, vmem_limit_bytes=VMEM_LIMIT)


def _s5_constants(a_re, a_im, log_dt, b_re, b_im, c_re, c_im):
    C, G, N, P = SSM_CHUNK, SSM_GROUPS, SSM_STATE, SSM_GROUP
    L = a_re.shape[0]
    dt = jnp.exp(log_dt)[..., None]
    mag = jnp.exp(dt * a_re)
    ar, ai = mag * jnp.cos(dt * a_im), mag * jnp.sin(dt * a_im)
    den = a_re * a_re + a_im * a_im
    nr, ni = ar - 1.0, ai
    cr, ci = (nr * a_re + ni * a_im) / den, (ni * a_re - nr * a_im) / den
    bbr = cr[..., None] * b_re - ci[..., None] * b_im
    bbi = cr[..., None] * b_im + ci[..., None] * b_re
    pr, pi = [jnp.ones_like(ar)], [jnp.zeros_like(ar)]
    for _ in range(C):
        pr, pi = pr + [pr[-1] * ar - pi[-1] * ai], pi + [pr[-1] * ai + pi[-1] * ar]
    pr, pi = jnp.stack(pr), jnp.stack(pi)
    bbr_t, bbi_t = jnp.swapaxes(bbr, 2, 3), jnp.swapaxes(bbi, 2, 3)
    cat = lambda a, b: jnp.concatenate([a, b], axis=-1)
    pw_r = jnp.moveaxis(cat(pr, pr), 0, 2)
    pw_i = jnp.moveaxis(cat(pi, pi), 0, 2)
    gap = jnp.zeros((L, G, 24 - (C + 1), 2 * N), F32)
    tab = jnp.concatenate([cat(bbr_t, bbi_t), cat(-bbi_t, bbr_t), cat(c_re, -c_im), cat(-c_im, -c_re),
                           pw_r, gap, pw_i, gap], axis=2)
    sr, si = pr[C], pi[C]
    lr, li, log_rows = sr, si, []
    for _ in range(3):
        log_rows += [cat(lr, lr), cat(-li, li)]
        lr, li = lr * lr - li * li, 2.0 * lr * li
    qr, qi, p1, p2 = sr, si, [], []
    for _ in range(8):
        p1, p2 = p1 + [cat(qr, qr)], p2 + [cat(-qi, qi)]
        qr, qi = qr * sr - qi * si, qr * si + qi * sr
    zero = jnp.zeros_like(log_rows[0])
    muls = jnp.stack(log_rows + [zero, zero] + p1 + p2, axis=2)
    return tab.astype(F32), muls.astype(F32)


def _retention_tables(lp, npad, tile):
    half = RET_DK // 2
    pos = (jnp.arange(lp) - npad).astype(F32)
    inv = ROPE_BASE ** (-jnp.arange(half, dtype=F32) / half)
    ang = pos[:, None] * inv[None, :]
    cos, sin = jnp.cos(ang), jnp.sin(ang)
    cos_t = jnp.tile(jnp.concatenate([cos, cos], axis=1), (1, RET_HEADS))
    sin_t = jnp.tile(jnp.concatenate([-sin, sin], axis=1), (1, RET_HEADS))
    log_gamma = jnp.log1p(-jnp.exp2(-5.0 - jnp.arange(RET_HEADS, dtype=F32)))
    idx = jnp.arange(tile)
    loc = (idx % CHUNK).astype(F32)
    diff = (idx[:, None] - idx[None, :]).astype(F32)
    ok = ((idx[:, None] // CHUNK) == (idx[None, :] // CHUNK)) & (diff >= 0)
    dec = jnp.where(ok[None], jnp.exp(log_gamma[:, None, None] * jnp.maximum(diff, 0.0)[None]), 0.0)
    xi = jnp.repeat(jnp.exp(log_gamma[None, :] * (loc[:, None] + 1.0)), RET_DK, axis=1)
    zeta = jnp.repeat(jnp.exp(log_gamma[None, :] * (CHUNK - 1.0 - loc[:, None])), RET_DK, axis=1)
    gc = jnp.repeat(jnp.exp(log_gamma * CHUNK), RET_DK)[None, :]
    return cos_t, sin_t, dec, xi, zeta, gc


def kernel(x, meta, norm_mix, w_in, hg_lb_logits, hg_norm, ret_norm, ssm_a_re, ssm_a_im, ssm_log_dt,
           ssm_b_re, ssm_b_im, ssm_c_re, ssm_c_im, ssm_d, ssm_w_glu, gdn_conv, gdn_a_log, gdn_dt_bias,
           gdn_norm, w_branch, w_out, norm_ffn, w_gu, w_down, norm_final):
    bsz, seq, dm = x.shape
    depth = w_in.shape[0]
    T = ROW_TILE
    assert seq % T == 0 and dm % 128 == 0
    npad = (-(N_META + seq)) % T
    lp = npad + N_META + seq
    nt = lp // T
    rows = bsz * lp
    tm = next(t for t in DENSE_TILES if lp % t == 0)
    dff = w_down.shape[1]

    h = jnp.concatenate([jnp.zeros((bsz, npad, dm), F32),
                         jnp.broadcast_to(meta.astype(F32)[None], (bsz, N_META, dm)), x.astype(F32)], axis=1)
    h = h.reshape(rows, dm)

    wb16 = w_in.astype(BF16)
    w_bpad = jnp.pad(wb16[:, :, 5120:5128], ((0, 0), (0, 0), (0, 120)))
    w_ba = jnp.swapaxes(wb16[:, :, 5120:5128], 1, 2)
    w_tail = wb16[:, :, 5128:5640 + N_BRANCH * dm]
    w_glu16, w_br16, w_out16 = ssm_w_glu.astype(BF16), w_branch.astype(BF16), w_out.astype(BF16)
    w_gu16, w_down16 = w_gu.astype(BF16), w_down.astype(BF16)

    lb = jnp.cumsum(jax.nn.softmax(hg_lb_logits.astype(F32), axis=0), axis=0)
    lb = lb - lb[:1]
    lb_rows = jnp.stack([jnp.log(lb), jnp.log1p(-lb), 1.0 - lb] + [jnp.zeros_like(lb)] * 5, axis=1)

    neg_decay = -jnp.exp(gdn_a_log.astype(F32))
    zeros4 = jnp.zeros_like(neg_decay)
    hp_col = jnp.stack([jnp.pad(jnp.concatenate([zeros4, neg_decay], axis=1), ((0, 0), (0, 120))),
                        jnp.pad(jnp.concatenate([zeros4, gdn_dt_bias.astype(F32)], axis=1), ((0, 0), (0, 120)))]
                       + [jnp.zeros((depth, 128), F32)] * 6, axis=1)
    hp_row = jnp.stack([jnp.concatenate([zeros4, neg_decay], axis=1),
                        jnp.concatenate([zeros4, gdn_dt_bias.astype(F32)], axis=1)], axis=1)
    hp_row = jnp.broadcast_to(hp_row[..., None], (depth, 2, 8, T))

    cos_t, sin_t, dec, xi, zeta, gc = _retention_tables(lp, npad, SUB_BLOCK)

    rchunks = rows // SSM_CHUNK
    f32 = lambda a: a.astype(F32)
    s5_tab, s5_muls = _s5_constants(f32(ssm_a_re), f32(ssm_a_im), f32(ssm_log_dt), f32(ssm_b_re),
                                           f32(ssm_b_im), f32(ssm_c_re), f32(ssm_c_im))

    gmix = norm_mix.astype(F32).reshape(depth, 1, dm)
    ni = next(n for n in MIXER_INNER_TILES if nt % n == 0)
    row3 = lambda w: pl.BlockSpec((bsz, ni * T, w), lambda i: (0, i, 0))
    tile_tab = lambda w: pl.BlockSpec((ni * T, w), lambda i: (i, 0))
    fixed = lambda *shape: _const_spec(shape, lambda i: (0,) * len(shape))
    vec = lambda w, l: _layer_spec((1, w), l)
    cols = lambda l, start, width: pl.BlockSpec((pl.Element(1), pl.Element(dm), pl.Element(width)),
                                                lambda i: (l, 0, start), pipeline_mode=pl.Buffered(1))
    seq_params = _params(("arbitrary",))

    for l in range(depth):
        h3 = h.reshape(bsz, lp, dm)
        br0, u = pl.pallas_call(
            functools.partial(_tiled_mixer, _hg_tile, 1, npad=npad, n_inner=ni),
            grid=(nt // ni,),
            in_specs=[row3(dm), vec(dm, l), cols(l, 0, 1536), cols(l, 3072, 512), _layer_spec((8, 256), l),
                      vec(512, l)],
            out_specs=[row3(512), row3(512)],
            out_shape=[jax.ShapeDtypeStruct((bsz, lp, 512), BF16), jax.ShapeDtypeStruct((bsz, lp, 512), F32)],
            scratch_shapes=[pltpu.VMEM((bsz, 2, 256, 128), F32)],
            compiler_params=seq_params, name="hgrn2_mixer",
        )(h3, gmix, wb16, wb16, lb_rows, hg_norm.astype(F32).reshape(depth, 1, 512))

        br1 = pl.pallas_call(
            functools.partial(_tiled_mixer, _ret_tile, 1, npad=npad, n_inner=ni),
            grid=(nt // ni,),
            in_specs=[row3(dm), vec(dm, l), cols(l, 1536, 1536), tile_tab(256), tile_tab(256),
                      fixed(RET_HEADS, SUB_BLOCK, SUB_BLOCK), fixed(SUB_BLOCK, 256), fixed(SUB_BLOCK, 256),
                      fixed(1, 256), vec(512, l)],
            out_specs=row3(512),
            out_shape=jax.ShapeDtypeStruct((bsz, lp, 512), BF16),
            scratch_shapes=[pltpu.VMEM((bsz, 2, 256, 128), F32)],
            compiler_params=seq_params, name="retention_mixer",
        )(h3, gmix, wb16, cos_t, sin_t, dec, xi, zeta, gc, ret_norm.astype(F32).reshape(depth, 1, 512))

        br3 = pl.pallas_call(
            functools.partial(_tiled_mixer, _gdn_tile, 2, npad=npad, n_inner=ni),
            grid=(nt // ni,),
            in_specs=[row3(dm), vec(dm, l), cols(l, 3584, 1536), cols(l, 0, 512), _layer_spec((dm, 128), l),
                      _layer_spec((8, dm), l), _layer_spec((CONV_K, 1536), l), _layer_spec((8, 128), l),
                      _layer_spec((2, 8, T), l), vec(512, l)],
            out_specs=row3(512),
            out_shape=jax.ShapeDtypeStruct((bsz, lp, 512), BF16),
            scratch_shapes=[pltpu.VMEM((bsz, GDN_HEADS, GDN_DK, GDN_DV), F32), pltpu.VMEM((bsz, 8, 1536), F32)],
            compiler_params=seq_params, name="gdn_mixer",
        )(h3, gmix, wb16, w_tail, w_bpad, w_ba, gdn_conv.astype(F32), hp_col, hp_row,
          gdn_norm.astype(F32).reshape(depth, 1, 512))
        br0, br1, br3 = (a.reshape(rows, 512) for a in (br0, br1, br3))

        gb = S5_GROUP_BLOCK
        cb = rchunks // bsz
        gspec = lambda a, b_: pl.BlockSpec((None, gb, a, b_), lambda g, b: (l, g, 0, 0))
        uspec = pl.BlockSpec((cb, SSM_CHUNK, gb * SSM_GROUP), lambda g, b: (b, 0, g))
        cw = SSM_CHUNK * gb * SSM_GROUP
        y5 = pl.pallas_call(
            _s5_kernel,
            grid=(SSM_GROUPS // gb, bsz),
            in_specs=[uspec, gspec(s5_tab.shape[2], 2 * SSM_STATE),
                      gspec(s5_muls.shape[2], 2 * SSM_STATE),
                      pl.BlockSpec((None, 1, gb * SSM_GROUP), lambda g, b: (l, 0, g))],
            out_specs=uspec,
            out_shape=jax.ShapeDtypeStruct((rchunks, SSM_CHUNK, SSM_WIDTH), F32),
            scratch_shapes=[pltpu.VMEM((cw, cw), BF16), pltpu.VMEM((cw, gb * 2 * SSM_STATE), BF16),
                            pltpu.VMEM((cw, gb * 2 * SSM_STATE), BF16)],
            compiler_params=_params(("arbitrary", "arbitrary")), name="s5_mixer",
        )(u.reshape(rchunks, SSM_CHUNK, SSM_WIDTH), s5_tab, s5_muls,
          ssm_d.astype(F32).reshape(depth, 1, SSM_WIDTH))
        y5 = y5.reshape(rows, SSM_WIDTH)

        drow = lambda w: pl.BlockSpec((tm, w), lambda i: (i, 0))
        h = pl.pallas_call(
            functools.partial(_merge_kernel, npad=npad, tiles_per_batch=lp // tm),
            grid=(rows // tm,),
            in_specs=[drow(dm), vec(dm, l), drow(512), drow(512), drow(512), drow(512),
                      cols(l, 512, N_BRANCH * dm), _layer_spec((512, 512), l),
                      _layer_spec((N_BRANCH, BRANCH_WIDTH, dm), l), _layer_spec((dm, dm), l)],
            out_specs=drow(dm),
            out_shape=jax.ShapeDtypeStruct((rows, dm), F32),
            compiler_params=seq_params, name="merge_out",
        )(h, gmix, br0, br1, y5, br3, w_tail, w_glu16, w_br16, w_out16)

        h = pl.pallas_call(
            functools.partial(_ffn_kernel, n_split=FFN_SPLITS),
            grid=(rows // tm,),
            in_specs=[drow(dm), vec(dm, l), _layer_spec((dm, 2 * dff), l), _layer_spec((dff, dm), l)],
            out_specs=drow(dm),
            out_shape=jax.ShapeDtypeStruct((rows, dm), F32),
            compiler_params=seq_params, name="swiglu_ffn",
        )(h, norm_ffn.astype(F32).reshape(depth, 1, dm), w_gu16, w_down16)

    tf = next(t for t in FINAL_TILES if seq % t == 0)
    ns = seq // tf
    out = pl.pallas_call(
        _final_kernel,
        grid=(bsz, ns),
        in_specs=[pl.BlockSpec((pl.Element(tf), pl.Element(dm)),
                               lambda b, i: ((b * (lp // 8) + (npad + N_META) // 8 + i * (tf // 8)) * 8, 0)),
                  _const_spec((1, dm), lambda b, i: (0, 0))],
        out_specs=pl.BlockSpec((tf, dm), lambda b, i: (b * ns + i, 0)),
        out_shape=jax.ShapeDtypeStruct((bsz * seq, dm), x.dtype),
        compiler_params=_params(("arbitrary", "arbitrary")), name="final_norm",
    )(h, norm_final.astype(F32).reshape(1, dm))
    return out.reshape(bsz, seq, dm)
```

```python
import functools
import math

import jax
import jax.numpy as jnp
from jax import lax
from jax.experimental import pallas as pl
from jax.experimental.pallas import tpu as pltpu

F32 = jnp.float32
BF16 = jnp.bfloat16

N_META = 16
EPS = 1e-6
HG_HEADS, HG_DK, HG_DV, HG_CHUNK = 4, 64, 128, 16
RET_HEADS, RET_DK, RET_DV = 4, 64, 128
ROPE_BASE = 10000.0
SSM_WIDTH, SSM_GROUP, SSM_STATE = 512, 16, 64
SSM_GROUPS = SSM_WIDTH // SSM_GROUP
SSM_CHUNK = 16
S5_GROUP_BLOCK = 8
GDN_HEADS, GDN_DK, GDN_DV = 4, 128, 128
CONV_K = 4
CHUNK = 64
N_BRANCH = 4
BRANCH_WIDTH = 512

ROW_TILE = 256
SUB_BLOCK = 128
MIXER_INNER_TILES = (5, 4, 3, 2, 1)
DENSE_TILES = (640, 384, 320, 128)
MERGE_TILES = (768, 640, 384, 320, 128)
FINAL_TILES = (1024, 512, 128)
FFN_SPLITS = 1
VMEM_LIMIT = 56 * 1024 * 1024

NT_DIMS = (((1,), (1,)), ((), ()))
TN_DIMS = (((0,), (0,)), ((), ()))


def _mm(a, b):
    return jnp.dot(a, b, preferred_element_type=F32)


def _mm_nt(a, b):
    return lax.dot_general(a, b, NT_DIMS, preferred_element_type=F32)


def _mm_tn(a, b):
    return lax.dot_general(a, b, TN_DIMS, preferred_element_type=F32)


def _split3(x):
    hi = x.astype(BF16)
    r = x - hi.astype(F32)
    mid = r.astype(BF16)
    lo = (r - mid.astype(F32)).astype(BF16)
    return hi, mid, lo


def _mm01(m01, x):
    hi, mid, lo = _split3(x)
    return _mm(m01, hi) + _mm(m01, mid) + _mm(m01, lo)


def _mm01_right(x, m01):
    hi, mid, lo = _split3(x)
    return _mm(hi, m01) + _mm(mid, m01) + _mm(lo, m01)


def _seg_masks(n, chunk):
    sh = int(math.log2(chunk))
    r = lax.broadcasted_iota(jnp.int32, (n, n), 0)
    c = lax.broadcasted_iota(jnp.int32, (n, n), 1)
    same = lax.shift_right_logical(r, sh) == lax.shift_right_logical(c, sh)
    return same, same & (c <= r), same & (c < r), same & (c >= r)


def _sigmoid(x):
    return jax.nn.sigmoid(x)


def _silu(x):
    return x * jax.nn.sigmoid(x)


def _softplus(x):
    return jnp.maximum(x, 0.0) + jnp.log1p(jnp.exp(-jnp.abs(x)))


def _norm_in(x, g, row0, npad):
    y = x * lax.rsqrt(jnp.mean(x * x, axis=-1, keepdims=True) + EPS) * g
    rows = row0 + lax.broadcasted_iota(jnp.int32, y.shape, 0)
    return jnp.where(rows >= npad, y, 0.0).astype(BF16)


def _stacked_norm_in(x_ref, rs, g_ref, row0, npad):
    return jnp.concatenate([_norm_in(x_ref[b, rs, :], g_ref[...], row0, npad) for b in range(x_ref.shape[0])], axis=0)


def _tiled_mixer(tile_fn, n_scratch, *refs, npad, n_inner):
    i = pl.program_id(0)

    @pl.when(i == 0)
    def _():
        for r in refs[len(refs) - n_scratch:]:
            r[...] = jnp.zeros_like(r)

    def step(j, carry):
        tile_fn(i * n_inner + j, pl.ds(pl.multiple_of(j * ROW_TILE, ROW_TILE), ROW_TILE), *refs, npad=npad)
        return carry

    lax.fori_loop(0, n_inner, step, 0)


def _head_rms(o, width):
    parts = []
    for h in range(o.shape[1] // width):
        oh = o[:, h * width:(h + 1) * width]
        parts.append(oh * lax.rsqrt(jnp.mean(oh * oh, axis=-1, keepdims=True) + EPS))
    return jnp.concatenate(parts, axis=1)


def _pair_block_mask():
    r = lax.broadcasted_iota(jnp.int32, (256, 128), 0)
    c = lax.broadcasted_iota(jnp.int32, (256, 128), 1)
    return lax.shift_right_logical(r, 7) == lax.shift_right_logical(c, 6)


def _head_lane_mask(a):
    c = lax.broadcasted_iota(jnp.int32, (1, 128), 1)
    return lax.shift_right_logical(c, 6) == a


def _pair_state_scan(s_ref, kv, decay_row, nb, nchunk):
    entering = {}
    for b in range(nb):
        for p in range(2):
            st = s_ref[b, p]
            for n in range(nchunk):
                entering[b, p, n] = st.astype(BF16)
                st = decay_row(b, p, n) * st + kv[b, p, n]
            s_ref[b, p] = st
    return entering


def _hg_tile(tile, rs, x_ref, g_ref, w_ref, wu_ref, lb_ref, ng_ref, o_ref, u_ref, s_ref, *, npad):
    nb, T, SB = x_ref.shape[0], ROW_TILE, SUB_BLOCK
    hn = _stacked_norm_in(x_ref, rs, g_ref, tile * T, npad)
    z = _mm(hn, w_ref[0])
    zq, zf, v, zg = z[:, 0:256], z[:, 256:512], z[:, 512:1024], z[:, 1024:1536]
    u = _mm(hn, wu_ref[0])
    for b in range(nb):
        u_ref[b, rs, :] = u[b * T:(b + 1) * T]

    log_lb, log1m_lb, one_m_lb = lb_ref[0:1, :], lb_ref[1:2, :], lb_ref[2:3, :]
    e = jnp.exp(-jnp.abs(zf))
    inv1pe = 1.0 / (1.0 + e)
    log_sig = jnp.minimum(zf, 0.0) - jnp.log1p(e)
    y2 = log1m_lb + log_sig
    log_f = jnp.maximum(log_lb, y2) + jnp.log1p(jnp.exp(-jnp.abs(log_lb - y2)))
    k = one_m_lb * jnp.where(zf >= 0.0, e * inv1pe, inv1pe)
    q = _silu(zq) * (HG_DK ** -0.5)

    same, low, _, _ = _seg_masks(T, HG_CHUNK)
    per_batch = lambda m01: jnp.concatenate([_mm01(m01, log_f[b * T:(b + 1) * T]) for b in range(nb)], axis=0)
    b_cum = per_batch(low.astype(BF16))
    b_last = per_batch(same.astype(BF16))
    q_t = q * jnp.exp(b_cum)
    k_t = (k * jnp.exp(-b_cum)).astype(BF16)
    k_end = (k * jnp.exp(b_last - b_cum)).astype(BF16)
    d_end = jnp.exp(b_last)
    q_tb = q_t.astype(BF16)
    vb = v.astype(BF16)
    low_t = low[0:SB, 0:SB]

    nblk = nb * T // SB
    chains = [(k, h) for k in range(nblk) for h in range(HG_HEADS)]
    rows = lambda k: slice(k * SB, (k + 1) * SB)
    slab = lambda p: slice(128 * p, 128 * p + 128)

    qh = [jnp.where(_head_lane_mask(h % 2), q_t[rows(k), slab(h // 2)], 0.0).astype(BF16) for k, h in chains]
    att = [jnp.where(low_t, _mm_nt(qh[c], k_t[rows(k), slab(h // 2)]), 0.0).astype(BF16)
           for c, (k, h) in enumerate(chains)]
    o_intra = [_mm(att[c], vb[rows(k), slab(h)]) for c, (k, h) in enumerate(chains)]

    nchunk = T // HG_CHUNK
    crow = lambda b, n: slice(b * T + n * HG_CHUNK, b * T + (n + 1) * HG_CHUNK)
    blk = _pair_block_mask()
    kv = {(b, p, n): jnp.where(blk, _mm_tn(vb[crow(b, n), 256 * p:256 * p + 256], k_end[crow(b, n), slab(p)]), 0.0)
          for b in range(nb) for n in range(nchunk) for p in range(2)}
    entering = _pair_state_scan(
        s_ref, kv, lambda b, p, n: d_end[b * T + n * HG_CHUNK:b * T + n * HG_CHUNK + 1, slab(p)], nb, nchunk)
    o_inter = {(b, p, n): _mm_nt(q_tb[crow(b, n), slab(p)], entering[b, p, n])
               for b in range(nb) for n in range(nchunk) for p in range(2)}

    inter = jnp.concatenate([jnp.concatenate([o_inter[b, 0, n], o_inter[b, 1, n]], axis=1)
                             for b in range(nb) for n in range(nchunk)], axis=0)
    intra = jnp.concatenate([jnp.concatenate([o_intra[k * HG_HEADS + h] for h in range(HG_HEADS)], axis=1)
                             for k in range(nblk)], axis=0)
    o = _head_rms(intra + inter, HG_DV) * ng_ref[...] * _silu(zg)
    for b in range(nb):
        o_ref[b, rs, :] = o[b * T:(b + 1) * T].astype(o_ref.dtype)


def _rot_half_swap(x):
    lane = lax.broadcasted_iota(jnp.int32, x.shape, 1)
    first_half = (lane & 63) < 32
    return jnp.where(first_half, pltpu.roll(x, x.shape[1] - 32, 1), pltpu.roll(x, 32, 1))


def _ret_tile(tile, rs, x_ref, g_ref, w_ref, cos_ref, sin_ref, dec_ref, xi_ref, zeta_ref, gc_ref, ng_ref,
              o_ref, s_ref, *, npad):
    nb, T, SB = x_ref.shape[0], ROW_TILE, SUB_BLOCK
    hn = _stacked_norm_in(x_ref, rs, g_ref, tile * T, npad)
    z = _mm(hn, w_ref[0])
    zq, zk, v, zg = z[:, 0:256], z[:, 256:512], z[:, 512:1024], z[:, 1024:1536]
    tile_rows = lambda a: jnp.concatenate([a] * (nb * T // a.shape[0]), axis=0)
    cos, sin = tile_rows(cos_ref[rs, :]), tile_rows(sin_ref[rs, :])
    q = zq * cos + _rot_half_swap(zq) * sin
    k = (zk * cos + _rot_half_swap(zk) * sin) * (RET_DK ** -0.5)
    kb, vb = k.astype(BF16), v.astype(BF16)
    q_xi = (q * tile_rows(xi_ref[...])).astype(BF16)
    k_zeta = (k * tile_rows(zeta_ref[...])).astype(BF16)

    nblk = nb * T // SB
    chains = [(kk, h) for kk in range(nblk) for h in range(RET_HEADS)]
    rows = lambda kk: slice(kk * SB, (kk + 1) * SB)
    slab = lambda p: slice(128 * p, 128 * p + 128)

    qh = [jnp.where(_head_lane_mask(h % 2), q[rows(kk), slab(h // 2)], 0.0).astype(BF16) for kk, h in chains]
    att = [(_mm_nt(qh[c], kb[rows(kk), slab(h // 2)]) * dec_ref[h]).astype(BF16) for c, (kk, h) in enumerate(chains)]
    o_intra = [_mm(att[c], vb[rows(kk), slab(h)]) for c, (kk, h) in enumerate(chains)]

    nchunk = T // CHUNK
    crow = lambda b, n: slice(b * T + n * CHUNK, b * T + (n + 1) * CHUNK)
    blk = _pair_block_mask()
    kv = {(b, p, n): jnp.where(blk, _mm_tn(vb[crow(b, n), 256 * p:256 * p + 256], k_zeta[crow(b, n), slab(p)]), 0.0)
          for b in range(nb) for n in range(nchunk) for p in range(2)}
    entering = _pair_state_scan(s_ref, kv, lambda b, p, n: gc_ref[:, slab(p)], nb, nchunk)
    o_inter = {(b, p, n): _mm_nt(q_xi[crow(b, n), slab(p)], entering[b, p, n])
               for b in range(nb) for n in range(nchunk) for p in range(2)}

    inter = jnp.concatenate([jnp.concatenate([o_inter[b, 0, n], o_inter[b, 1, n]], axis=1)
                             for b in range(nb) for n in range(nchunk)], axis=0)
    intra = jnp.concatenate([jnp.concatenate([o_intra[kk * RET_HEADS + h] for h in range(RET_HEADS)], axis=1)
                             for kk in range(nblk)], axis=0)
    o = intra + inter

    parts = []
    for h in range(RET_HEADS):
        oh = o[:, 128 * h:128 * h + 128]
        oc = oh - jnp.mean(oh, axis=-1, keepdims=True)
        parts.append(oc * lax.rsqrt(jnp.mean(oc * oc, axis=-1, keepdims=True) + EPS))
    o = jnp.concatenate(parts, axis=1) * ng_ref[...] * _silu(zg)
    for b in range(nb):
        o_ref[b, rs, :] = o[b * T:(b + 1) * T].astype(o_ref.dtype)


def _gdn_tile(tile, rs, x_ref, g_ref, w_ref, wg_ref, wbp_ref, wba_ref, conv_ref, hp_col_ref, hp_row_ref, ng_ref,
              o_ref, s_ref, carry_ref, *, npad):
    nb, T, SB = x_ref.shape[0], ROW_TILE, SUB_BLOCK
    H = GDN_HEADS
    assert SB == 128
    hn = _stacked_norm_in(x_ref, rs, g_ref, tile * T, npad)
    zqkv = _mm(hn, w_ref[0])
    zg = _mm(hn, wg_ref[0])
    zba = _mm(hn, wbp_ref[...])
    ba_row = _mm_nt(wba_ref[...], hn)

    xc = []
    for b in range(nb):
        zb = zqkv[b * T:(b + 1) * T]
        ext = jnp.concatenate([carry_ref[b], zb], axis=0)
        carry_ref[b] = zb[T - 8:T, :]
        acc = conv_ref[3:4, :] * zb
        for j in range(CONV_K - 1):
            s = CONV_K - 1 - j
            acc = acc + conv_ref[j:j + 1, :] * ext[8 - s:8 - s + T, :]
        xc.append(acc)
    qkv = _silu(jnp.concatenate(xc, axis=0))

    beta_col = _sigmoid(zba)
    g_col = hp_col_ref[0:1, :] * _softplus(zba + hp_col_ref[1:2, :])
    rep = nb * T // hp_row_ref.shape[2]
    g_row = (jnp.concatenate([hp_row_ref[0]] * rep, axis=1)
             * _softplus(ba_row + jnp.concatenate([hp_row_ref[1]] * rep, axis=1)))
    same, low, strict, up = _seg_masks(T, CHUNK)
    low_b, same_b, up_b = low.astype(BF16), same.astype(BF16), up.astype(BF16)
    bat = lambda b: slice(b * T, (b + 1) * T)
    cum_col = jnp.concatenate([_mm01(low_b, g_col[bat(b)]) for b in range(nb)], axis=0)
    last_col = jnp.concatenate([_mm01(same_b, g_col[bat(b)]) for b in range(nb)], axis=0)
    cum_row = jnp.concatenate([_mm01_right(g_row[:, bat(b)], up_b) for b in range(nb)], axis=1)
    low_t, strict_t = low[0:SB, 0:SB], strict[0:SB, 0:SB]
    eye = (lax.broadcasted_iota(jnp.int32, (SB, SB), 0) == lax.broadcasted_iota(jnp.int32, (SB, SB), 1)).astype(F32)

    nsb = T // SB
    chains = [(k, h) for k in range(nb * nsb) for h in range(H)]
    CS = range(len(chains))
    rows = lambda k: slice(k * SB, (k + 1) * SB)
    qn, kn, knb, vh, cum_c, last_c, beta_c, gamma = [], [], [], [], [], [], [], []
    for k, h in chains:
        qh = qkv[rows(k), 128 * h:128 * h + 128]
        kh = qkv[rows(k), 512 + 128 * h:512 + 128 * h + 128]
        vh.append(qkv[rows(k), 1024 + 128 * h:1024 + 128 * h + 128])
        qn.append(qh * lax.rsqrt(jnp.sum(qh * qh, axis=-1, keepdims=True) + EPS) * (GDN_DK ** -0.5))
        kn.append(kh * lax.rsqrt(jnp.sum(kh * kh, axis=-1, keepdims=True) + EPS))
        knb.append(kn[-1].astype(BF16))
        cum_c.append(jnp.broadcast_to(cum_col[rows(k), H + h:H + h + 1], (SB, 128)))
        last_c.append(jnp.broadcast_to(last_col[rows(k), H + h:H + h + 1], (SB, 128)))
        beta_c.append(jnp.broadcast_to(beta_col[rows(k), h:h + 1], (SB, 128)))
        gamma.append(jnp.exp(jnp.where(low_t, cum_c[-1] - cum_row[H + h:H + h + 1, rows(k)], -jnp.inf)))

    kk = [_mm_nt(knb[c], knb[c]) for c in CS]
    qk = [_mm_nt(qn[c].astype(BF16), knb[c]) for c in CS]
    a0 = [jnp.where(strict_t, kk[c] * beta_c[c] * gamma[c], 0.0) for c in CS]
    inv = [eye - a0[c] for c in CS]
    a0b = [a.astype(BF16) for a in a0]
    pw = [_mm(a0b[c], a0b[c]) for c in CS]
    nlev = int(math.log2(CHUNK)) - 1
    for lev in range(nlev):
        pwb = [p.astype(BF16) for p in pw]
        if lev + 1 < nlev:
            both = [_mm(pwb[c], jnp.concatenate([inv[c], pw[c]], axis=1).astype(BF16)) for c in CS]
            inv = [inv[c] + both[c][:, 0:SB] for c in CS]
            pw = [both[c][:, SB:2 * SB] for c in CS]
        else:
            inv = [inv[c] + _mm(pwb[c], inv[c].astype(BF16)) for c in CS]
    e_cum = [jnp.exp(cum_c[c]) for c in CS]
    uw = [_mm(inv[c].astype(BF16),
              jnp.concatenate([vh[c] * beta_c[c], kn[c] * beta_c[c] * e_cum[c]], axis=1).astype(BF16)) for c in CS]
    u = [x[:, 0:128] for x in uw]
    wb = [x[:, 128:256].astype(BF16) for x in uw]
    att = [(qk[c] * gamma[c]).astype(BF16) for c in CS]
    qd = [(qn[c] * e_cum[c]).astype(BF16) for c in CS]
    kd = [(kn[c] * jnp.exp(last_c[c] - cum_c[c])).astype(BF16) for c in CS]
    dl = [jnp.exp(last_c[c]) for c in CS]

    seqs = [(b, h) for b in range(nb) for h in range(H)]
    st = [s_ref[b, h] for b, h in seqs]
    o_rows = [[] for _ in CS]
    for s in range(nsb):
        cs = [(b * nsb + s) * H + h for b, h in seqs]
        v_rows = [[] for _ in seqs]
        for n in range(SB // CHUNK):
            r = slice(n * CHUNK, (n + 1) * CHUNK)
            stb = [x.astype(BF16) for x in st]
            v_new = [u[c][r] - _mm(wb[c][r], stb[i]) for i, c in enumerate(cs)]
            for i, c in enumerate(cs):
                v_rows[i].append(v_new[i])
                rest = [u[c][(n + 1) * CHUNK:]] if (n + 1) * CHUNK < SB else []
                v_all = jnp.concatenate(v_rows[i] + rest, axis=0)
                o_rows[c].append(_mm(qd[c][r], stb[i]) + _mm(att[c][r], v_all.astype(BF16)))
            st = [dl[c][n * CHUNK:n * CHUNK + 1, :] * st[i] + _mm_tn(kd[c][r], v_new[i].astype(BF16))
                  for i, c in enumerate(cs)]
    for i, (b, h) in enumerate(seqs):
        s_ref[b, h] = st[i]
    o = jnp.concatenate([jnp.concatenate([jnp.concatenate(o_rows[k * H + h], axis=0) for h in range(H)], axis=1)
                         for k in range(nb * nsb)], axis=0)
    o = _head_rms(o, GDN_DV) * ng_ref[...] * _silu(zg)
    for b in range(nb):
        o_ref[b, rs, :] = o[b * T:(b + 1) * T].astype(o_ref.dtype)


def _shift_rows(x, s):
    n = x.shape[0]
    if s % 8 == 0:
        return jnp.concatenate([jnp.zeros((s, x.shape[1]), x.dtype), x[:n - s]], axis=0)
    rows = lax.broadcasted_iota(jnp.int32, x.shape, 0)
    return jnp.where(rows >= s, pltpu.roll(x, s, 0), 0.0)


def _s5_kernel(u_ref, tab_ref, mul_ref, d_ref, y_ref, m_ref, wst_ref, wot_ref):
    C, P = SSM_CHUNK, SSM_GROUP
    W = u_ref.shape[2]
    GB = W // P
    rb = u_ref.shape[0]

    @pl.when(pl.program_id(1) == 0)
    def _():
        shift = int(math.log2(P))
        row_g = lax.shift_right_logical(lax.broadcasted_iota(jnp.int32, (W, W), 0), shift)
        lane_g = lax.shift_right_logical(lax.broadcasted_iota(jnp.int32, (W, W), 1), shift)
        tabs = [tab_ref[g] for g in range(GB)]
        i_hi, i_mid, i_lo = _split3(jnp.concatenate([tb[0:16] for tb in tabs], axis=0))
        lag_blocks = []
        for lag in range(C):
            o_hi, o_mid, o_lo = _split3(jnp.concatenate(
                [tb[32:48] * tb[64 + lag:65 + lag] + tb[48:64] * tb[88 + lag:89 + lag] for tb in tabs], axis=0))
            k_lag = (_mm_nt(i_hi, o_hi) + _mm_nt(i_hi, o_mid) + _mm_nt(i_mid, o_hi)
                     + _mm_nt(i_hi, o_lo) + _mm_nt(i_lo, o_hi) + _mm_nt(i_mid, o_mid))
            lag_blocks.append(jnp.where(row_g == lane_g, k_lag, 0.0).astype(BF16))
        zero = jnp.zeros((W, W), BF16)
        for s in range(C):
            for t in range(C):
                m_ref[s * W:(s + 1) * W, t * W:(t + 1) * W] = lag_blocks[t - s] if t >= s else zero
        row_g = lax.shift_right_logical(lax.broadcasted_iota(jnp.int32, (W, 128), 0), shift)
        for s in range(C):
            k_in, k_out = C - 1 - s, s + 1
            w_in_s = jnp.concatenate([tb[0:16] * tb[64 + k_in:65 + k_in] + tb[16:32] * tb[88 + k_in:89 + k_in]
                                      for tb in tabs], axis=0)
            w_out_s = jnp.concatenate([tb[32:48] * tb[64 + k_out:65 + k_out] + tb[48:64] * tb[88 + k_out:89 + k_out]
                                       for tb in tabs], axis=0)
            for g in range(GB):
                wst_ref[s * W:(s + 1) * W, g * 128:(g + 1) * 128] = jnp.where(row_g == g, w_in_s, 0.0).astype(BF16)
                wot_ref[s * W:(s + 1) * W, g * 128:(g + 1) * 128] = jnp.where(row_g == g, w_out_s, 0.0).astype(BF16)

    ut = [u_ref[:, t, :] for t in range(C)]
    v = jnp.concatenate(ut, axis=1).astype(BF16)
    conv = [_mm(v[:, 0:(2 * j + 2) * W], m_ref[0:(2 * j + 2) * W, 2 * j * W:(2 * j + 2) * W]) for j in range(C // 2)]
    z = _mm(v, wst_ref[...])
    xin = []
    for g in range(GB):
        x = z[:, g * 128:(g + 1) * 128]
        row_in_blk = lax.broadcasted_iota(jnp.int32, x.shape, 0) & 7
        for k in range(3):
            s = 1 << k
            xs = jnp.where(row_in_blk >= s, pltpu.roll(x, s, 0), 0.0)
            x = x + xs * mul_ref[g, 2 * k:2 * k + 1, :] + pltpu.roll(xs, 64, 1) * mul_ref[g, 2 * k + 1:2 * k + 2, :]
        xsw = pltpu.roll(x, 64, 1)
        p1, p2 = mul_ref[g, 8:16, :], mul_ref[g, 16:24, :]
        blocks, last, last_sw = [x[0:8]], x[7:8], xsw[7:8]
        for j in range(1, rb // 8):
            xb = x[8 * j:8 * j + 8] + last * p1 + last_sw * p2
            xb_sw = xsw[8 * j:8 * j + 8] + last_sw * p1 - last * p2
            blocks.append(xb)
            last, last_sw = xb[7:8], xb_sw[7:8]
        xin.append(_shift_rows(jnp.concatenate(blocks, axis=0), 1))
    from_state = _mm_nt(jnp.concatenate(xin, axis=1).astype(BF16), wot_ref[...])
    for t in range(C):
        y_ref[:, t, :] = (conv[t // 2][:, (t % 2) * W:(t % 2 + 1) * W] + from_state[:, t * W:(t + 1) * W]
                          + d_ref[...] * ut[t])


def _gelu_tanh(x):
    return 0.5 * x * (1.0 + jnp.tanh(math.sqrt(2.0 / math.pi) * (x + 0.044715 * (x * x * x))))


def _merge_kernel(x_ref, g_ref, b0_ref, b1_ref, y5_ref, b3_ref, wg_ref, wglu_ref, wb_ref, wo_ref,
                  o_ref, *, npad, tiles_per_batch):
    T = x_ref.shape[0]
    x = x_ref[...]
    row0 = lax.rem(pl.program_id(0), tiles_per_batch) * T
    hn = _norm_in(x, g_ref[...], row0, npad)
    ya = _gelu_tanh(y5_ref[...])
    b2 = ya * _sigmoid(_mm(ya.astype(BF16), wglu_ref[...]))
    branches = (b0_ref[...], b1_ref[...], b2.astype(BF16), b3_ref[...])
    dm = x.shape[1]
    merged = None
    for m in range(N_BRANCH):
        gate = _sigmoid(_mm(hn, wg_ref[0, :, m * dm:(m + 1) * dm]))
        term = gate * _mm(branches[m], wb_ref[m])
        merged = term if merged is None else merged + term
    o_ref[...] = x + _mm(merged.astype(BF16), wo_ref[...])


def _ffn_kernel(x_ref, g_ref, wgu_ref, wd_ref, o_ref, *, n_split):
    x = x_ref[...]
    h = (x * lax.rsqrt(jnp.mean(x * x, axis=-1, keepdims=True) + EPS) * g_ref[...]).astype(BF16)
    dff = wd_ref.shape[0]
    cw = dff // n_split
    acts = []
    for c in range(n_split):
        gt = _mm(h, wgu_ref[:, c * cw:(c + 1) * cw])
        up = _mm(h, wgu_ref[:, dff + c * cw:dff + (c + 1) * cw])
        acts.append((_silu(gt) * up).astype(BF16))
    acc = x
    for c in range(n_split):
        acc = acc + _mm(acts[c], wd_ref[c * cw:(c + 1) * cw, :])
    o_ref[...] = acc


def _final_kernel(x_ref, g_ref, o_ref):
    x = x_ref[...]
    o_ref[...] = x * lax.rsqrt(jnp.mean(x * x, axis=-1, keepdims=True) + EPS) * g_ref[...]


def _const_spec(block, index):
    return pl.BlockSpec(block, index, pipeline_mode=pl.Buffered(1))


def _layer_spec(shape, layer):
    zeros = (0,) * len(shape)
    return _const_spec((None,) + tuple(shape), lambda i: (layer,) + zeros)


def _params(sem):
    return pltpu.CompilerParams(dimension_semantics=sem, vmem_limit_bytes=VMEM_LIMIT)


def _s5_constants(a_re, a_im, log_dt, b_re, b_im, c_re, c_im):
    C, G, N, P = SSM_CHUNK, SSM_GROUPS, SSM_STATE, SSM_GROUP
    L = a_re.shape[0]
    dt = jnp.exp(log_dt)[..., None]
    mag = jnp.exp(dt * a_re)
    ar, ai = mag * jnp.cos(dt * a_im), mag * jnp.sin(dt * a_im)
    den = a_re * a_re + a_im * a_im
    nr, ni = ar - 1.0, ai
    cr, ci = (nr * a_re + ni * a_im) / den, (ni * a_re - nr * a_im) / den
    bbr = cr[..., None] * b_re - ci[..., None] * b_im
    bbi = cr[..., None] * b_im + ci[..., None] * b_re
    pr, pi = [jnp.ones_like(ar)], [jnp.zeros_like(ar)]
    for _ in range(C):
        pr, pi = pr + [pr[-1] * ar - pi[-1] * ai], pi + [pr[-1] * ai + pi[-1] * ar]
    pr, pi = jnp.stack(pr), jnp.stack(pi)
    bbr_t, bbi_t = jnp.swapaxes(bbr, 2, 3), jnp.swapaxes(bbi, 2, 3)
    cat = lambda a, b: jnp.concatenate([a, b], axis=-1)
    pw_r = jnp.moveaxis(cat(pr, pr), 0, 2)
    pw_i = jnp.moveaxis(cat(pi, pi), 0, 2)
    gap = jnp.zeros((L, G, 24 - (C + 1), 2 * N), F32)
    tab = jnp.concatenate([cat(bbr_t, bbi_t), cat(-bbi_t, bbr_t), cat(c_re, -c_im), cat(-c_im, -c_re),
                           pw_r, gap, pw_i, gap], axis=2)
    sr, si = pr[C], pi[C]
    lr, li, log_rows = sr, si, []
    for _ in range(3):
        log_rows += [cat(lr, lr), cat(-li, li)]
        lr, li = lr * lr - li * li, 2.0 * lr * li
    qr, qi, p1, p2 = sr, si, [], []
    for _ in range(8):
        p1, p2 = p1 + [cat(qr, qr)], p2 + [cat(-qi, qi)]
        qr, qi = qr * sr - qi * si, qr * si + qi * sr
    zero = jnp.zeros_like(log_rows[0])
    muls = jnp.stack(log_rows + [zero, zero] + p1 + p2, axis=2)
    return tab.astype(F32), muls.astype(F32)


def _retention_tables(lp, npad, tile):
    half = RET_DK // 2
    pos = (jnp.arange(lp) - npad).astype(F32)
    inv = ROPE_BASE ** (-jnp.arange(half, dtype=F32) / half)
    ang = pos[:, None] * inv[None, :]
    cos, sin = jnp.cos(ang), jnp.sin(ang)
    cos_t = jnp.tile(jnp.concatenate([cos, cos], axis=1), (1, RET_HEADS))
    sin_t = jnp.tile(jnp.concatenate([-sin, sin], axis=1), (1, RET_HEADS))
    log_gamma = jnp.log1p(-jnp.exp2(-5.0 - jnp.arange(RET_HEADS, dtype=F32)))
    idx = jnp.arange(tile)
    loc = (idx % CHUNK).astype(F32)
    diff = (idx[:, None] - idx[None, :]).astype(F32)
    ok = ((idx[:, None] // CHUNK) == (idx[None, :] // CHUNK)) & (diff >= 0)
    dec = jnp.where(ok[None], jnp.exp(log_gamma[:, None, None] * jnp.maximum(diff, 0.0)[None]), 0.0)
    xi = jnp.repeat(jnp.exp(log_gamma[None, :] * (loc[:, None] + 1.0)), RET_DK, axis=1)
    zeta = jnp.repeat(jnp.exp(log_gamma[None, :] * (CHUNK - 1.0 - loc[:, None])), RET_DK, axis=1)
    gc = jnp.repeat(jnp.exp(log_gamma * CHUNK), RET_DK)[None, :]
    return cos_t, sin_t, dec, xi, zeta, gc


def kernel(x, meta, norm_mix, w_in, hg_lb_logits, hg_norm, ret_norm, ssm_a_re, ssm_a_im, ssm_log_dt,
           ssm_b_re, ssm_b_im, ssm_c_re, ssm_c_im, ssm_d, ssm_w_glu, gdn_conv, gdn_a_log, gdn_dt_bias,
           gdn_norm, w_branch, w_out, norm_ffn, w_gu, w_down, norm_final):
    bsz, seq, dm = x.shape
    depth = w_in.shape[0]
    T = ROW_TILE
    assert seq % T == 0 and dm % 128 == 0
    npad = (-(N_META + seq)) % T
    lp = npad + N_META + seq
    nt = lp // T
    rows = bsz * lp
    tm = next(t for t in DENSE_TILES if lp % t == 0)
    tmm = next(t for t in MERGE_TILES if lp % t == 0)
    dff = w_down.shape[1]

    h = jnp.concatenate([jnp.zeros((bsz, npad, dm), F32),
                         jnp.broadcast_to(meta.astype(F32)[None], (bsz, N_META, dm)), x.astype(F32)], axis=1)
    h = h.reshape(rows, dm)

    wb16 = w_in.astype(BF16)
    w_bpad = jnp.pad(wb16[:, :, 5120:5128], ((0, 0), (0, 0), (0, 120)))
    w_ba = jnp.swapaxes(wb16[:, :, 5120:5128], 1, 2)
    w_tail = wb16[:, :, 5128:5640 + N_BRANCH * dm]
    w_glu16, w_br16, w_out16 = ssm_w_glu.astype(BF16), w_branch.astype(BF16), w_out.astype(BF16)
    w_gu16, w_down16 = w_gu.astype(BF16), w_down.astype(BF16)

    lb = jnp.cumsum(jax.nn.softmax(hg_lb_logits.astype(F32), axis=0), axis=0)
    lb = lb - lb[:1]
    lb_rows = jnp.stack([jnp.log(lb), jnp.log1p(-lb), 1.0 - lb] + [jnp.zeros_like(lb)] * 5, axis=1)

    neg_decay = -jnp.exp(gdn_a_log.astype(F32))
    zeros4 = jnp.zeros_like(neg_decay)
    hp_col = jnp.stack([jnp.pad(jnp.concatenate([zeros4, neg_decay], axis=1), ((0, 0), (0, 120))),
                        jnp.pad(jnp.concatenate([zeros4, gdn_dt_bias.astype(F32)], axis=1), ((0, 0), (0, 120)))]
                       + [jnp.zeros((depth, 128), F32)] * 6, axis=1)
    hp_row = jnp.stack([jnp.concatenate([zeros4, neg_decay], axis=1),
                        jnp.concatenate([zeros4, gdn_dt_bias.astype(F32)], axis=1)], axis=1)
    hp_row = jnp.broadcast_to(hp_row[..., None], (depth, 2, 8, T))

    cos_t, sin_t, dec, xi, zeta, gc = _retention_tables(lp, npad, SUB_BLOCK)

    rchunks = rows // SSM_CHUNK
    f32 = lambda a: a.astype(F32)
    s5_tab, s5_muls = _s5_constants(f32(ssm_a_re), f32(ssm_a_im), f32(ssm_log_dt), f32(ssm_b_re),
                                           f32(ssm_b_im), f32(ssm_c_re), f32(ssm_c_im))

    gmix = norm_mix.astype(F32).reshape(depth, 1, dm)
    ni = next(n for n in MIXER_INNER_TILES if nt % n == 0)
    row3 = lambda w: pl.BlockSpec((bsz, ni * T, w), lambda i: (0, i, 0))
    tile_tab = lambda w: pl.BlockSpec((ni * T, w), lambda i: (i, 0))
    fixed = lambda *shape: _const_spec(shape, lambda i: (0,) * len(shape))
    vec = lambda w, l: _layer_spec((1, w), l)
    cols = lambda l, start, width: pl.BlockSpec((pl.Element(1), pl.Element(dm), pl.Element(width)),
                                                lambda i: (l, 0, start), pipeline_mode=pl.Buffered(1))
    seq_params = _params(("arbitrary",))

    for l in range(depth):
        h3 = h.reshape(bsz, lp, dm)
        br0, u = pl.pallas_call(
            functools.partial(_tiled_mixer, _hg_tile, 1, npad=npad, n_inner=ni),
            grid=(nt // ni,),
            in_specs=[row3(dm), vec(dm, l), cols(l, 0, 1536), cols(l, 3072, 512), _layer_spec((8, 256), l),
                      vec(512, l)],
            out_specs=[row3(512), row3(512)],
            out_shape=[jax.ShapeDtypeStruct((bsz, lp, 512), BF16), jax.ShapeDtypeStruct((bsz, lp, 512), F32)],
            scratch_shapes=[pltpu.VMEM((bsz, 2, 256, 128), F32)],
            compiler_params=seq_params, name="hgrn2_mixer",
        )(h3, gmix, wb16, wb16, lb_rows, hg_norm.astype(F32).reshape(depth, 1, 512))

        br1 = pl.pallas_call(
            functools.partial(_tiled_mixer, _ret_tile, 1, npad=npad, n_inner=ni),
            grid=(nt // ni,),
            in_specs=[row3(dm), vec(dm, l), cols(l, 1536, 1536), tile_tab(256), tile_tab(256),
                      fixed(RET_HEADS, SUB_BLOCK, SUB_BLOCK), fixed(SUB_BLOCK, 256), fixed(SUB_BLOCK, 256),
                      fixed(1, 256), vec(512, l)],
            out_specs=row3(512),
            out_shape=jax.ShapeDtypeStruct((bsz, lp, 512), BF16),
            scratch_shapes=[pltpu.VMEM((bsz, 2, 256, 128), F32)],
            compiler_params=seq_params, name="retention_mixer",
        )(h3, gmix, wb16, cos_t, sin_t, dec, xi, zeta, gc, ret_norm.astype(F32).reshape(depth, 1, 512))

        br3 = pl.pallas_call(
            functools.partial(_tiled_mixer, _gdn_tile, 2, npad=npad, n_inner=ni),
            grid=(nt // ni,),
            in_specs=[row3(dm), vec(dm, l), cols(l, 3584, 1536), cols(l, 0, 512), _layer_spec((dm, 128), l),
                      _layer_spec((8, dm), l), _layer_spec((CONV_K, 1536), l), _layer_spec((8, 128), l),
                      _layer_spec((2, 8, T), l), vec(512, l)],
            out_specs=row3(512),
            out_shape=jax.ShapeDtypeStruct((bsz, lp, 512), BF16),
            scratch_shapes=[pltpu.VMEM((bsz, GDN_HEADS, GDN_DK, GDN_DV), F32), pltpu.VMEM((bsz, 8, 1536), F32)],
            compiler_params=seq_params, name="gdn_mixer",
        )(h3, gmix, wb16, w_tail, w_bpad, w_ba, gdn_conv.astype(F32), hp_col, hp_row,
          gdn_norm.astype(F32).reshape(depth, 1, 512))
        br0, br1, br3 = (a.reshape(rows, 512) for a in (br0, br1, br3))

        gb = S5_GROUP_BLOCK
        cb = rchunks // bsz
        gspec = lambda a, b_: pl.BlockSpec((None, gb, a, b_), lambda g, b: (l, g, 0, 0))
        uspec = pl.BlockSpec((cb, SSM_CHUNK, gb * SSM_GROUP), lambda g, b: (b, 0, g))
        cw = SSM_CHUNK * gb * SSM_GROUP
        y5 = pl.pallas_call(
            _s5_kernel,
            grid=(SSM_GROUPS // gb, bsz),
            in_specs=[uspec, gspec(s5_tab.shape[2], 2 * SSM_STATE),
                      gspec(s5_muls.shape[2], 2 * SSM_STATE),
                      pl.BlockSpec((None, 1, gb * SSM_GROUP), lambda g, b: (l, 0, g))],
            out_specs=uspec,
            out_shape=jax.ShapeDtypeStruct((rchunks, SSM_CHUNK, SSM_WIDTH), F32),
            scratch_shapes=[pltpu.VMEM((cw, cw), BF16), pltpu.VMEM((cw, gb * 2 * SSM_STATE), BF16),
                            pltpu.VMEM((cw, gb * 2 * SSM_STATE), BF16)],
            compiler_params=_params(("arbitrary", "arbitrary")), name="s5_mixer",
        )(u.reshape(rchunks, SSM_CHUNK, SSM_WIDTH), s5_tab, s5_muls,
          ssm_d.astype(F32).reshape(depth, 1, SSM_WIDTH))
        y5 = y5.reshape(rows, SSM_WIDTH)

        mrow = lambda w: pl.BlockSpec((tmm, w), lambda i: (i, 0))
        drow = lambda w: pl.BlockSpec((tm, w), lambda i: (i, 0))
        h = pl.pallas_call(
            functools.partial(_merge_kernel, npad=npad, tiles_per_batch=lp // tmm),
            grid=(rows // tmm,),
            in_specs=[mrow(dm), vec(dm, l), mrow(512), mrow(512), mrow(512), mrow(512),
                      cols(l, 512, N_BRANCH * dm), _layer_spec((512, 512), l),
                      _layer_spec((N_BRANCH, BRANCH_WIDTH, dm), l), _layer_spec((dm, dm), l)],
            out_specs=mrow(dm),
            out_shape=jax.ShapeDtypeStruct((rows, dm), F32),
            compiler_params=seq_params, name="merge_out",
        )(h, gmix, br0, br1, y5, br3, w_tail, w_glu16, w_br16, w_out16)

        h = pl.pallas_call(
            functools.partial(_ffn_kernel, n_split=FFN_SPLITS),
            grid=(rows // tm,),
            in_specs=[drow(dm), vec(dm, l), _layer_spec((dm, 2 * dff), l), _layer_spec((dff, dm), l)],
            out_specs=drow(dm),
            out_shape=jax.ShapeDtypeStruct((rows, dm), F32),
            compiler_params=seq_params, name="swiglu_ffn",
        )(h, norm_ffn.astype(F32).reshape(depth, 1, dm), w_gu16, w_down16)

    tf = next(t for t in FINAL_TILES if seq % t == 0)
    ns = seq // tf
    out = pl.pallas_call(
        _final_kernel,
        grid=(bsz, ns),
        in_specs=[pl.BlockSpec((pl.Element(tf), pl.Element(dm)),
                               lambda b, i: ((b * (lp // 8) + (npad + N_META) // 8 + i * (tf // 8)) * 8, 0)),
                  _const_spec((1, dm), lambda b, i: (0, 0))],
        out_specs=pl.BlockSpec((tf, dm), lambda b, i: (b * ns + i, 0)),
        out_shape=jax.ShapeDtypeStruct((bsz * seq, dm), x.dtype),
        compiler_params=_params(("arbitrary", "arbitrary")), name="final_norm",
    )(h, norm_final.astype(F32).reshape(1, dm))
    return out.reshape(bsz, seq, dm)
```

```python
import functools
import math

import jax
import jax.numpy as jnp
from jax import lax
from jax.experimental import pallas as pl
from jax.experimental.pallas import tpu as pltpu

F32 = jnp.float32
BF16 = jnp.bfloat16

N_META = 16
EPS = 1e-6
HG_HEADS, HG_DK, HG_DV, HG_CHUNK = 4, 64, 128, 16
RET_HEADS, RET_DK, RET_DV = 4, 64, 128
ROPE_BASE = 10000.0
SSM_WIDTH, SSM_GROUP, SSM_STATE = 512, 16, 64
SSM_GROUPS = SSM_WIDTH // SSM_GROUP
SSM_CHUNK = 16
S5_GROUP_BLOCK = 8
GDN_HEADS, GDN_DK, GDN_DV = 4, 128, 128
CONV_K = 4
CHUNK = 64
N_BRANCH = 4
BRANCH_WIDTH = 512

ROW_TILE = 256
SUB_BLOCK = 128
MIXER_INNER_TILES = (5, 4, 3, 2, 1)
DENSE_TILES = (768, 640, 384, 320, 128)
MERGE_TILES = (768, 640, 384, 320, 128)
FINAL_TILES = (1024, 512, 128)
FFN_SPLITS = 2
VMEM_LIMIT = 56 * 1024 * 1024

NT_DIMS = (((1,), (1,)), ((), ()))
TN_DIMS = (((0,), (0,)), ((), ()))


def _mm(a, b):
    return jnp.dot(a, b, preferred_element_type=F32)


def _mm_nt(a, b):
    return lax.dot_general(a, b, NT_DIMS, preferred_element_type=F32)


def _mm_tn(a, b):
    return lax.dot_general(a, b, TN_DIMS, preferred_element_type=F32)


def _split3(x):
    hi = x.astype(BF16)
    r = x - hi.astype(F32)
    mid = r.astype(BF16)
    lo = (r - mid.astype(F32)).astype(BF16)
    return hi, mid, lo


def _mm01(m01, x):
    hi, mid, lo = _split3(x)
    return _mm(m01, hi) + _mm(m01, mid) + _mm(m01, lo)


def _mm01_right(x, m01):
    hi, mid, lo = _split3(x)
    return _mm(hi, m01) + _mm(mid, m01) + _mm(lo, m01)


def _seg_masks(n, chunk):
    sh = int(math.log2(chunk))
    r = lax.broadcasted_iota(jnp.int32, (n, n), 0)
    c = lax.broadcasted_iota(jnp.int32, (n, n), 1)
    same = lax.shift_right_logical(r, sh) == lax.shift_right_logical(c, sh)
    return same, same & (c <= r), same & (c < r), same & (c >= r)


def _sigmoid(x):
    return jax.nn.sigmoid(x)


def _silu(x):
    return x * jax.nn.sigmoid(x)


def _softplus(x):
    return jnp.maximum(x, 0.0) + jnp.log1p(jnp.exp(-jnp.abs(x)))


def _norm_in(x, g, row0, npad):
    y = x * lax.rsqrt(jnp.mean(x * x, axis=-1, keepdims=True) + EPS) * g
    rows = row0 + lax.broadcasted_iota(jnp.int32, y.shape, 0)
    return jnp.where(rows >= npad, y, 0.0).astype(BF16)


def _stacked_norm_in(x_ref, rs, g_ref, row0, npad):
    return jnp.concatenate([_norm_in(x_ref[b, rs, :], g_ref[...], row0, npad) for b in range(x_ref.shape[0])], axis=0)


def _tiled_mixer(tile_fn, n_scratch, *refs, npad, n_inner):
    i = pl.program_id(0)

    @pl.when(i == 0)
    def _():
        for r in refs[len(refs) - n_scratch:]:
            r[...] = jnp.zeros_like(r)

    def step(j, carry):
        tile_fn(i * n_inner + j, pl.ds(pl.multiple_of(j * ROW_TILE, ROW_TILE), ROW_TILE), *refs, npad=npad)
        return carry

    lax.fori_loop(0, n_inner, step, 0)


def _head_rms(o, width):
    parts = []
    for h in range(o.shape[1] // width):
        oh = o[:, h * width:(h + 1) * width]
        parts.append(oh * lax.rsqrt(jnp.mean(oh * oh, axis=-1, keepdims=True) + EPS))
    return jnp.concatenate(parts, axis=1)


def _pair_block_mask():
    r = lax.broadcasted_iota(jnp.int32, (256, 128), 0)
    c = lax.broadcasted_iota(jnp.int32, (256, 128), 1)
    return lax.shift_right_logical(r, 7) == lax.shift_right_logical(c, 6)


def _head_lane_mask(a):
    c = lax.broadcasted_iota(jnp.int32, (1, 128), 1)
    return lax.shift_right_logical(c, 6) == a


def _pair_state_scan(s_ref, kv, decay_row, nb, nchunk):
    entering = {}
    for b in range(nb):
        for p in range(2):
            st = s_ref[b, p]
            for n in range(nchunk):
                entering[b, p, n] = st.astype(BF16)
                st = decay_row(b, p, n) * st + kv[b, p, n]
            s_ref[b, p] = st
    return entering


def _hg_tile(tile, rs, x_ref, g_ref, w_ref, wu_ref, lb_ref, ng_ref, o_ref, u_ref, s_ref, *, npad):
    nb, T, SB = x_ref.shape[0], ROW_TILE, SUB_BLOCK
    hn = _stacked_norm_in(x_ref, rs, g_ref, tile * T, npad)
    z = _mm(hn, w_ref[0])
    zq, zf, v, zg = z[:, 0:256], z[:, 256:512], z[:, 512:1024], z[:, 1024:1536]
    u = _mm(hn, wu_ref[0])
    for b in range(nb):
        u_ref[b, rs, :] = u[b * T:(b + 1) * T]

    log_lb, log1m_lb, one_m_lb = lb_ref[0:1, :], lb_ref[1:2, :], lb_ref[2:3, :]
    e = jnp.exp(-jnp.abs(zf))
    inv1pe = 1.0 / (1.0 + e)
    log_sig = jnp.minimum(zf, 0.0) - jnp.log1p(e)
    y2 = log1m_lb + log_sig
    log_f = jnp.maximum(log_lb, y2) + jnp.log1p(jnp.exp(-jnp.abs(log_lb - y2)))
    k = one_m_lb * jnp.where(zf >= 0.0, e * inv1pe, inv1pe)
    q = _silu(zq) * (HG_DK ** -0.5)

    same, low, _, _ = _seg_masks(T, HG_CHUNK)
    per_batch = lambda m01: jnp.concatenate([_mm01(m01, log_f[b * T:(b + 1) * T]) for b in range(nb)], axis=0)
    b_cum = per_batch(low.astype(BF16))
    b_last = per_batch(same.astype(BF16))
    q_t = q * jnp.exp(b_cum)
    k_t = (k * jnp.exp(-b_cum)).astype(BF16)
    k_end = (k * jnp.exp(b_last - b_cum)).astype(BF16)
    d_end = jnp.exp(b_last)
    q_tb = q_t.astype(BF16)
    vb = v.astype(BF16)
    low_t = low[0:SB, 0:SB]

    nblk = nb * T // SB
    chains = [(k, h) for k in range(nblk) for h in range(HG_HEADS)]
    rows = lambda k: slice(k * SB, (k + 1) * SB)
    slab = lambda p: slice(128 * p, 128 * p + 128)

    qh = [jnp.where(_head_lane_mask(h % 2), q_t[rows(k), slab(h // 2)], 0.0).astype(BF16) for k, h in chains]
    att = [jnp.where(low_t, _mm_nt(qh[c], k_t[rows(k), slab(h // 2)]), 0.0).astype(BF16)
           for c, (k, h) in enumerate(chains)]
    o_intra = [_mm(att[c], vb[rows(k), slab(h)]) for c, (k, h) in enumerate(chains)]

    nchunk = T // HG_CHUNK
    crow = lambda b, n: slice(b * T + n * HG_CHUNK, b * T + (n + 1) * HG_CHUNK)
    blk = _pair_block_mask()
    kv = {(b, p, n): jnp.where(blk, _mm_tn(vb[crow(b, n), 256 * p:256 * p + 256], k_end[crow(b, n), slab(p)]), 0.0)
          for b in range(nb) for n in range(nchunk) for p in range(2)}
    entering = _pair_state_scan(
        s_ref, kv, lambda b, p, n: d_end[b * T + n * HG_CHUNK:b * T + n * HG_CHUNK + 1, slab(p)], nb, nchunk)
    o_inter = {(b, p, n): _mm_nt(q_tb[crow(b, n), slab(p)], entering[b, p, n])
               for b in range(nb) for n in range(nchunk) for p in range(2)}

    inter = jnp.concatenate([jnp.concatenate([o_inter[b, 0, n], o_inter[b, 1, n]], axis=1)
                             for b in range(nb) for n in range(nchunk)], axis=0)
    intra = jnp.concatenate([jnp.concatenate([o_intra[k * HG_HEADS + h] for h in range(HG_HEADS)], axis=1)
                             for k in range(nblk)], axis=0)
    o = _head_rms(intra + inter, HG_DV) * ng_ref[...] * _silu(zg)
    for b in range(nb):
        o_ref[b, rs, :] = o[b * T:(b + 1) * T].astype(o_ref.dtype)


def _rot_half_swap(x):
    lane = lax.broadcasted_iota(jnp.int32, x.shape, 1)
    first_half = (lane & 63) < 32
    return jnp.where(first_half, pltpu.roll(x, x.shape[1] - 32, 1), pltpu.roll(x, 32, 1))


def _ret_tile(tile, rs, x_ref, g_ref, w_ref, cos_ref, sin_ref, dec_ref, xi_ref, zeta_ref, gc_ref, ng_ref,
              o_ref, s_ref, *, npad):
    nb, T, SB = x_ref.shape[0], ROW_TILE, SUB_BLOCK
    hn = _stacked_norm_in(x_ref, rs, g_ref, tile * T, npad)
    z = _mm(hn, w_ref[0])
    zq, zk, v, zg = z[:, 0:256], z[:, 256:512], z[:, 512:1024], z[:, 1024:1536]
    tile_rows = lambda a: jnp.concatenate([a] * (nb * T // a.shape[0]), axis=0)
    cos, sin = tile_rows(cos_ref[rs, :]), tile_rows(sin_ref[rs, :])
    q = zq * cos + _rot_half_swap(zq) * sin
    k = (zk * cos + _rot_half_swap(zk) * sin) * (RET_DK ** -0.5)
    kb, vb = k.astype(BF16), v.astype(BF16)
    q_xi = (q * tile_rows(xi_ref[...])).astype(BF16)
    k_zeta = (k * tile_rows(zeta_ref[...])).astype(BF16)

    nblk = nb * T // SB
    chains = [(kk, h) for kk in range(nblk) for h in range(RET_HEADS)]
    rows = lambda kk: slice(kk * SB, (kk + 1) * SB)
    slab = lambda p: slice(128 * p, 128 * p + 128)

    qh = [jnp.where(_head_lane_mask(h % 2), q[rows(kk), slab(h // 2)], 0.0).astype(BF16) for kk, h in chains]
    att = [(_mm_nt(qh[c], kb[rows(kk), slab(h // 2)]) * dec_ref[h]).astype(BF16) for c, (kk, h) in enumerate(chains)]
    o_intra = [_mm(att[c], vb[rows(kk), slab(h)]) for c, (kk, h) in enumerate(chains)]

    nchunk = T // CHUNK
    crow = lambda b, n: slice(b * T + n * CHUNK, b * T + (n + 1) * CHUNK)
    blk = _pair_block_mask()
    kv = {(b, p, n): jnp.where(blk, _mm_tn(vb[crow(b, n), 256 * p:256 * p + 256], k_zeta[crow(b, n), slab(p)]), 0.0)
          for b in range(nb) for n in range(nchunk) for p in range(2)}
    entering = _pair_state_scan(s_ref, kv, lambda b, p, n: gc_ref[:, slab(p)], nb, nchunk)
    o_inter = {(b, p, n): _mm_nt(q_xi[crow(b, n), slab(p)], entering[b, p, n])
               for b in range(nb) for n in range(nchunk) for p in range(2)}

    inter = jnp.concatenate([jnp.concatenate([o_inter[b, 0, n], o_inter[b, 1, n]], axis=1)
                             for b in range(nb) for n in range(nchunk)], axis=0)
    intra = jnp.concatenate([jnp.concatenate([o_intra[kk * RET_HEADS + h] for h in range(RET_HEADS)], axis=1)
                             for kk in range(nblk)], axis=0)
    o = intra + inter

    parts = []
    for h in range(RET_HEADS):
        oh = o[:, 128 * h:128 * h + 128]
        oc = oh - jnp.mean(oh, axis=-1, keepdims=True)
        parts.append(oc * lax.rsqrt(jnp.mean(oc * oc, axis=-1, keepdims=True) + EPS))
    o = jnp.concatenate(parts, axis=1) * ng_ref[...] * _silu(zg)
    for b in range(nb):
        o_ref[b, rs, :] = o[b * T:(b + 1) * T].astype(o_ref.dtype)


def _gdn_tile(tile, rs, x_ref, g_ref, w_ref, wg_ref, wbp_ref, wba_ref, conv_ref, hp_col_ref, hp_row_ref, ng_ref,
              o_ref, s_ref, carry_ref, *, npad):
    nb, T, SB = x_ref.shape[0], ROW_TILE, SUB_BLOCK
    H = GDN_HEADS
    assert SB == 128
    hn = _stacked_norm_in(x_ref, rs, g_ref, tile * T, npad)
    zqkv = _mm(hn, w_ref[0])
    zg = _mm(hn, wg_ref[0])
    zba = _mm(hn, wbp_ref[...])
    ba_row = _mm_nt(wba_ref[...], hn)

    xc = []
    for b in range(nb):
        zb = zqkv[b * T:(b + 1) * T]
        ext = jnp.concatenate([carry_ref[b], zb], axis=0)
        carry_ref[b] = zb[T - 8:T, :]
        acc = conv_ref[3:4, :] * zb
        for j in range(CONV_K - 1):
            s = CONV_K - 1 - j
            acc = acc + conv_ref[j:j + 1, :] * ext[8 - s:8 - s + T, :]
        xc.append(acc)
    qkv = _silu(jnp.concatenate(xc, axis=0))

    beta_col = _sigmoid(zba)
    g_col = hp_col_ref[0:1, :] * _softplus(zba + hp_col_ref[1:2, :])
    rep = nb * T // hp_row_ref.shape[2]
    g_row = (jnp.concatenate([hp_row_ref[0]] * rep, axis=1)
             * _softplus(ba_row + jnp.concatenate([hp_row_ref[1]] * rep, axis=1)))
    same, low, strict, up = _seg_masks(T, CHUNK)
    low_b, same_b, up_b = low.astype(BF16), same.astype(BF16), up.astype(BF16)
    bat = lambda b: slice(b * T, (b + 1) * T)
    cum_col = jnp.concatenate([_mm01(low_b, g_col[bat(b)]) for b in range(nb)], axis=0)
    last_col = jnp.concatenate([_mm01(same_b, g_col[bat(b)]) for b in range(nb)], axis=0)
    cum_row = jnp.concatenate([_mm01_right(g_row[:, bat(b)], up_b) for b in range(nb)], axis=1)
    low_t, strict_t = low[0:SB, 0:SB], strict[0:SB, 0:SB]
    eye = (lax.broadcasted_iota(jnp.int32, (SB, SB), 0) == lax.broadcasted_iota(jnp.int32, (SB, SB), 1)).astype(F32)

    nsb = T // SB
    chains = [(k, h) for k in range(nb * nsb) for h in range(H)]
    CS = range(len(chains))
    rows = lambda k: slice(k * SB, (k + 1) * SB)
    qn, kn, knb, vh, cum_c, last_c, beta_c, gamma = [], [], [], [], [], [], [], []
    for k, h in chains:
        qh = qkv[rows(k), 128 * h:128 * h + 128]
        kh = qkv[rows(k), 512 + 128 * h:512 + 128 * h + 128]
        vh.append(qkv[rows(k), 1024 + 128 * h:1024 + 128 * h + 128])
        qn.append(qh * lax.rsqrt(jnp.sum(qh * qh, axis=-1, keepdims=True) + EPS) * (GDN_DK ** -0.5))
        kn.append(kh * lax.rsqrt(jnp.sum(kh * kh, axis=-1, keepdims=True) + EPS))
        knb.append(kn[-1].astype(BF16))
        cum_c.append(jnp.broadcast_to(cum_col[rows(k), H + h:H + h + 1], (SB, 128)))
        last_c.append(jnp.broadcast_to(last_col[rows(k), H + h:H + h + 1], (SB, 128)))
        beta_c.append(jnp.broadcast_to(beta_col[rows(k), h:h + 1], (SB, 128)))
        gamma.append(jnp.exp(jnp.where(low_t, cum_c[-1] - cum_row[H + h:H + h + 1, rows(k)], -jnp.inf)))

    kk = [_mm_nt(knb[c], knb[c]) for c in CS]
    qk = [_mm_nt(qn[c].astype(BF16), knb[c]) for c in CS]
    a0 = [jnp.where(strict_t, kk[c] * beta_c[c] * gamma[c], 0.0) for c in CS]
    inv = [eye - a0[c] for c in CS]
    a0b = [a.astype(BF16) for a in a0]
    pw = [_mm(a0b[c], a0b[c]) for c in CS]
    nlev = int(math.log2(CHUNK)) - 1
    for lev in range(nlev):
        pwb = [p.astype(BF16) for p in pw]
        if lev + 1 < nlev:
            both = [_mm(pwb[c], jnp.concatenate([inv[c], pw[c]], axis=1).astype(BF16)) for c in CS]
            inv = [inv[c] + both[c][:, 0:SB] for c in CS]
            pw = [both[c][:, SB:2 * SB] for c in CS]
        else:
            inv = [inv[c] + _mm(pwb[c], inv[c].astype(BF16)) for c in CS]
    e_cum = [jnp.exp(cum_c[c]) for c in CS]
    uw = [_mm(inv[c].astype(BF16),
              jnp.concatenate([vh[c] * beta_c[c], kn[c] * beta_c[c] * e_cum[c]], axis=1).astype(BF16)) for c in CS]
    u = [x[:, 0:128] for x in uw]
    wb = [x[:, 128:256].astype(BF16) for x in uw]
    att = [(qk[c] * gamma[c]).astype(BF16) for c in CS]
    qd = [(qn[c] * e_cum[c]).astype(BF16) for c in CS]
    kd = [(kn[c] * jnp.exp(last_c[c] - cum_c[c])).astype(BF16) for c in CS]
    dl = [jnp.exp(last_c[c]) for c in CS]

    seqs = [(b, h) for b in range(nb) for h in range(H)]
    st = [s_ref[b, h] for b, h in seqs]
    o_rows = [[] for _ in CS]
    for s in range(nsb):
        cs = [(b * nsb + s) * H + h for b, h in seqs]
        v_rows = [[] for _ in seqs]
        for n in range(SB // CHUNK):
            r = slice(n * CHUNK, (n + 1) * CHUNK)
            stb = [x.astype(BF16) for x in st]
            v_new = [u[c][r] - _mm(wb[c][r], stb[i]) for i, c in enumerate(cs)]
            for i, c in enumerate(cs):
                v_rows[i].append(v_new[i])
                rest = [u[c][(n + 1) * CHUNK:]] if (n + 1) * CHUNK < SB else []
                v_all = jnp.concatenate(v_rows[i] + rest, axis=0)
                o_rows[c].append(_mm(qd[c][r], stb[i]) + _mm(att[c][r], v_all.astype(BF16)))
            st = [dl[c][n * CHUNK:n * CHUNK + 1, :] * st[i] + _mm_tn(kd[c][r], v_new[i].astype(BF16))
                  for i, c in enumerate(cs)]
    for i, (b, h) in enumerate(seqs):
        s_ref[b, h] = st[i]
    o = jnp.concatenate([jnp.concatenate([jnp.concatenate(o_rows[k * H + h], axis=0) for h in range(H)], axis=1)
                         for k in range(nb * nsb)], axis=0)
    o = _head_rms(o, GDN_DV) * ng_ref[...] * _silu(zg)
    for b in range(nb):
        o_ref[b, rs, :] = o[b * T:(b + 1) * T].astype(o_ref.dtype)


def _shift_rows(x, s):
    n = x.shape[0]
    if s % 8 == 0:
        return jnp.concatenate([jnp.zeros((s, x.shape[1]), x.dtype), x[:n - s]], axis=0)
    rows = lax.broadcasted_iota(jnp.int32, x.shape, 0)
    return jnp.where(rows >= s, pltpu.roll(x, s, 0), 0.0)


def _s5_kernel(u_ref, tab_ref, mul_ref, d_ref, y_ref, m_ref, wst_ref, wot_ref):
    C, P = SSM_CHUNK, SSM_GROUP
    W = u_ref.shape[2]
    GB = W // P
    rb = u_ref.shape[0]

    @pl.when(pl.program_id(1) == 0)
    def _():
        shift = int(math.log2(P))
        row_g = lax.shift_right_logical(lax.broadcasted_iota(jnp.int32, (W, W), 0), shift)
        lane_g = lax.shift_right_logical(lax.broadcasted_iota(jnp.int32, (W, W), 1), shift)
        tabs = [tab_ref[g] for g in range(GB)]
        i_hi, i_mid, i_lo = _split3(jnp.concatenate([tb[0:16] for tb in tabs], axis=0))
        lag_blocks = []
        for lag in range(C):
            o_hi, o_mid, o_lo = _split3(jnp.concatenate(
                [tb[32:48] * tb[64 + lag:65 + lag] + tb[48:64] * tb[88 + lag:89 + lag] for tb in tabs], axis=0))
            k_lag = (_mm_nt(i_hi, o_hi) + _mm_nt(i_hi, o_mid) + _mm_nt(i_mid, o_hi)
                     + _mm_nt(i_hi, o_lo) + _mm_nt(i_lo, o_hi) + _mm_nt(i_mid, o_mid))
            lag_blocks.append(jnp.where(row_g == lane_g, k_lag, 0.0).astype(BF16))
        zero = jnp.zeros((W, W), BF16)
        for s in range(C):
            for t in range(C):
                m_ref[s * W:(s + 1) * W, t * W:(t + 1) * W] = lag_blocks[t - s] if t >= s else zero
        row_g = lax.shift_right_logical(lax.broadcasted_iota(jnp.int32, (W, 128), 0), shift)
        for s in range(C):
            k_in, k_out = C - 1 - s, s + 1
            w_in_s = jnp.concatenate([tb[0:16] * tb[64 + k_in:65 + k_in] + tb[16:32] * tb[88 + k_in:89 + k_in]
                                      for tb in tabs], axis=0)
            w_out_s = jnp.concatenate([tb[32:48] * tb[64 + k_out:65 + k_out] + tb[48:64] * tb[88 + k_out:89 + k_out]
                                       for tb in tabs], axis=0)
            for g in range(GB):
                wst_ref[s * W:(s + 1) * W, g * 128:(g + 1) * 128] = jnp.where(row_g == g, w_in_s, 0.0).astype(BF16)
                wot_ref[s * W:(s + 1) * W, g * 128:(g + 1) * 128] = jnp.where(row_g == g, w_out_s, 0.0).astype(BF16)

    ut = [u_ref[:, t, :] for t in range(C)]
    v = jnp.concatenate(ut, axis=1).astype(BF16)
    conv = [_mm(v[:, 0:(2 * j + 2) * W], m_ref[0:(2 * j + 2) * W, 2 * j * W:(2 * j + 2) * W]) for j in range(C // 2)]
    z = _mm(v, wst_ref[...])
    xin = []
    for g in range(GB):
        x = z[:, g * 128:(g + 1) * 128]
        row_in_blk = lax.broadcasted_iota(jnp.int32, x.shape, 0) & 7
        for k in range(3):
            s = 1 << k
            xs = jnp.where(row_in_blk >= s, pltpu.roll(x, s, 0), 0.0)
            x = x + xs * mul_ref[g, 2 * k:2 * k + 1, :] + pltpu.roll(xs, 64, 1) * mul_ref[g, 2 * k + 1:2 * k + 2, :]
        xsw = pltpu.roll(x, 64, 1)
        p1, p2 = mul_ref[g, 8:16, :], mul_ref[g, 16:24, :]
        blocks, last, last_sw = [x[0:8]], x[7:8], xsw[7:8]
        for j in range(1, rb // 8):
            xb = x[8 * j:8 * j + 8] + last * p1 + last_sw * p2
            xb_sw = xsw[8 * j:8 * j + 8] + last_sw * p1 - last * p2
            blocks.append(xb)
            last, last_sw = xb[7:8], xb_sw[7:8]
        xin.append(_shift_rows(jnp.concatenate(blocks, axis=0), 1))
    from_state = _mm_nt(jnp.concatenate(xin, axis=1).astype(BF16), wot_ref[...])
    for t in range(C):
        y_ref[:, t, :] = (conv[t // 2][:, (t % 2) * W:(t % 2 + 1) * W] + from_state[:, t * W:(t + 1) * W]
                          + d_ref[...] * ut[t])


def _gelu_tanh(x):
    return 0.5 * x * (1.0 + jnp.tanh(math.sqrt(2.0 / math.pi) * (x + 0.044715 * (x * x * x))))


def _merge_kernel(x_ref, g_ref, b0_ref, b1_ref, y5_ref, b3_ref, wg_ref, wglu_ref, wb_ref, wo_ref,
                  o_ref, *, npad, tiles_per_batch):
    T = x_ref.shape[0]
    x = x_ref[...]
    row0 = lax.rem(pl.program_id(0), tiles_per_batch) * T
    hn = _norm_in(x, g_ref[...], row0, npad)
    ya = _gelu_tanh(y5_ref[...])
    b2 = ya * _sigmoid(_mm(ya.astype(BF16), wglu_ref[...]))
    branches = (b0_ref[...], b1_ref[...], b2.astype(BF16), b3_ref[...])
    dm = x.shape[1]
    merged = None
    for m in range(N_BRANCH):
        gate = _sigmoid(_mm(hn, wg_ref[0, :, m * dm:(m + 1) * dm]))
        term = gate * _mm(branches[m], wb_ref[m])
        merged = term if merged is None else merged + term
    o_ref[...] = x + _mm(merged.astype(BF16), wo_ref[...])


def _ffn_kernel(x_ref, g_ref, wgu_ref, wd_ref, o_ref, *, n_split):
    x = x_ref[...]
    h = (x * lax.rsqrt(jnp.mean(x * x, axis=-1, keepdims=True) + EPS) * g_ref[...]).astype(BF16)
    dff = wd_ref.shape[0]
    tiles = -(-dff // 256)
    bounds = [min(dff, 256 * (-(-tiles * c // n_split))) for c in range(n_split + 1)]
    acts = []
    for lo, hi in zip(bounds[:-1], bounds[1:]):
        gt = _mm(h, wgu_ref[:, lo:hi])
        up = _mm(h, wgu_ref[:, dff + lo:dff + hi])
        acts.append((_silu(gt) * up).astype(BF16))
    acc = x
    for a, lo, hi in zip(acts, bounds[:-1], bounds[1:]):
        acc = acc + _mm(a, wd_ref[lo:hi, :])
    o_ref[...] = acc


def _final_kernel(x_ref, g_ref, o_ref):
    x = x_ref[...]
    o_ref[...] = x * lax.rsqrt(jnp.mean(x * x, axis=-1, keepdims=True) + EPS) * g_ref[...]


def _const_spec(block, index):
    return pl.BlockSpec(block, index, pipeline_mode=pl.Buffered(1))


def _layer_spec(shape, layer):
    zeros = (0,) * len(shape)
    return _const_spec((None,) + tuple(shape), lambda i: (layer,) + zeros)


def _params(sem):
    return pltpu.CompilerParams(dimension_semantics=sem, vmem_limit_bytes=VMEM_LIMIT)


def _s5_constants(a_re, a_im, log_dt, b_re, b_im, c_re, c_im):
    C, G, N, P = SSM_CHUNK, SSM_GROUPS, SSM_STATE, SSM_GROUP
    L = a_re.shape[0]
    dt = jnp.exp(log_dt)[..., None]
    mag = jnp.exp(dt * a_re)
    ar, ai = mag * jnp.cos(dt * a_im), mag * jnp.sin(dt * a_im)
    den = a_re * a_re + a_im * a_im
    nr, ni = ar - 1.0, ai
    cr, ci = (nr * a_re + ni * a_im) / den, (ni * a_re - nr * a_im) / den
    bbr = cr[..., None] * b_re - ci[..., None] * b_im
    bbi = cr[..., None] * b_im + ci[..., None] * b_re
    pr, pi = [jnp.ones_like(ar)], [jnp.zeros_like(ar)]
    for _ in range(C):
        pr, pi = pr + [pr[-1] * ar - pi[-1] * ai], pi + [pr[-1] * ai + pi[-1] * ar]
    pr, pi = jnp.stack(pr), jnp.stack(pi)
    bbr_t, bbi_t = jnp.swapaxes(bbr, 2, 3), jnp.swapaxes(bbi, 2, 3)
    cat = lambda a, b: jnp.concatenate([a, b], axis=-1)
    pw_r = jnp.moveaxis(cat(pr, pr), 0, 2)
    pw_i = jnp.moveaxis(cat(pi, pi), 0, 2)
    gap = jnp.zeros((L, G, 24 - (C + 1), 2 * N), F32)
    tab = jnp.concatenate([cat(bbr_t, bbi_t), cat(-bbi_t, bbr_t), cat(c_re, -c_im), cat(-c_im, -c_re),
                           pw_r, gap, pw_i, gap], axis=2)
    sr, si = pr[C], pi[C]
    lr, li, log_rows = sr, si, []
    for _ in range(3):
        log_rows += [cat(lr, lr), cat(-li, li)]
        lr, li = lr * lr - li * li, 2.0 * lr * li
    qr, qi, p1, p2 = sr, si, [], []
    for _ in range(8):
        p1, p2 = p1 + [cat(qr, qr)], p2 + [cat(-qi, qi)]
        qr, qi = qr * sr - qi * si, qr * si + qi * sr
    zero = jnp.zeros_like(log_rows[0])
    muls = jnp.stack(log_rows + [zero, zero] + p1 + p2, axis=2)
    return tab.astype(F32), muls.astype(F32)


def _retention_tables(lp, npad, tile):
    half = RET_DK // 2
    pos = (jnp.arange(lp) - npad).astype(F32)
    inv = ROPE_BASE ** (-jnp.arange(half, dtype=F32) / half)
    ang = pos[:, None] * inv[None, :]
    cos, sin = jnp.cos(ang), jnp.sin(ang)
    cos_t = jnp.tile(jnp.concatenate([cos, cos], axis=1), (1, RET_HEADS))
    sin_t = jnp.tile(jnp.concatenate([-sin, sin], axis=1), (1, RET_HEADS))
    log_gamma = jnp.log1p(-jnp.exp2(-5.0 - jnp.arange(RET_HEADS, dtype=F32)))
    idx = jnp.arange(tile)
    loc = (idx % CHUNK).astype(F32)
    diff = (idx[:, None] - idx[None, :]).astype(F32)
    ok = ((idx[:, None] // CHUNK) == (idx[None, :] // CHUNK)) & (diff >= 0)
    dec = jnp.where(ok[None], jnp.exp(log_gamma[:, None, None] * jnp.maximum(diff, 0.0)[None]), 0.0)
    xi = jnp.repeat(jnp.exp(log_gamma[None, :] * (loc[:, None] + 1.0)), RET_DK, axis=1)
    zeta = jnp.repeat(jnp.exp(log_gamma[None, :] * (CHUNK - 1.0 - loc[:, None])), RET_DK, axis=1)
    gc = jnp.repeat(jnp.exp(log_gamma * CHUNK), RET_DK)[None, :]
    return cos_t, sin_t, dec, xi, zeta, gc


def kernel(x, meta, norm_mix, w_in, hg_lb_logits, hg_norm, ret_norm, ssm_a_re, ssm_a_im, ssm_log_dt,
           ssm_b_re, ssm_b_im, ssm_c_re, ssm_c_im, ssm_d, ssm_w_glu, gdn_conv, gdn_a_log, gdn_dt_bias,
           gdn_norm, w_branch, w_out, norm_ffn, w_gu, w_down, norm_final):
    bsz, seq, dm = x.shape
    depth = w_in.shape[0]
    T = ROW_TILE
    assert seq % T == 0 and dm % 128 == 0
    npad = (-(N_META + seq)) % T
    lp = npad + N_META + seq
    nt = lp // T
    rows = bsz * lp
    tm = next(t for t in DENSE_TILES if lp % t == 0)
    tmm = next(t for t in MERGE_TILES if lp % t == 0)
    dff = w_down.shape[1]

    h = jnp.concatenate([jnp.zeros((bsz, npad, dm), F32),
                         jnp.broadcast_to(meta.astype(F32)[None], (bsz, N_META, dm)), x.astype(F32)], axis=1)
    h = h.reshape(rows, dm)

    wb16 = w_in.astype(BF16)
    w_bpad = jnp.pad(wb16[:, :, 5120:5128], ((0, 0), (0, 0), (0, 120)))
    w_ba = jnp.swapaxes(wb16[:, :, 5120:5128], 1, 2)
    w_tail = wb16[:, :, 5128:5640 + N_BRANCH * dm]
    w_glu16, w_br16, w_out16 = ssm_w_glu.astype(BF16), w_branch.astype(BF16), w_out.astype(BF16)
    w_gu16, w_down16 = w_gu.astype(BF16), w_down.astype(BF16)

    lb = jnp.cumsum(jax.nn.softmax(hg_lb_logits.astype(F32), axis=0), axis=0)
    lb = lb - lb[:1]
    lb_rows = jnp.stack([jnp.log(lb), jnp.log1p(-lb), 1.0 - lb] + [jnp.zeros_like(lb)] * 5, axis=1)

    neg_decay = -jnp.exp(gdn_a_log.astype(F32))
    zeros4 = jnp.zeros_like(neg_decay)
    hp_col = jnp.stack([jnp.pad(jnp.concatenate([zeros4, neg_decay], axis=1), ((0, 0), (0, 120))),
                        jnp.pad(jnp.concatenate([zeros4, gdn_dt_bias.astype(F32)], axis=1), ((0, 0), (0, 120)))]
                       + [jnp.zeros((depth, 128), F32)] * 6, axis=1)
    hp_row = jnp.stack([jnp.concatenate([zeros4, neg_decay], axis=1),
                        jnp.concatenate([zeros4, gdn_dt_bias.astype(F32)], axis=1)], axis=1)
    hp_row = jnp.broadcast_to(hp_row[..., None], (depth, 2, 8, T))

    cos_t, sin_t, dec, xi, zeta, gc = _retention_tables(lp, npad, SUB_BLOCK)

    rchunks = rows // SSM_CHUNK
    f32 = lambda a: a.astype(F32)
    s5_tab, s5_muls = _s5_constants(f32(ssm_a_re), f32(ssm_a_im), f32(ssm_log_dt), f32(ssm_b_re),
                                           f32(ssm_b_im), f32(ssm_c_re), f32(ssm_c_im))

    gmix = norm_mix.astype(F32).reshape(depth, 1, dm)
    ni = next(n for n in MIXER_INNER_TILES if nt % n == 0)
    row3 = lambda w: pl.BlockSpec((bsz, ni * T, w), lambda i: (0, i, 0))
    tile_tab = lambda w: pl.BlockSpec((ni * T, w), lambda i: (i, 0))
    fixed = lambda *shape: _const_spec(shape, lambda i: (0,) * len(shape))
    vec = lambda w, l: _layer_spec((1, w), l)
    cols = lambda l, start, width: pl.BlockSpec((pl.Element(1), pl.Element(dm), pl.Element(width)),
                                                lambda i: (l, 0, start), pipeline_mode=pl.Buffered(1))
    seq_params = _params(("arbitrary",))

    for l in range(depth):
        h3 = h.reshape(bsz, lp, dm)
        br0, u = pl.pallas_call(
            functools.partial(_tiled_mixer, _hg_tile, 1, npad=npad, n_inner=ni),
            grid=(nt // ni,),
            in_specs=[row3(dm), vec(dm, l), cols(l, 0, 1536), cols(l, 3072, 512), _layer_spec((8, 256), l),
                      vec(512, l)],
            out_specs=[row3(512), row3(512)],
            out_shape=[jax.ShapeDtypeStruct((bsz, lp, 512), BF16), jax.ShapeDtypeStruct((bsz, lp, 512), F32)],
            scratch_shapes=[pltpu.VMEM((bsz, 2, 256, 128), F32)],
            compiler_params=seq_params, name="hgrn2_mixer",
        )(h3, gmix, wb16, wb16, lb_rows, hg_norm.astype(F32).reshape(depth, 1, 512))

        br1 = pl.pallas_call(
            functools.partial(_tiled_mixer, _ret_tile, 1, npad=npad, n_inner=ni),
            grid=(nt // ni,),
            in_specs=[row3(dm), vec(dm, l), cols(l, 1536, 1536), tile_tab(256), tile_tab(256),
                      fixed(RET_HEADS, SUB_BLOCK, SUB_BLOCK), fixed(SUB_BLOCK, 256), fixed(SUB_BLOCK, 256),
                      fixed(1, 256), vec(512, l)],
            out_specs=row3(512),
            out_shape=jax.ShapeDtypeStruct((bsz, lp, 512), BF16),
            scratch_shapes=[pltpu.VMEM((bsz, 2, 256, 128), F32)],
            compiler_params=seq_params, name="retention_mixer",
        )(h3, gmix, wb16, cos_t, sin_t, dec, xi, zeta, gc, ret_norm.astype(F32).reshape(depth, 1, 512))

        br3 = pl.pallas_call(
            functools.partial(_tiled_mixer, _gdn_tile, 2, npad=npad, n_inner=ni),
            grid=(nt // ni,),
            in_specs=[row3(dm), vec(dm, l), cols(l, 3584, 1536), cols(l, 0, 512), _layer_spec((dm, 128), l),
                      _layer_spec((8, dm), l), _layer_spec((CONV_K, 1536), l), _layer_spec((8, 128), l),
                      _layer_spec((2, 8, T), l), vec(512, l)],
            out_specs=row3(512),
            out_shape=jax.ShapeDtypeStruct((bsz, lp, 512), BF16),
            scratch_shapes=[pltpu.VMEM((bsz, GDN_HEADS, GDN_DK, GDN_DV), F32), pltpu.VMEM((bsz, 8, 1536), F32)],
            compiler_params=seq_params, name="gdn_mixer",
        )(h3, gmix, wb16, w_tail, w_bpad, w_ba, gdn_conv.astype(F32), hp_col, hp_row,
          gdn_norm.astype(F32).reshape(depth, 1, 512))
        br0, br1, br3 = (a.reshape(rows, 512) for a in (br0, br1, br3))

        gb = S5_GROUP_BLOCK
        cb = rchunks // bsz
        gspec = lambda a, b_: pl.BlockSpec((None, gb, a, b_), lambda g, b: (l, g, 0, 0))
        uspec = pl.BlockSpec((cb, SSM_CHUNK, gb * SSM_GROUP), lambda g, b: (b, 0, g))
        cw = SSM_CHUNK * gb * SSM_GROUP
        y5 = pl.pallas_call(
            _s5_kernel,
            grid=(SSM_GROUPS // gb, bsz),
            in_specs=[uspec, gspec(s5_tab.shape[2], 2 * SSM_STATE),
                      gspec(s5_muls.shape[2], 2 * SSM_STATE),
                      pl.BlockSpec((None, 1, gb * SSM_GROUP), lambda g, b: (l, 0, g))],
            out_specs=uspec,
            out_shape=jax.ShapeDtypeStruct((rchunks, SSM_CHUNK, SSM_WIDTH), F32),
            scratch_shapes=[pltpu.VMEM((cw, cw), BF16), pltpu.VMEM((cw, gb * 2 * SSM_STATE), BF16),
                            pltpu.VMEM((cw, gb * 2 * SSM_STATE), BF16)],
            compiler_params=_params(("arbitrary", "arbitrary")), name="s5_mixer",
        )(u.reshape(rchunks, SSM_CHUNK, SSM_WIDTH), s5_tab, s5_muls,
          ssm_d.astype(F32).reshape(depth, 1, SSM_WIDTH))
        y5 = y5.reshape(rows, SSM_WIDTH)

        mrow = lambda w: pl.BlockSpec((tmm, w), lambda i: (i, 0))
        drow = lambda w: pl.BlockSpec((tm, w), lambda i: (i, 0))
        h = pl.pallas_call(
            functools.partial(_merge_kernel, npad=npad, tiles_per_batch=lp // tmm),
            grid=(rows // tmm,),
            in_specs=[mrow(dm), vec(dm, l), mrow(512), mrow(512), mrow(512), mrow(512),
                      cols(l, 512, N_BRANCH * dm), _layer_spec((512, 512), l),
                      _layer_spec((N_BRANCH, BRANCH_WIDTH, dm), l), _layer_spec((dm, dm), l)],
            out_specs=mrow(dm),
            out_shape=jax.ShapeDtypeStruct((rows, dm), F32),
            compiler_params=seq_params, name="merge_out",
        )(h, gmix, br0, br1, y5, br3, w_tail, w_glu16, w_br16, w_out16)

        h = pl.pallas_call(
            functools.partial(_ffn_kernel, n_split=FFN_SPLITS),
            grid=(rows // tm,),
            in_specs=[drow(dm), vec(dm, l), _layer_spec((dm, 2 * dff), l), _layer_spec((dff, dm), l)],
            out_specs=drow(dm),
            out_shape=jax.ShapeDtypeStruct((rows, dm), F32),
            compiler_params=seq_params, name="swiglu_ffn",
        )(h, norm_ffn.astype(F32).reshape(depth, 1, dm), w_gu16, w_down16)

    tf = next(t for t in FINAL_TILES if seq % t == 0)
    ns = seq // tf
    out = pl.pallas_call(
        _final_kernel,
        grid=(bsz, ns),
        in_specs=[pl.BlockSpec((pl.Element(tf), pl.Element(dm)),
                               lambda b, i: ((b * (lp // 8) + (npad + N_META) // 8 + i * (tf // 8)) * 8, 0)),
                  _const_spec((1, dm), lambda b, i: (0, 0))],
        out_specs=pl.BlockSpec((tf, dm), lambda b, i: (b * ns + i, 0)),
        out_shape=jax.ShapeDtypeStruct((bsz * seq, dm), x.dtype),
        compiler_params=_params(("arbitrary", "arbitrary")), name="final_norm",
    )(h, norm_final.astype(F32).reshape(1, dm))
    return out.reshape(bsz, seq, dm)
```
